```python
import jax
import jax.numpy as jnp
from jax import lax
import numpy as np

D_MODEL = 2048
BATCH = 4
SEQ = 4096
DEPTH = 2

Q_BLOCK = 128
NORM_EPS = 1e-6
FOX_HEADS = 4
FOX_HEAD_DIM = 128
SB_HEADS = 4
SB_HEAD_DIM = 128
MLA_HEADS = 4
MLA_Q_LORA = 512
MLA_KV_LORA = 256
MLA_NOPE_DIM = 128
MLA_ROPE_DIM = 64
MLA_V_DIM = 128
ROPE_BASE = 10000.0
POOL_GROUPS = 4
POOL_GROUP_DIM = 128
POOL_WINDOWS = (2, 4, 8, 16)
N_BRANCHES = 4
BRANCH_WIDTH = 512
N_EXPERT_GROUPS = 4
EXPERTS_PER_GROUP = 8
N_EXPERTS = N_EXPERT_GROUPS * EXPERTS_PER_GROUP
EXPERT_TOP_K = 2
D_EXPERT = 1024
EXPERT_BLOCK = 128

FOX_QKV = 3 * FOX_HEADS * FOX_HEAD_DIM
FOX_F = FOX_HEADS
SB_QKV = 3 * SB_HEADS * SB_HEAD_DIM
POOL_IN = POOL_GROUPS * POOL_GROUP_DIM
GATE_IN = N_BRANCHES * D_MODEL
SPLITS = (FOX_QKV, FOX_F, SB_QKV, MLA_Q_LORA, MLA_KV_LORA, MLA_ROPE_DIM, POOL_IN, GATE_IN)
N_IN = FOX_QKV + FOX_F + SB_QKV + MLA_Q_LORA + MLA_KV_LORA + MLA_ROPE_DIM + POOL_IN + GATE_IN

kernel_name = "hybrid_fox_sb_mla_pool_hmoe_adaln"


def _rmsnorm(x, g):
    xf = x.astype(jnp.float32)
    y = xf * lax.rsqrt(jnp.mean(xf * xf, axis=-1, keepdims=True) + NORM_EPS)
    return (y * g.astype(jnp.float32)).astype(x.dtype)


def _heads(t, n_heads):
    b, s, _ = t.shape
    return t.reshape(b, s, n_heads, -1).transpose(0, 2, 1, 3)


def _merge_blocks(o):
    nb, b, h, qb, d = o.shape
    return o.transpose(1, 0, 3, 2, 4).reshape(b, nb * qb, h * d)


def _rope(x, pos):
    half = MLA_ROPE_DIM // 2
    inv = jnp.power(ROPE_BASE, -2.0 * jnp.arange(half, dtype=jnp.float32) / MLA_ROPE_DIM)
    ang = pos.astype(jnp.float32)[:, None] * inv[None, :]
    cos = jnp.cos(ang).astype(x.dtype)
    sin = jnp.sin(ang).astype(x.dtype)
    x1, x2 = x[..., :half], x[..., half:]
    return jnp.concatenate([x1 * cos - x2 * sin, x1 * sin + x2 * cos], axis=-1)


def _causal_softmax_attention(q, k, v, log_decay_cum=None):
    s_len = q.shape[2]
    scale = q.shape[-1] ** -0.5
    kpos = jnp.arange(s_len)

    def block(i):
        t0 = i * Q_BLOCK
        qb = lax.dynamic_slice_in_dim(q, t0, Q_BLOCK, axis=2)
        s = jnp.einsum('bhqd,bhkd->bhqk', qb, k).astype(jnp.float32) * scale
        if log_decay_cum is not None:
            fq = lax.dynamic_slice_in_dim(log_decay_cum, t0, Q_BLOCK, axis=2)
            s = s + fq[..., :, None] - log_decay_cum[..., None, :]
        qpos = t0 + jnp.arange(Q_BLOCK)
        s = jnp.where(qpos[:, None] >= kpos[None, :], s, -jnp.inf)
        p = jax.nn.softmax(s, axis=-1)
        return jnp.einsum('bhqk,bhkd->bhqd', p.astype(v.dtype), v)

    return _merge_blocks(lax.map(block, jnp.arange(s_len // Q_BLOCK)))


def _stick_breaking_attention(q, k, v):
    s_len = q.shape[2]
    scale = q.shape[-1] ** -0.5
    kpos = jnp.arange(s_len)

    def block(i):
        t0 = i * Q_BLOCK
        qb = lax.dynamic_slice_in_dim(q, t0, Q_BLOCK, axis=2)
        z = jnp.einsum('bhqd,bhkd->bhqk', qb, k).astype(jnp.float32) * scale
        qpos = t0 + jnp.arange(Q_BLOCK)
        strict = qpos[:, None] > kpos[None, :]
        log_1m = jnp.where(strict, jax.nn.log_sigmoid(-z), 0.0)
        log_stick = lax.cumsum(log_1m, axis=3, reverse=True) - log_1m
        a = jnp.where(strict, jnp.exp(jax.nn.log_sigmoid(z) + log_stick), 0.0)
        return jnp.einsum('bhqk,bhkd->bhqd', a.astype(v.dtype), v)

    return _merge_blocks(lax.map(block, jnp.arange(s_len // Q_BLOCK)))


def _multiscale_pool(u, w_pool, pool_scale):
    b, s_len, _ = u.shape
    ug = u.astype(jnp.float32).reshape(b, s_len, POOL_GROUPS, POOL_GROUP_DIM)
    cs = jnp.concatenate([jnp.zeros_like(ug[:, :1]), jnp.cumsum(ug, axis=1)], axis=1)
    hi = jnp.arange(1, s_len + 1)
    outs = []
    for g, w in enumerate(POOL_WINDOWS):
        lo = jnp.maximum(hi - w, 0)
        cnt = (hi - lo).astype(jnp.float32)[None, :, None]
        mean = (cs[:, hi, g] - cs[:, lo, g]) / cnt
        outs.append(mean - ug[:, :, g])
    pooled = jnp.stack(outs, axis=2).astype(u.dtype)
    y = jnp.einsum('bsgc,gcd->bsgd', pooled, w_pool).reshape(b, s_len, POOL_IN)
    return y * pool_scale


def _mixer(h, pos, w_in, b_forget, g_q_norm, w_uq, g_kv_norm, w_ukv,
           w_pool, pool_scale, w_branch, w_out):
    b, s_len, d = h.shape
    proj = h @ w_in
    offs = [int(o) for o in np.cumsum(SPLITS)[:-1]]
    fox_qkv, fox_f, sb_qkv, c_q, c_kv, k_rope, pool_u, gate_logits = jnp.split(proj, offs, axis=-1)

    fq, fk, fv = [_heads(t, FOX_HEADS) for t in jnp.split(fox_qkv, 3, axis=-1)]
    log_f = jax.nn.log_sigmoid(fox_f.astype(jnp.float32) + b_forget.astype(jnp.float32))
    log_f_cum = jnp.cumsum(log_f, axis=1).transpose(0, 2, 1)
    y_fox = _causal_softmax_attention(fq, fk, fv, log_f_cum)

    sq, sk, sv = [_heads(t, SB_HEADS) for t in jnp.split(sb_qkv, 3, axis=-1)]
    y_sb = _stick_breaking_attention(sq, sk, sv)

    q = _heads(_rmsnorm(c_q, g_q_norm) @ w_uq, MLA_HEADS)
    q = jnp.concatenate([q[..., :MLA_NOPE_DIM], _rope(q[..., MLA_NOPE_DIM:], pos)], axis=-1)
    kv = _heads(_rmsnorm(c_kv, g_kv_norm) @ w_ukv, MLA_HEADS)
    k_nope, v = kv[..., :MLA_NOPE_DIM], kv[..., MLA_NOPE_DIM:]
    kr = jnp.broadcast_to(_rope(k_rope, pos)[:, None], (b, MLA_HEADS, s_len, MLA_ROPE_DIM))
    k = jnp.concatenate([k_nope, kr], axis=-1)
    y_mla = _causal_softmax_attention(q, k, v)

    y_pool = _multiscale_pool(pool_u, w_pool, pool_scale)

    gates = jax.nn.sigmoid(gate_logits.reshape(b, s_len, N_BRANCHES, d))
    branches = (y_fox, y_sb, y_mla, y_pool)
    merged = gates[:, :, 0] * (branches[0] @ w_branch[0])
    for n in range(1, N_BRANCHES):
        merged = merged + gates[:, :, n] * (branches[n] @ w_branch[n])
    return merged @ w_out


def _hier_moe(h, w_rg, b_rg, w_re, b_re, w_gate, w_up, w_down):
    b, s_len, d = h.shape
    n_tok = b * s_len
    xt = h.reshape(n_tok, d)
    g_logits = (xt @ w_rg).astype(jnp.float32) + b_rg.astype(jnp.float32)
    g_top, g_idx = lax.top_k(g_logits, 1)
    p_group = jnp.exp(g_top[:, 0] - jax.nn.logsumexp(g_logits, axis=-1))
    e_logits = ((xt @ w_re).astype(jnp.float32) + b_re.astype(jnp.float32)).reshape(
        n_tok, N_EXPERT_GROUPS, EXPERTS_PER_GROUP)
    e_logits = jnp.take_along_axis(e_logits, g_idx[:, :, None], axis=1)[:, 0]
    top_p, top_i = lax.top_k(jax.nn.softmax(e_logits, axis=-1), EXPERT_TOP_K)
    weights = p_group[:, None] * top_p / jnp.sum(top_p, axis=-1, keepdims=True)

    expert_id = (g_idx * EXPERTS_PER_GROUP + top_i).reshape(-1)
    tok_id = jnp.repeat(jnp.arange(n_tok, dtype=jnp.int32), EXPERT_TOP_K)
    w_flat = weights.reshape(-1)
    m = expert_id.shape[0]

    order = jnp.argsort(expert_id)
    e_sorted = expert_id[order]
    counts = jnp.bincount(expert_id, length=N_EXPERTS).astype(jnp.int32)
    padded = ((counts + EXPERT_BLOCK - 1) // EXPERT_BLOCK) * EXPERT_BLOCK
    start = jnp.cumsum(counts) - counts
    pend = jnp.cumsum(padded)
    pstart = pend - padded
    dest = pstart[e_sorted] + jnp.arange(m, dtype=jnp.int32) - start[e_sorted]
    n_rows = m + N_EXPERTS * EXPERT_BLOCK
    n_blocks = n_rows // EXPERT_BLOCK
    x_buf = jnp.zeros((n_rows, d), h.dtype).at[dest].set(xt[tok_id[order]])
    w_buf = jnp.zeros((n_rows,), jnp.float32).at[dest].set(w_flat[order])
    t_buf = jnp.zeros((n_rows,), jnp.int32).at[dest].set(tok_id[order])
    block_start = jnp.arange(n_blocks, dtype=jnp.int32) * EXPERT_BLOCK
    block_expert = jnp.minimum(jnp.searchsorted(pend, block_start, side='right'),
                               N_EXPERTS - 1).astype(jnp.int32)

    def expert_block(args):
        xb, e = args
        return (jax.nn.silu(xb @ w_gate[e]) * (xb @ w_up[e])) @ w_down[e]

    y = lax.map(expert_block, (x_buf.reshape(n_blocks, EXPERT_BLOCK, d), block_expert))
    y = y.reshape(n_rows, d) * w_buf[:, None].astype(h.dtype)
    out = jax.ops.segment_sum(y, t_buf, num_segments=n_tok)
    return out.reshape(b, s_len, d)


def setup_inputs(seed: int = 0) -> dict:
    key = jax.random.key(seed)
    k = jax.random.split(key, 24)
    L, D = DEPTH, D_MODEL

    def nrm(kk, shape, scale):
        return jax.random.normal(kk, shape, jnp.float32) * scale

    col_scale = jnp.ones((N_IN,), jnp.float32).at[FOX_QKV:FOX_QKV + FOX_F].set(0.1)
    return {
        "x": nrm(k[0], (BATCH, SEQ, D), 1.0),
        "c": nrm(k[1], (BATCH, D), 1.0),
        "w_mod": nrm(k[2], (L, D, 6 * D), 0.3 * D ** -0.5),
        "b_mod": nrm(k[3], (L, 6 * D), 0.02),
        "g_norm1": 1.0 + nrm(k[4], (L, D), 0.1),
        "g_norm2": 1.0 + nrm(k[5], (L, D), 0.1),
        "w_in": nrm(k[6], (L, D, N_IN), D ** -0.5) * col_scale,
        "b_forget": jax.random.uniform(k[7], (L, FOX_HEADS), jnp.float32, 1.0, 4.0),
        "g_q_norm": 1.0 + nrm(k[8], (L, MLA_Q_LORA), 0.1),
        "w_uq": nrm(k[9], (L, MLA_Q_LORA, MLA_HEADS * (MLA_NOPE_DIM + MLA_ROPE_DIM)), MLA_Q_LORA ** -0.5),
        "g_kv_norm": 1.0 + nrm(k[10], (L, MLA_KV_LORA), 0.1),
        "w_ukv": nrm(k[11], (L, MLA_KV_LORA, MLA_HEADS * (MLA_NOPE_DIM + MLA_V_DIM)), MLA_KV_LORA ** -0.5),
        "w_pool": nrm(k[12], (L, POOL_GROUPS, POOL_GROUP_DIM, POOL_GROUP_DIM), POOL_GROUP_DIM ** -0.5),
        "pool_scale": 1.0 + nrm(k[13], (L, POOL_IN), 0.1),
        "w_branch": nrm(k[14], (L, N_BRANCHES, BRANCH_WIDTH, D), BRANCH_WIDTH ** -0.5),
        "w_out": nrm(k[15], (L, D, D), D ** -0.5),
        "w_route_group": nrm(k[16], (L, D, N_EXPERT_GROUPS), D ** -0.5),
        "b_route_group": nrm(k[17], (L, N_EXPERT_GROUPS), 0.01),
        "w_route_expert": nrm(k[18], (L, D, N_EXPERTS), D ** -0.5),
        "b_route_expert": nrm(k[19], (L, N_EXPERTS), 0.01),
        "w_gate": nrm(k[20], (L, N_EXPERTS, D, D_EXPERT), D ** -0.5),
        "w_up": nrm(k[21], (L, N_EXPERTS, D, D_EXPERT), D ** -0.5),
        "w_down": nrm(k[22], (L, N_EXPERTS, D_EXPERT, D), D_EXPERT ** -0.5),
        "g_final": 1.0 + nrm(k[23], (D,), 0.1),
    }


def reference(x, c, w_mod, b_mod, g_norm1, g_norm2, w_in, b_forget, g_q_norm, w_uq,
              g_kv_norm, w_ukv, w_pool, pool_scale, w_branch, w_out, w_route_group,
              b_route_group, w_route_expert, b_route_expert, w_gate, w_up, w_down, g_final):
    s_len = x.shape[1]
    pos = jnp.arange(s_len)
    c_act = jax.nn.silu(c)
    for l in range(DEPTH):
        mod = c_act @ w_mod[l] + b_mod[l]
        shift1, scale1, gate1, shift2, scale2, gate2 = [m[:, None, :] for m in jnp.split(mod, 6, axis=-1)]
        h = _rmsnorm(x, g_norm1[l]) * (1.0 + scale1) + shift1
        x = x + gate1 * _mixer(h, pos, w_in[l], b_forget[l], g_q_norm[l], w_uq[l],
                               g_kv_norm[l], w_ukv[l], w_pool[l], pool_scale[l],
                               w_branch[l], w_out[l])
        h = _rmsnorm(x, g_norm2[l]) * (1.0 + scale2) + shift2
        x = x + gate2 * _hier_moe(h, w_route_group[l], b_route_group[l], w_route_expert[l],
                                  b_route_expert[l], w_gate[l], w_up[l], w_down[l])
    return _rmsnorm(x, g_final)
```

```python
import functools

import numpy as np
import jax
import jax.numpy as jnp
from jax import lax
from jax.experimental import pallas as pl
from jax.experimental.pallas import tpu as pltpu

F32 = jnp.float32
BF16 = jnp.bfloat16

NORM_EPS = 1e-6
N_HEADS = 4
HEAD_DIM = 128
MLA_Q_LORA = 512
MLA_KV_LORA = 256
MLA_NOPE_DIM = 128
MLA_ROPE_DIM = 64
ROPE_BASE = 10000.0
POOL_WINDOWS = (2, 4, 8, 16)
POOL_HALO = 16
N_BRANCHES = 4
BRANCH_WIDTH = 512
N_EXPERT_GROUPS = 4
EXPERTS_PER_GROUP = 8
N_EXPERTS = N_EXPERT_GROUPS * EXPERTS_PER_GROUP
EXPERT_TOP_K = 2
SMALL_WIDTH = 256

LANE = 128
V7X_VMEM_BYTES = 64 * 1024 * 1024
VMEM_LIMIT = V7X_VMEM_BYTES - 12 * 1024 * 1024


def _cparams(*sem):
    return pltpu.CompilerParams(dimension_semantics=sem, vmem_limit_bytes=VMEM_LIMIT)


def _tile(n, pref):
    t = min(n, pref)
    assert n % t == 0, (n, pref)
    return t


def _mod_body(c_ref, w_ref, b_ref, o_ref):
    c = c_ref[...]
    ca = c * jax.nn.sigmoid(c)
    acc = jnp.dot(ca.astype(BF16), w_ref[0].astype(BF16), preferred_element_type=F32)
    o_ref[0] = acc + b_ref[0]


def _modulation(c_pad, w_mod, b_mod):
    depth, d, n = w_mod.shape
    tn = _tile(n, 1024)
    return pl.pallas_call(
        _mod_body,
        grid=(depth, n // tn),
        in_specs=[
            pl.BlockSpec((8, d), lambda l, j: (0, 0)),
            pl.BlockSpec((1, d, tn), lambda l, j: (l, 0, j)),
            pl.BlockSpec((1, 1, tn), lambda l, j: (l, 0, j)),
        ],
        out_specs=pl.BlockSpec((1, 8, tn), lambda l, j: (l, 0, j)),
        out_shape=jax.ShapeDtypeStruct((depth, 8, n), F32),
        compiler_params=_cparams("parallel", "parallel"),
        name="adaln_modulation",
    )(c_pad, w_mod, b_mod.reshape(depth, 1, n))


def _norm_small_body(x_ref, g_ref, sc_ref, sh_ref, w_ref, h_ref, s_ref):
    x = x_ref[...]
    ms = jnp.mean(x * x, axis=-1, keepdims=True)
    y = x * lax.rsqrt(ms + NORM_EPS) * g_ref[...]
    h = (y * (1.0 + sc_ref[0]) + sh_ref[0]).astype(BF16)
    h_ref[...] = h
    s_ref[...] = jnp.dot(h, w_ref[...], preferred_element_type=F32)


def _norm_small(x2, g, scale, shift, w_small, seq):
    n, d = x2.shape
    tm = _tile(seq, 512)
    per_b = seq // tm
    ws = w_small.shape[1]
    return pl.pallas_call(
        _norm_small_body,
        grid=(n // tm,),
        in_specs=[
            pl.BlockSpec((tm, d), lambda i: (i, 0)),
            pl.BlockSpec((1, d), lambda i: (0, 0)),
            pl.BlockSpec((1, 1, d), lambda i: (i // per_b, 0, 0)),
            pl.BlockSpec((1, 1, d), lambda i: (i // per_b, 0, 0)),
            pl.BlockSpec((d, ws), lambda i: (0, 0)),
        ],
        out_specs=[
            pl.BlockSpec((tm, d), lambda i: (i, 0)),
            pl.BlockSpec((tm, ws), lambda i: (i, 0)),
        ],
        out_shape=[
            jax.ShapeDtypeStruct((n, d), BF16),
            jax.ShapeDtypeStruct((n, ws), F32),
        ],
        compiler_params=_cparams("parallel"),
        name="norm_modulate_small_proj",
    )(x2, g.reshape(1, d), scale, shift, w_small.astype(BF16))


def _mm_body(a_ref, w_ref, o_ref, *, act):
    acc = jnp.dot(a_ref[...], w_ref[...], preferred_element_type=F32)
    if act == "sigmoid":
        acc = jax.nn.sigmoid(acc)
    o_ref[...] = acc.astype(o_ref.dtype)


def _matmul(a, w, out_dtype, act=None, tm=1024, tn=1024, name="matmul"):
    m, k = a.shape
    n = w.shape[1]
    tm = _tile(m, tm)
    tn = _tile(n, tn)
    return pl.pallas_call(
        functools.partial(_mm_body, act=act),
        grid=(m // tm, n // tn),
        in_specs=[
            pl.BlockSpec((tm, k), lambda i, j: (i, 0)),
            pl.BlockSpec((k, tn), lambda i, j: (0, j)),
        ],
        out_specs=pl.BlockSpec((tm, tn), lambda i, j: (i, j)),
        out_shape=jax.ShapeDtypeStruct((m, n), out_dtype),
        compiler_params=_cparams("parallel", "parallel"),
        name=name,
    )(a, w)


def _mm_residual_body(a_ref, w_ref, x_ref, gate_ref, o_ref):
    acc = jnp.dot(a_ref[...], w_ref[...], preferred_element_type=F32)
    o_ref[...] = x_ref[...] + gate_ref[0] * acc


def _matmul_residual(a, w, x2, gate, seq, tm=1024, tn=1024):
    m, k = a.shape
    n = w.shape[1]
    tm = _tile(seq, tm)
    tn = _tile(n, tn)
    per_b = seq // tm
    return pl.pallas_call(
        _mm_residual_body,
        grid=(m // tm, n // tn),
        in_specs=[
            pl.BlockSpec((tm, k), lambda i, j: (i, 0)),
            pl.BlockSpec((k, tn), lambda i, j: (0, j)),
            pl.BlockSpec((tm, tn), lambda i, j: (i, j)),
            pl.BlockSpec((1, 1, tn), lambda i, j: (i // per_b, 0, j)),
        ],
        out_specs=pl.BlockSpec((tm, tn), lambda i, j: (i, j)),
        out_shape=jax.ShapeDtypeStruct((m, n), F32),
        compiler_params=_cparams("parallel", "parallel"),
        name="out_proj_residual",
    )(a, w, x2, gate)


def _latent_norm(a_ref, g_ref):
    a = a_ref[...]
    ms = jnp.mean(a * a, axis=-1, keepdims=True)
    return (a * lax.rsqrt(ms + NORM_EPS) * g_ref[...]).astype(BF16)


def _mla_q_body(a_ref, g_ref, w_ref, cos_ref, sin_ref, nope_ref, rope_ref):
    acc = jnp.dot(_latent_norm(a_ref, g_ref), w_ref[...], preferred_element_type=F32)
    w = nope_ref.shape[1]
    nope_ref[...] = acc[:, :w].astype(BF16)
    rope_ref[...] = (acc[:, w:2 * w] * cos_ref[...] + acc[:, 2 * w:] * sin_ref[...]).astype(BF16)


def _mla_q(latent, col_block, g, w_q, cos_t, sin_t, seq):
    n = latent.shape[0]
    k = w_q.shape[0]
    w = w_q.shape[1] // 3
    tm = _tile(seq, 1024)
    per_b = seq // tm
    return pl.pallas_call(
        _mla_q_body,
        grid=(n // tm,),
        in_specs=[
            pl.BlockSpec((tm, k), lambda i: (i, col_block)),
            pl.BlockSpec((1, k), lambda i: (0, 0)),
            pl.BlockSpec((k, 3 * w), lambda i: (0, 0)),
            pl.BlockSpec((tm, w), lambda i: (i % per_b, 0)),
            pl.BlockSpec((tm, w), lambda i: (i % per_b, 0)),
        ],
        out_specs=[
            pl.BlockSpec((tm, w), lambda i: (i, 0)),
            pl.BlockSpec((tm, w), lambda i: (i, 0)),
        ],
        out_shape=[
            jax.ShapeDtypeStruct((n, w), BF16),
            jax.ShapeDtypeStruct((n, w), BF16),
        ],
        compiler_params=_cparams("parallel"),
        name="mla_q_up_rope",
    )(latent, g.reshape(1, k), w_q, cos_t, sin_t)


def _mla_kv_body(a_ref, g_ref, w_ref, o_ref):
    o_ref[...] = jnp.dot(_latent_norm(a_ref, g_ref), w_ref[...],
                         preferred_element_type=F32).astype(BF16)


def _mla_kv(latent, col_block, g, w_kv, seq):
    n = latent.shape[0]
    k, nout = w_kv.shape
    tm = _tile(seq, 1024)
    return pl.pallas_call(
        _mla_kv_body,
        grid=(n // tm,),
        in_specs=[
            pl.BlockSpec((tm, k), lambda i: (i, col_block)),
            pl.BlockSpec((1, k), lambda i: (0, 0)),
            pl.BlockSpec((k, nout), lambda i: (0, 0)),
        ],
        out_specs=pl.BlockSpec((tm, nout), lambda i: (i, 0)),
        out_shape=jax.ShapeDtypeStruct((n, nout), BF16),
        compiler_params=_cparams("parallel"),
        name="mla_kv_up",
    )(latent, g.reshape(1, k), w_kv)


def _softmax_attn_body(*refs, tq, tk, has_extra, has_bias):
    refs = list(refs)
    qm_ref = refs.pop(0)
    qe_ref = refs.pop(0) if has_extra else None
    km_ref = refs.pop(0)
    ke_ref = refs.pop(0) if has_extra else None
    v_ref = refs.pop(0)
    kb_ref = refs.pop(0) if has_bias else None
    o_ref, m_ref, l_ref, acc_ref = refs
    qi = pl.program_id(2)
    q = qm_ref[0]
    if has_extra:
        q = jnp.concatenate([q, qe_ref[0]], axis=-1)
    m_ref[...] = jnp.full(m_ref.shape, -jnp.inf, F32)
    l_ref[...] = jnp.zeros(l_ref.shape, F32)
    acc_ref[...] = jnp.zeros(acc_ref.shape, F32)
    chunks_per_q = tq // tk
    n_full = qi * chunks_per_q

    def chunk(c, diag):
        k0 = pl.multiple_of(c * tk, tk)
        k = km_ref[0, pl.ds(k0, tk), :]
        if has_extra:
            k = jnp.concatenate([k, ke_ref[0, pl.ds(k0, tk), :]], axis=-1)
        s = lax.dot_general(q, k, (((1,), (1,)), ((), ())), preferred_element_type=F32)
        if has_bias:
            s = s + kb_ref[0, 0, :, pl.ds(k0, tk)]
        if diag is not None:
            row = lax.broadcasted_iota(jnp.int32, (tq, tk), 0)
            col = lax.broadcasted_iota(jnp.int32, (tq, tk), 1) + diag * tk
            s = jnp.where(row >= col, s, -jnp.inf)
        m_prev = m_ref[...]
        m_new = jnp.maximum(m_prev, jnp.max(s, axis=-1, keepdims=True))
        alpha = jnp.exp(m_prev - m_new)
        p = jnp.exp(s - m_new)
        l_ref[...] = alpha * l_ref[...] + jnp.sum(p, axis=-1, keepdims=True)
        acc_ref[...] = alpha * acc_ref[...] + jnp.dot(
            p.astype(BF16), v_ref[0, pl.ds(k0, tk), :], preferred_element_type=F32)
        m_ref[...] = m_new

    def full_chunk(c, carry):
        chunk(c, None)
        return carry

    lax.fori_loop(0, n_full, full_chunk, 0)
    for d in range(chunks_per_q):
        chunk(n_full + d, d)
    o_ref[0] = (acc_ref[...] / l_ref[...]).astype(o_ref.dtype)


def _softmax_attention(qm, km, v, maps, batch, seq, extra=None, key_bias=None):
    tq = _tile(seq, 512)
    tk = _tile(seq, 512)

    def qspec(f):
        return pl.BlockSpec((1, tq, HEAD_DIM), lambda b, h, i: (f(b, h)[0], i, f(b, h)[1]))

    def kspec(f):
        return pl.BlockSpec((1, seq, HEAD_DIM), lambda b, h, i: (f(b, h)[0], 0, f(b, h)[1]))

    fqm, fkm, fv = maps[:3]
    args, specs = [qm], [qspec(fqm)]
    if extra is not None:
        args.append(extra[0])
        specs.append(qspec(maps[3]))
    args.append(km)
    specs.append(kspec(fkm))
    if extra is not None:
        args.append(extra[1])
        specs.append(kspec(maps[4]))
    args.append(v)
    specs.append(kspec(fv))
    if key_bias is not None:
        args.append(key_bias)
        specs.append(pl.BlockSpec((1, 1, 1, seq), lambda b, h, i: (b, h, 0, 0)))

    return pl.pallas_call(
        functools.partial(_softmax_attn_body, tq=tq, tk=tk, has_extra=extra is not None,
                          has_bias=key_bias is not None),
        grid=(batch, N_HEADS, seq // tq),
        in_specs=specs,
        out_specs=pl.BlockSpec((1, tq, HEAD_DIM), lambda b, h, i: (b, i, h)),
        out_shape=jax.ShapeDtypeStruct((batch, seq, N_HEADS * HEAD_DIM), BF16),
        scratch_shapes=[
            pltpu.VMEM((tq, 1), F32),
            pltpu.VMEM((tq, 1), F32),
            pltpu.VMEM((tq, HEAD_DIM), F32),
        ],
        compiler_params=_cparams("parallel", "parallel", "arbitrary"),
        name="causal_softmax_attention",
    )(*args)


def _sb_attn_body(q_ref, k_ref, v_ref, tri_ref, o_ref, r_ref, acc_ref, *, tq, tk):
    qi = pl.program_id(2)
    q = q_ref[0]
    r_ref[...] = jnp.zeros(r_ref.shape, F32)
    acc_ref[...] = jnp.zeros(acc_ref.shape, F32)
    chunks_per_q = tq // tk
    n_full = qi * chunks_per_q

    def chunk(c, diag):
        k0 = pl.multiple_of(c * tk, tk)
        z = lax.dot_general(q, k_ref[0, pl.ds(k0, tk), :], (((1,), (1,)), ((), ())),
                            preferred_element_type=F32)
        log_1m = -(jnp.maximum(z, 0.0) + jnp.log(1.0 + jnp.exp(-jnp.abs(z))))
        log_beta = z + log_1m
        if diag is not None:
            row = lax.broadcasted_iota(jnp.int32, (tq, tk), 0)
            col = lax.broadcasted_iota(jnp.int32, (tq, tk), 1) + diag * tk
            strict = row > col
            log_1m = jnp.where(strict, log_1m, 0.0)
        hi = log_1m.astype(BF16)
        lo = (log_1m - hi.astype(F32)).astype(BF16)
        tri = tri_ref[...]
        e = jnp.dot(hi, tri, preferred_element_type=F32) + jnp.dot(lo, tri, preferred_element_type=F32)
        a = jnp.exp(log_beta + e[:, :tk] + r_ref[...])
        if diag is not None:
            a = jnp.where(strict, a, 0.0)
        acc_ref[...] += jnp.dot(a.astype(BF16), v_ref[0, pl.ds(k0, tk), :],
                                preferred_element_type=F32)
        r_ref[...] += e[:, tk:tk + 1]

    for d in reversed(range(chunks_per_q)):
        chunk(n_full + d, d)

    def full_chunk(i, carry):
        chunk(n_full - 1 - i, None)
        return carry

    lax.fori_loop(0, n_full, full_chunk, 0)
    o_ref[0] = acc_ref[...].astype(o_ref.dtype)


def _sb_tri(tk):
    j_src = np.arange(tk)[:, None]
    j_dst = np.arange(tk + LANE)[None, :]
    return jnp.asarray((j_src > j_dst) | (j_dst >= tk), dtype=BF16)


def _sb_attention(qkv, q_blk, k_blk, v_blk, batch, seq):
    tq = _tile(seq, 512)
    tk = _tile(seq, 256)
    return pl.pallas_call(
        functools.partial(_sb_attn_body, tq=tq, tk=tk),
        grid=(batch, N_HEADS, seq // tq),
        in_specs=[
            pl.BlockSpec((1, tq, HEAD_DIM), lambda b, h, i: (b, i, q_blk + h)),
            pl.BlockSpec((1, seq, HEAD_DIM), lambda b, h, i: (b, 0, k_blk + h)),
            pl.BlockSpec((1, seq, HEAD_DIM), lambda b, h, i: (b, 0, v_blk + h)),
            pl.BlockSpec((tk, tk + LANE), lambda b, h, i: (0, 0)),
        ],
        out_specs=pl.BlockSpec((1, tq, HEAD_DIM), lambda b, h, i: (b, i, h)),
        out_shape=jax.ShapeDtypeStruct((batch, seq, N_HEADS * HEAD_DIM), BF16),
        scratch_shapes=[
            pltpu.VMEM((tq, 1), F32),
            pltpu.VMEM((tq, HEAD_DIM), F32),
        ],
        compiler_params=_cparams("parallel", "parallel", "arbitrary"),
        name="stick_breaking_attention",
    )(qkv, qkv, qkv, _sb_tri(tk))


def _pool_body(u_ref, halo_ref, w_ref, sc_ref, o_ref, ext_ref, *, ts):
    i = pl.program_id(1)
    halo = jnp.where(i > 0, halo_ref[0], 0.0)
    ext_ref[:POOL_HALO, :] = halo
    ext_ref[POOL_HALO:, :] = u_ref[0]
    pos = i * ts + lax.broadcasted_iota(jnp.int32, (ts, 1), 0)
    outs = []
    for g, win in enumerate(POOL_WINDOWS):
        lanes = slice(g * LANE, (g + 1) * LANE)
        tok = ext_ref[POOL_HALO:, lanes]
        tot = tok
        for back in range(1, win):
            tot = tot + ext_ref[POOL_HALO - back:POOL_HALO - back + ts, lanes]
        cnt = jnp.minimum(pos + 1, win).astype(F32)
        pooled = (tot / cnt - tok).astype(BF16)
        outs.append(jnp.dot(pooled, w_ref[g], preferred_element_type=F32))
    y = jnp.concatenate(outs, axis=-1) * sc_ref[...]
    o_ref[0] = y.astype(o_ref.dtype)


def _pool_mixer(latent3, col_block, w_pool, pool_scale, batch, seq):
    width = len(POOL_WINDOWS) * LANE
    ts = _tile(seq, 512)
    halo_per_tile = ts // POOL_HALO
    return pl.pallas_call(
        functools.partial(_pool_body, ts=ts),
        grid=(batch, seq // ts),
        in_specs=[
            pl.BlockSpec((1, ts, width), lambda b, i: (b, i, col_block)),
            pl.BlockSpec((1, POOL_HALO, width),
                         lambda b, i: (b, jnp.maximum(i * halo_per_tile - 1, 0), col_block)),
            pl.BlockSpec((len(POOL_WINDOWS), LANE, LANE), lambda b, i: (0, 0, 0)),
            pl.BlockSpec((1, width), lambda b, i: (0, 0)),
        ],
        out_specs=pl.BlockSpec((1, ts, width), lambda b, i: (b, i, 0)),
        out_shape=jax.ShapeDtypeStruct((batch, seq, width), BF16),
        scratch_shapes=[pltpu.VMEM((ts + POOL_HALO, width), F32)],
        compiler_params=_cparams("parallel", "arbitrary"),
        name="multiscale_pool",
    )(latent3, latent3, w_pool.astype(BF16), pool_scale.reshape(1, width))


def _merge_body(y0, y1, y2, y3, g0, g1, g2, g3, wb_ref, o_ref):
    acc = None
    for n, (y, g) in enumerate(((y0, g0), (y1, g1), (y2, g2), (y3, g3))):
        t = g[...].astype(F32) * jnp.dot(y[...], wb_ref[n], preferred_element_type=F32)
        acc = t if acc is None else acc + t
    o_ref[...] = acc.astype(o_ref.dtype)


def _merge(ys, gates, w_branch):
    n = ys[0].shape[0]
    d = w_branch.shape[2]
    tm = _tile(n, 1024)
    tn = _tile(d, 512)
    nj = d // tn
    y_spec = pl.BlockSpec((tm, BRANCH_WIDTH), lambda i, j: (i, 0))

    def g_spec(b):
        return pl.BlockSpec((tm, tn), lambda i, j: (i, b * nj + j))

    return pl.pallas_call(
        _merge_body,
        grid=(n // tm, nj),
        in_specs=[y_spec] * N_BRANCHES + [g_spec(b) for b in range(N_BRANCHES)] + [
            pl.BlockSpec((N_BRANCHES, BRANCH_WIDTH, tn), lambda i, j: (0, 0, j))],
        out_specs=pl.BlockSpec((tm, tn), lambda i, j: (i, j)),
        out_shape=jax.ShapeDtypeStruct((n, d), BF16),
        compiler_params=_cparams("parallel", "parallel"),
        name="gated_branch_merge",
    )(*ys, gates, gates, gates, gates, w_branch)


def _moe_body(be_ref, nb_ref, x_ref, wg_ref, wu_ref, wd_ref, rw_ref, o_ref):
    blk = pl.program_id(0)

    @pl.when(blk < nb_ref[0])
    def _():
        x = x_ref[...]
        g = jnp.dot(x, wg_ref[0], preferred_element_type=F32)
        u = jnp.dot(x, wu_ref[0], preferred_element_type=F32)
        a = (g * jax.nn.sigmoid(g) * u).astype(BF16)
        y = jnp.dot(a, wd_ref[0], preferred_element_type=F32)
        o_ref[...] = y * rw_ref[...]

    @pl.when(blk >= nb_ref[0])
    def _():
        o_ref[...] = jnp.zeros(o_ref.shape, o_ref.dtype)


def _moe_ffn(x_buf, row_w, block_expert, n_used, w_gate, w_up, w_down, tb):
    n_rows, d = x_buf.shape
    f = w_gate.shape[2]
    return pl.pallas_call(
        _moe_body,
        grid_spec=pltpu.PrefetchScalarGridSpec(
            num_scalar_prefetch=2,
            grid=(n_rows // tb,),
            in_specs=[
                pl.BlockSpec((tb, d), lambda i, be, nb: (i, 0)),
                pl.BlockSpec((1, d, f), lambda i, be, nb: (be[i], 0, 0)),
                pl.BlockSpec((1, d, f), lambda i, be, nb: (be[i], 0, 0)),
                pl.BlockSpec((1, f, d), lambda i, be, nb: (be[i], 0, 0)),
                pl.BlockSpec((tb, 1), lambda i, be, nb: (i, 0)),
            ],
            out_specs=pl.BlockSpec((tb, d), lambda i, be, nb: (i, 0)),
        ),
        out_shape=jax.ShapeDtypeStruct((n_rows, d), F32),
        compiler_params=_cparams("arbitrary"),
        name="expert_swiglu",
    )(block_expert, n_used, x_buf, w_gate, w_up, w_down, row_w)


def _final_norm_body(x_ref, g_ref, o_ref):
    x = x_ref[...]
    ms = jnp.mean(x * x, axis=-1, keepdims=True)
    o_ref[...] = x * lax.rsqrt(ms + NORM_EPS) * g_ref[...]


def _final_norm(x2, g):
    n, d = x2.shape
    tm = _tile(n, 512)
    return pl.pallas_call(
        _final_norm_body,
        grid=(n // tm,),
        in_specs=[pl.BlockSpec((tm, d), lambda i: (i, 0)), pl.BlockSpec((1, d), lambda i: (0, 0))],
        out_specs=pl.BlockSpec((tm, d), lambda i: (i, 0)),
        out_shape=jax.ShapeDtypeStruct((n, d), F32),
        compiler_params=_cparams("parallel"),
        name="final_rmsnorm",
    )(x2, g.reshape(1, d))


def _rope_tables(seq):
    half = MLA_ROPE_DIM // 2
    inv = jnp.power(ROPE_BASE, -2.0 * jnp.arange(half, dtype=F32) / MLA_ROPE_DIM)
    ang = jnp.arange(seq).astype(F32)[:, None] * inv[None, :]
    cos, sin = jnp.cos(ang), jnp.sin(ang)
    c64 = jnp.concatenate([cos, cos], axis=-1)
    s64 = jnp.concatenate([-sin, sin], axis=-1)
    return c64, s64


def _swap_halves(w):
    half = w.shape[-1] // 2
    return jnp.concatenate([w[..., half:], w[..., :half]], axis=-1)


def _pad_lanes(a, width):
    return jnp.pad(a, [(0, 0)] * (a.ndim - 1) + [(0, width - a.shape[-1])])


def _mixer_layer(x2, h, small, batch, seq, prm):
    n, d = x2.shape
    (w_qkv, w_lat, w_gates, b_forget, g_q, w_q, g_kv, w_kv, w_pool, pool_scale, w_branch, w_out,
     gate1) = prm
    c64, s64 = _rope_tables(seq)

    qkv = _matmul(h, w_qkv, BF16, name="fox_sb_qkv_proj").reshape(batch, seq, -1)
    latent = _matmul(h, w_lat, F32, tn=640, name="latent_pool_proj")
    gates = _matmul(h, w_gates, BF16, act="sigmoid", name="branch_gate_proj")

    fox_f = small[:, 2 * MLA_ROPE_DIM:2 * MLA_ROPE_DIM + N_HEADS].reshape(batch, seq, N_HEADS)
    log_f_cum = jnp.cumsum(jax.nn.log_sigmoid(fox_f + b_forget), axis=1)
    k_bias = -log_f_cum.transpose(0, 2, 1)[:, :, None, :]
    nh = N_HEADS
    y_fox = _softmax_attention(
        qkv, qkv, qkv,
        (lambda b, hh: (b, hh), lambda b, hh: (b, nh + hh), lambda b, hh: (b, 2 * nh + hh)),
        batch, seq, key_bias=k_bias)

    y_sb = _sb_attention(qkv, 3 * nh, 4 * nh, 5 * nh, batch, seq)

    cos_t = jnp.tile(_pad_lanes(c64, HEAD_DIM), (1, nh))
    sin_t = jnp.tile(_pad_lanes(s64, HEAD_DIM), (1, nh))
    q_nope, q_rope = _mla_q(latent, 0, g_q, w_q, cos_t, sin_t, seq)
    kv = _mla_kv(latent, 4, g_kv, w_kv, seq).reshape(batch, seq, -1)
    kr = small[:, :MLA_ROPE_DIM] * jnp.tile(c64, (batch, 1)) \
        + small[:, MLA_ROPE_DIM:2 * MLA_ROPE_DIM] * jnp.tile(s64, (batch, 1))
    kr = _pad_lanes(kr.astype(BF16), HEAD_DIM).reshape(batch, seq, HEAD_DIM)
    y_mla = _softmax_attention(
        q_nope.reshape(batch, seq, -1), kv, kv,
        (lambda b, hh: (b, hh), lambda b, hh: (b, hh), lambda b, hh: (b, nh + hh),
         lambda b, hh: (b, hh), lambda b, hh: (b, 0)),
        batch, seq, extra=(q_rope.reshape(batch, seq, -1), kr))

    y_pool = _pool_mixer(latent.reshape(batch, seq, -1), 1, w_pool, pool_scale, batch, seq)

    ys = [y.reshape(n, BRANCH_WIDTH) for y in (y_fox, y_sb, y_mla, y_pool)]
    merged = _merge(ys, gates, w_branch)
    return _matmul_residual(merged, w_out, x2, gate1, seq)


def _moe_layer(x2, h, logits, gate2, b_rg, b_re, w_gate, w_up, w_down, batch, seq):
    n, d = x2.shape
    tb = 256
    g_logits = logits[:, :N_EXPERT_GROUPS] + b_rg
    g_top, g_idx = lax.top_k(g_logits, 1)
    p_group = jnp.exp(g_top[:, 0] - jax.nn.logsumexp(g_logits, axis=-1))
    e_logits = (logits[:, N_EXPERT_GROUPS:N_EXPERT_GROUPS + N_EXPERTS] + b_re).reshape(
        n, N_EXPERT_GROUPS, EXPERTS_PER_GROUP)
    e_logits = jnp.take_along_axis(e_logits, g_idx[:, :, None], axis=1)[:, 0]
    top_p, top_i = lax.top_k(jax.nn.softmax(e_logits, axis=-1), EXPERT_TOP_K)
    weights = p_group[:, None] * top_p / jnp.sum(top_p, axis=-1, keepdims=True)

    expert_id = (g_idx * EXPERTS_PER_GROUP + top_i).reshape(-1).astype(jnp.int32)
    m = expert_id.shape[0]
    w_flat = weights.reshape(-1)

    order = jnp.argsort(expert_id).astype(jnp.int32)
    counts = jnp.sum(expert_id[:, None] == jnp.arange(N_EXPERTS, dtype=jnp.int32)[None, :],
                     axis=0).astype(jnp.int32)
    padded = ((counts + tb - 1) // tb) * tb
    start = jnp.cumsum(counts) - counts
    pend = jnp.cumsum(padded)
    pstart = pend - padded
    n_rows = m + N_EXPERTS * tb
    n_blocks = n_rows // tb
    block_start = jnp.arange(n_blocks, dtype=jnp.int32) * tb
    n_used = (pend[-1] // tb).astype(jnp.int32)
    block_expert = jnp.minimum(jnp.searchsorted(pend, block_start, side='right'),
                               N_EXPERTS - 1).astype(jnp.int32)
    last_used = block_expert[jnp.maximum(n_used - 1, 0)]
    block_expert = jnp.where(jnp.arange(n_blocks) < n_used, block_expert, last_used)

    row = jnp.arange(n_rows, dtype=jnp.int32)
    row_e = jnp.repeat(block_expert, tb)
    within = row - pstart[row_e]
    valid = (within < counts[row_e]) & (row < pend[-1])
    src_sorted = jnp.clip(start[row_e] + within, 0, m - 1)
    src_assign = order[src_sorted]
    row_tok = jnp.where(valid, src_assign // EXPERT_TOP_K, 0)
    row_w = jnp.where(valid, w_flat[src_assign], 0.0).astype(F32)

    e_sorted = expert_id[order]
    dest_sorted = pstart[e_sorted] + jnp.arange(m, dtype=jnp.int32) - start[e_sorted]
    dest = jnp.zeros((m,), jnp.int32).at[order].set(dest_sorted)

    x_buf = jnp.take(h, row_tok, axis=0)
    y = _moe_ffn(x_buf, row_w.reshape(n_rows, 1), block_expert, n_used.reshape(1),
                 w_gate, w_up, w_down, tb)
    dest = dest.reshape(n, EXPERT_TOP_K)
    moe = jnp.take(y, dest[:, 0], axis=0) + jnp.take(y, dest[:, 1], axis=0)
    gate_rows = jnp.repeat(gate2[:, 0, :], seq, axis=0)
    return x2 + gate_rows * moe


def kernel(x, c, w_mod, b_mod, g_norm1, g_norm2, w_in, b_forget, g_q_norm, w_uq, g_kv_norm, w_ukv, w_pool, pool_scale, w_branch, w_out, w_route_group, b_route_group, w_route_expert, b_route_expert, w_gate, w_up, w_down, g_final):
    batch, seq, d = x.shape
    depth = w_mod.shape[0]
    n = batch * seq
    nh = N_HEADS

    c_pad = jnp.pad(c, ((0, 8 - batch), (0, 0)))
    mod = _modulation(c_pad, w_mod, b_mod)[:, :batch]

    hq = nh * HEAD_DIM
    o_fox, o_f = 0, 3 * hq
    o_sb = o_f + nh
    o_cq = o_sb + 3 * hq
    o_ckv = o_cq + MLA_Q_LORA
    o_kr = o_ckv + MLA_KV_LORA
    o_pool = o_kr + MLA_ROPE_DIM
    o_gate = o_pool + len(POOL_WINDOWS) * LANE
    attn_scale = HEAD_DIM ** -0.5
    mla_scale = (MLA_NOPE_DIM + MLA_ROPE_DIM) ** -0.5

    x2 = x.reshape(n, d)
    for l in range(depth):
        m6 = mod[l].reshape(batch, 6, 1, d)
        shift1, scale1, gate1, shift2, scale2, gate2 = [m6[:, t] for t in range(6)]
        wi = w_in[l]
        w_qkv = jnp.concatenate([
            wi[:, o_fox:o_fox + hq] * attn_scale, wi[:, o_fox + hq:o_fox + 3 * hq],
            wi[:, o_sb:o_sb + hq] * attn_scale, wi[:, o_sb + hq:o_sb + 3 * hq]], axis=1).astype(BF16)
        w_lat = jnp.concatenate([wi[:, o_cq:o_ckv], wi[:, o_pool:o_gate], wi[:, o_ckv:o_kr]],
                                axis=1).astype(BF16)
        w_gates = wi[:, o_gate:].astype(BF16)
        w_kr = wi[:, o_kr:o_pool]
        w_small1 = _pad_lanes(jnp.concatenate([w_kr, _swap_halves(w_kr), wi[:, o_f:o_sb]], axis=1),
                              SMALL_WIDTH)

        wq = w_uq[l].reshape(MLA_Q_LORA, nh, MLA_NOPE_DIM + MLA_ROPE_DIM) * mla_scale
        wq_rope = wq[:, :, MLA_NOPE_DIM:]
        w_q = jnp.concatenate([
            wq[:, :, :MLA_NOPE_DIM].reshape(MLA_Q_LORA, -1),
            _pad_lanes(wq_rope, HEAD_DIM).reshape(MLA_Q_LORA, -1),
            _pad_lanes(_swap_halves(wq_rope), HEAD_DIM).reshape(MLA_Q_LORA, -1)], axis=1).astype(BF16)
        wkv = w_ukv[l].reshape(MLA_KV_LORA, nh, 2 * HEAD_DIM)
        w_kv = jnp.concatenate([wkv[:, :, :HEAD_DIM].reshape(MLA_KV_LORA, -1),
                                wkv[:, :, HEAD_DIM:].reshape(MLA_KV_LORA, -1)], axis=1).astype(BF16)

        h, small = _norm_small(x2, g_norm1[l], scale1, shift1, w_small1, seq)
        prm = (w_qkv, w_lat, w_gates, b_forget[l], g_q_norm[l], w_q, g_kv_norm[l], w_kv,
               w_pool[l], pool_scale[l], w_branch[l].astype(BF16), w_out[l].astype(BF16), gate1)
        x2 = _mixer_layer(x2, h, small, batch, seq, prm)

        w_small2 = _pad_lanes(jnp.concatenate([w_route_group[l], w_route_expert[l]], axis=1),
                              SMALL_WIDTH)
        h, logits = _norm_small(x2, g_norm2[l], scale2, shift2, w_small2, seq)
        x2 = _moe_layer(x2, h, logits, gate2, b_route_group[l], b_route_expert[l],
                        w_gate[l].astype(BF16), w_up[l].astype(BF16), w_down[l].astype(BF16),
                        batch, seq)
    return _final_norm(x2, g_final).reshape(batch, seq, d)
```

```python
import functools

import numpy as np
import jax
import jax.numpy as jnp
from jax import lax
from jax.experimental import pallas as pl
from jax.experimental.pallas import tpu as pltpu

F32 = jnp.float32
BF16 = jnp.bfloat16

NORM_EPS = 1e-6
N_HEADS = 4
HEAD_DIM = 128
MLA_Q_LORA = 512
MLA_KV_LORA = 256
MLA_NOPE_DIM = 128
MLA_ROPE_DIM = 64
ROPE_BASE = 10000.0
POOL_WINDOWS = (2, 4, 8, 16)
POOL_HALO = 16
N_BRANCHES = 4
BRANCH_WIDTH = 512
N_EXPERT_GROUPS = 4
EXPERTS_PER_GROUP = 8
N_EXPERTS = N_EXPERT_GROUPS * EXPERTS_PER_GROUP
EXPERT_TOP_K = 2
SMALL_WIDTH = 256

LANE = 128
V7X_VMEM_BYTES = 64 * 1024 * 1024
VMEM_LIMIT = V7X_VMEM_BYTES - 12 * 1024 * 1024


def _cparams(*sem):
    return pltpu.CompilerParams(dimension_semantics=sem, vmem_limit_bytes=VMEM_LIMIT)


def _tile(n, pref):
    t = min(n, pref)
    assert n % t == 0, (n, pref)
    return t


def _mod_body(c_ref, w_ref, b_ref, o_ref):
    c = c_ref[...]
    ca = c * jax.nn.sigmoid(c)
    acc = jnp.dot(ca.astype(BF16), w_ref[0].astype(BF16), preferred_element_type=F32)
    o_ref[0] = acc + b_ref[0]


def _modulation(c_pad, w_mod, b_mod):
    depth, d, n = w_mod.shape
    tn = _tile(n, 1024)
    return pl.pallas_call(
        _mod_body,
        grid=(depth, n // tn),
        in_specs=[
            pl.BlockSpec((8, d), lambda l, j: (0, 0)),
            pl.BlockSpec((1, d, tn), lambda l, j: (l, 0, j)),
            pl.BlockSpec((1, 1, tn), lambda l, j: (l, 0, j)),
        ],
        out_specs=pl.BlockSpec((1, 8, tn), lambda l, j: (l, 0, j)),
        out_shape=jax.ShapeDtypeStruct((depth, 8, n), F32),
        compiler_params=_cparams("parallel", "parallel"),
        name="adaln_modulation",
    )(c_pad, w_mod, b_mod.reshape(depth, 1, n))


def _norm_small_body(x_ref, g_ref, sc_ref, sh_ref, w_ref, h_ref, s_ref):
    x = x_ref[...]
    ms = jnp.mean(x * x, axis=-1, keepdims=True)
    y = x * lax.rsqrt(ms + NORM_EPS) * g_ref[...]
    h = (y * (1.0 + sc_ref[0]) + sh_ref[0]).astype(BF16)
    h_ref[...] = h
    s_ref[...] = jnp.dot(h, w_ref[...], preferred_element_type=F32)


def _norm_small(x2, g, scale, shift, w_small, seq):
    n, d = x2.shape
    tm = _tile(seq, 512)
    per_b = seq // tm
    ws = w_small.shape[1]
    return pl.pallas_call(
        _norm_small_body,
        grid=(n // tm,),
        in_specs=[
            pl.BlockSpec((tm, d), lambda i: (i, 0)),
            pl.BlockSpec((1, d), lambda i: (0, 0)),
            pl.BlockSpec((1, 1, d), lambda i: (i // per_b, 0, 0)),
            pl.BlockSpec((1, 1, d), lambda i: (i // per_b, 0, 0)),
            pl.BlockSpec((d, ws), lambda i: (0, 0)),
        ],
        out_specs=[
            pl.BlockSpec((tm, d), lambda i: (i, 0)),
            pl.BlockSpec((tm, ws), lambda i: (i, 0)),
        ],
        out_shape=[
            jax.ShapeDtypeStruct((n, d), BF16),
            jax.ShapeDtypeStruct((n, ws), F32),
        ],
        compiler_params=_cparams("parallel"),
        name="norm_modulate_small_proj",
    )(x2, g.reshape(1, d), scale, shift, w_small.astype(BF16))


def _mm_body(a_ref, w_ref, o_ref, *, act):
    acc = jnp.dot(a_ref[...], w_ref[...], preferred_element_type=F32)
    if act == "sigmoid":
        acc = jax.nn.sigmoid(acc)
    o_ref[...] = acc.astype(o_ref.dtype)


def _matmul(a, w, out_dtype, act=None, tm=1024, tn=1024, name="matmul"):
    m, k = a.shape
    n = w.shape[1]
    tm = _tile(m, tm)
    tn = _tile(n, tn)
    return pl.pallas_call(
        functools.partial(_mm_body, act=act),
        grid=(m // tm, n // tn),
        in_specs=[
            pl.BlockSpec((tm, k), lambda i, j: (i, 0)),
            pl.BlockSpec((k, tn), lambda i, j: (0, j)),
        ],
        out_specs=pl.BlockSpec((tm, tn), lambda i, j: (i, j)),
        out_shape=jax.ShapeDtypeStruct((m, n), out_dtype),
        compiler_params=_cparams("parallel", "parallel"),
        name=name,
    )(a, w)


def _mm_residual_body(a_ref, w_ref, x_ref, gate_ref, o_ref):
    acc = jnp.dot(a_ref[...], w_ref[...], preferred_element_type=F32)
    o_ref[...] = x_ref[...] + gate_ref[0] * acc


def _matmul_residual(a, w, x2, gate, seq, tm=1024, tn=1024):
    m, k = a.shape
    n = w.shape[1]
    tm = _tile(seq, tm)
    tn = _tile(n, tn)
    per_b = seq // tm
    return pl.pallas_call(
        _mm_residual_body,
        grid=(m // tm, n // tn),
        in_specs=[
            pl.BlockSpec((tm, k), lambda i, j: (i, 0)),
            pl.BlockSpec((k, tn), lambda i, j: (0, j)),
            pl.BlockSpec((tm, tn), lambda i, j: (i, j)),
            pl.BlockSpec((1, 1, tn), lambda i, j: (i // per_b, 0, j)),
        ],
        out_specs=pl.BlockSpec((tm, tn), lambda i, j: (i, j)),
        out_shape=jax.ShapeDtypeStruct((m, n), F32),
        compiler_params=_cparams("parallel", "parallel"),
        name="out_proj_residual",
    )(a, w, x2, gate)


def _latent_norm(a_ref, g_ref):
    a = a_ref[...]
    ms = jnp.mean(a * a, axis=-1, keepdims=True)
    return (a * lax.rsqrt(ms + NORM_EPS) * g_ref[...]).astype(BF16)


def _mla_q_body(a_ref, g_ref, w_ref, cos_ref, sin_ref, nope_ref, rope_ref):
    acc = jnp.dot(_latent_norm(a_ref, g_ref), w_ref[...], preferred_element_type=F32)
    w = nope_ref.shape[1]
    nope_ref[...] = acc[:, :w].astype(BF16)
    rope_ref[...] = (acc[:, w:2 * w] * cos_ref[...] + acc[:, 2 * w:] * sin_ref[...]).astype(BF16)


def _mla_q(latent, col_block, g, w_q, cos_t, sin_t, seq):
    n = latent.shape[0]
    k = w_q.shape[0]
    w = w_q.shape[1] // 3
    tm = _tile(seq, 1024)
    per_b = seq // tm
    return pl.pallas_call(
        _mla_q_body,
        grid=(n // tm,),
        in_specs=[
            pl.BlockSpec((tm, k), lambda i: (i, col_block)),
            pl.BlockSpec((1, k), lambda i: (0, 0)),
            pl.BlockSpec((k, 3 * w), lambda i: (0, 0)),
            pl.BlockSpec((tm, w), lambda i: (i % per_b, 0)),
            pl.BlockSpec((tm, w), lambda i: (i % per_b, 0)),
        ],
        out_specs=[
            pl.BlockSpec((tm, w), lambda i: (i, 0)),
            pl.BlockSpec((tm, w), lambda i: (i, 0)),
        ],
        out_shape=[
            jax.ShapeDtypeStruct((n, w), BF16),
            jax.ShapeDtypeStruct((n, w), BF16),
        ],
        compiler_params=_cparams("parallel"),
        name="mla_q_up_rope",
    )(latent, g.reshape(1, k), w_q, cos_t, sin_t)


def _mla_kv_body(a_ref, g_ref, w_ref, o_ref):
    o_ref[...] = jnp.dot(_latent_norm(a_ref, g_ref), w_ref[...],
                         preferred_element_type=F32).astype(BF16)


def _mla_kv(latent, col_block, g, w_kv, seq):
    n = latent.shape[0]
    k, nout = w_kv.shape
    tm = _tile(seq, 1024)
    return pl.pallas_call(
        _mla_kv_body,
        grid=(n // tm,),
        in_specs=[
            pl.BlockSpec((tm, k), lambda i: (i, col_block)),
            pl.BlockSpec((1, k), lambda i: (0, 0)),
            pl.BlockSpec((k, nout), lambda i: (0, 0)),
        ],
        out_specs=pl.BlockSpec((tm, nout), lambda i: (i, 0)),
        out_shape=jax.ShapeDtypeStruct((n, nout), BF16),
        compiler_params=_cparams("parallel"),
        name="mla_kv_up",
    )(latent, g.reshape(1, k), w_kv)


def _softmax_attn_body(*refs, tq, tk, has_extra, has_bias):
    refs = list(refs)
    qm_ref = refs.pop(0)
    qe_ref = refs.pop(0) if has_extra else None
    km_ref = refs.pop(0)
    ke_ref = refs.pop(0) if has_extra else None
    v_ref = refs.pop(0)
    kb_ref = refs.pop(0) if has_bias else None
    o_ref, m_ref, l_ref, acc_ref = refs
    qi = pl.program_id(2)
    q = qm_ref[0]
    if has_extra:
        q = jnp.concatenate([q, qe_ref[0]], axis=-1)
    m_ref[...] = jnp.full(m_ref.shape, -jnp.inf, F32)
    l_ref[...] = jnp.zeros(l_ref.shape, F32)
    acc_ref[...] = jnp.zeros(acc_ref.shape, F32)
    chunks_per_q = tq // tk
    n_full = qi * chunks_per_q

    def chunk(c, diag):
        k0 = pl.multiple_of(c * tk, tk)
        k = km_ref[0, pl.ds(k0, tk), :]
        if has_extra:
            k = jnp.concatenate([k, ke_ref[0, pl.ds(k0, tk), :]], axis=-1)
        s = lax.dot_general(q, k, (((1,), (1,)), ((), ())), preferred_element_type=F32)
        if has_bias:
            s = s + kb_ref[0, 0, :, pl.ds(k0, tk)]
        if diag is not None:
            row = lax.broadcasted_iota(jnp.int32, (tq, tk), 0)
            col = lax.broadcasted_iota(jnp.int32, (tq, tk), 1) + diag * tk
            s = jnp.where(row >= col, s, -jnp.inf)
        m_prev = m_ref[...]
        m_new = jnp.maximum(m_prev, jnp.max(s, axis=-1, keepdims=True))
        alpha = jnp.exp(m_prev - m_new)
        p = jnp.exp(s - m_new)
        l_ref[...] = alpha * l_ref[...] + jnp.sum(p, axis=-1, keepdims=True)
        acc_ref[...] = alpha * acc_ref[...] + jnp.dot(
            p.astype(BF16), v_ref[0, pl.ds(k0, tk), :], preferred_element_type=F32)
        m_ref[...] = m_new

    def full_chunk(c, carry):
        chunk(c, None)
        return carry

    lax.fori_loop(0, n_full, full_chunk, 0)
    for d in range(chunks_per_q):
        chunk(n_full + d, d)
    o_ref[0] = (acc_ref[...] / l_ref[...]).astype(o_ref.dtype)


def _softmax_attention(qm, km, v, maps, batch, seq, extra=None, key_bias=None):
    tq = _tile(seq, 512)
    tk = _tile(seq, 512)

    def qspec(f):
        return pl.BlockSpec((1, tq, HEAD_DIM), lambda b, h, i: (f(b, h)[0], i, f(b, h)[1]))

    def kspec(f):
        return pl.BlockSpec((1, seq, HEAD_DIM), lambda b, h, i: (f(b, h)[0], 0, f(b, h)[1]))

    fqm, fkm, fv = maps[:3]
    args, specs = [qm], [qspec(fqm)]
    if extra is not None:
        args.append(extra[0])
        specs.append(qspec(maps[3]))
    args.append(km)
    specs.append(kspec(fkm))
    if extra is not None:
        args.append(extra[1])
        specs.append(kspec(maps[4]))
    args.append(v)
    specs.append(kspec(fv))
    if key_bias is not None:
        args.append(key_bias)
        specs.append(pl.BlockSpec((1, 1, 1, seq), lambda b, h, i: (b, h, 0, 0)))

    return pl.pallas_call(
        functools.partial(_softmax_attn_body, tq=tq, tk=tk, has_extra=extra is not None,
                          has_bias=key_bias is not None),
        grid=(batch, N_HEADS, seq // tq),
        in_specs=specs,
        out_specs=pl.BlockSpec((1, tq, HEAD_DIM), lambda b, h, i: (b, i, h)),
        out_shape=jax.ShapeDtypeStruct((batch, seq, N_HEADS * HEAD_DIM), BF16),
        scratch_shapes=[
            pltpu.VMEM((tq, 1), F32),
            pltpu.VMEM((tq, 1), F32),
            pltpu.VMEM((tq, HEAD_DIM), F32),
        ],
        compiler_params=_cparams("parallel", "parallel", "arbitrary"),
        name="causal_softmax_attention",
    )(*args)


def _sb_attn_body(q_ref, k_ref, v_ref, tri_ref, o_ref, r_ref, acc_ref, *, tq, tk):
    qi = pl.program_id(2)
    q = q_ref[0]
    r_ref[...] = jnp.zeros(r_ref.shape, F32)
    acc_ref[...] = jnp.zeros(acc_ref.shape, F32)
    chunks_per_q = tq // tk
    n_full = qi * chunks_per_q

    def chunk(c, diag):
        k0 = pl.multiple_of(c * tk, tk)
        z = lax.dot_general(q, k_ref[0, pl.ds(k0, tk), :], (((1,), (1,)), ((), ())),
                            preferred_element_type=F32)
        log_1m = -(jnp.maximum(z, 0.0) + jnp.log(1.0 + jnp.exp(-jnp.abs(z))))
        log_beta = z + log_1m
        if diag is not None:
            row = lax.broadcasted_iota(jnp.int32, (tq, tk), 0)
            col = lax.broadcasted_iota(jnp.int32, (tq, tk), 1) + diag * tk
            strict = row > col
            log_1m = jnp.where(strict, log_1m, 0.0)
        hi = log_1m.astype(BF16)
        lo = (log_1m - hi.astype(F32)).astype(BF16)
        tri = tri_ref[...]
        e = jnp.dot(hi, tri, preferred_element_type=F32) + jnp.dot(lo, tri, preferred_element_type=F32)
        a = jnp.exp(log_beta + e[:, :tk] + r_ref[...])
        if diag is not None:
            a = jnp.where(strict, a, 0.0)
        acc_ref[...] += jnp.dot(a.astype(BF16), v_ref[0, pl.ds(k0, tk), :],
                                preferred_element_type=F32)
        r_ref[...] += e[:, tk:tk + 1]

    for d in reversed(range(chunks_per_q)):
        chunk(n_full + d, d)

    def full_chunk(i, carry):
        chunk(n_full - 1 - i, None)
        return carry

    lax.fori_loop(0, n_full, full_chunk, 0)
    o_ref[0] = acc_ref[...].astype(o_ref.dtype)


def _sb_tri(tk):
    j_src = np.arange(tk)[:, None]
    j_dst = np.arange(tk + LANE)[None, :]
    return jnp.asarray((j_src > j_dst) | (j_dst >= tk), dtype=BF16)


def _sb_attention(qkv, q_blk, k_blk, v_blk, batch, seq):
    tq = _tile(seq, 512)
    tk = _tile(seq, 256)
    return pl.pallas_call(
        functools.partial(_sb_attn_body, tq=tq, tk=tk),
        grid=(batch, N_HEADS, seq // tq),
        in_specs=[
            pl.BlockSpec((1, tq, HEAD_DIM), lambda b, h, i: (b, i, q_blk + h)),
            pl.BlockSpec((1, seq, HEAD_DIM), lambda b, h, i: (b, 0, k_blk + h)),
            pl.BlockSpec((1, seq, HEAD_DIM), lambda b, h, i: (b, 0, v_blk + h)),
            pl.BlockSpec((tk, tk + LANE), lambda b, h, i: (0, 0)),
        ],
        out_specs=pl.BlockSpec((1, tq, HEAD_DIM), lambda b, h, i: (b, i, h)),
        out_shape=jax.ShapeDtypeStruct((batch, seq, N_HEADS * HEAD_DIM), BF16),
        scratch_shapes=[
            pltpu.VMEM((tq, 1), F32),
            pltpu.VMEM((tq, HEAD_DIM), F32),
        ],
        compiler_params=_cparams("parallel", "parallel", "arbitrary"),
        name="stick_breaking_attention",
    )(qkv, qkv, qkv, _sb_tri(tk))


def _pool_body(u_ref, halo_ref, w_ref, sc_ref, o_ref, ext_ref, *, ts):
    i = pl.program_id(1)
    halo = jnp.where(i > 0, halo_ref[0], 0.0)
    ext_ref[:POOL_HALO, :] = halo
    ext_ref[POOL_HALO:, :] = u_ref[0]
    pos = i * ts + lax.broadcasted_iota(jnp.int32, (ts, 1), 0)
    outs = []
    for g, win in enumerate(POOL_WINDOWS):
        lanes = slice(g * LANE, (g + 1) * LANE)
        tok = ext_ref[POOL_HALO:, lanes]
        tot = tok
        for back in range(1, win):
            tot = tot + ext_ref[POOL_HALO - back:POOL_HALO - back + ts, lanes]
        cnt = jnp.minimum(pos + 1, win).astype(F32)
        pooled = (tot / cnt - tok).astype(BF16)
        outs.append(jnp.dot(pooled, w_ref[g], preferred_element_type=F32))
    y = jnp.concatenate(outs, axis=-1) * sc_ref[...]
    o_ref[0] = y.astype(o_ref.dtype)


def _pool_mixer(latent3, col_block, w_pool, pool_scale, batch, seq):
    width = len(POOL_WINDOWS) * LANE
    ts = _tile(seq, 512)
    halo_per_tile = ts // POOL_HALO
    return pl.pallas_call(
        functools.partial(_pool_body, ts=ts),
        grid=(batch, seq // ts),
        in_specs=[
            pl.BlockSpec((1, ts, width), lambda b, i: (b, i, col_block)),
            pl.BlockSpec((1, POOL_HALO, width),
                         lambda b, i: (b, jnp.maximum(i * halo_per_tile - 1, 0), col_block)),
            pl.BlockSpec((len(POOL_WINDOWS), LANE, LANE), lambda b, i: (0, 0, 0)),
            pl.BlockSpec((1, width), lambda b, i: (0, 0)),
        ],
        out_specs=pl.BlockSpec((1, ts, width), lambda b, i: (b, i, 0)),
        out_shape=jax.ShapeDtypeStruct((batch, seq, width), BF16),
        scratch_shapes=[pltpu.VMEM((ts + POOL_HALO, width), F32)],
        compiler_params=_cparams("parallel", "arbitrary"),
        name="multiscale_pool",
    )(latent3, latent3, w_pool.astype(BF16), pool_scale.reshape(1, width))


def _merge_body(y0, y1, y2, y3, g0, g1, g2, g3, wb_ref, o_ref):
    acc = None
    for n, (y, g) in enumerate(((y0, g0), (y1, g1), (y2, g2), (y3, g3))):
        t = g[...].astype(F32) * jnp.dot(y[...], wb_ref[n], preferred_element_type=F32)
        acc = t if acc is None else acc + t
    o_ref[...] = acc.astype(o_ref.dtype)


def _merge(ys, gates, w_branch):
    n = ys[0].shape[0]
    d = w_branch.shape[2]
    tm = _tile(n, 1024)
    tn = _tile(d, 512)
    nj = d // tn
    y_spec = pl.BlockSpec((tm, BRANCH_WIDTH), lambda i, j: (i, 0))

    def g_spec(b):
        return pl.BlockSpec((tm, tn), lambda i, j: (i, b * nj + j))

    return pl.pallas_call(
        _merge_body,
        grid=(n // tm, nj),
        in_specs=[y_spec] * N_BRANCHES + [g_spec(b) for b in range(N_BRANCHES)] + [
            pl.BlockSpec((N_BRANCHES, BRANCH_WIDTH, tn), lambda i, j: (0, 0, j))],
        out_specs=pl.BlockSpec((tm, tn), lambda i, j: (i, j)),
        out_shape=jax.ShapeDtypeStruct((n, d), BF16),
        compiler_params=_cparams("parallel", "parallel"),
        name="gated_branch_merge",
    )(*ys, gates, gates, gates, gates, w_branch)


def _expert_changed(be_ref):
    blk = pl.program_id(1)
    return (blk == 0) | (be_ref[blk] != be_ref[jnp.maximum(blk - 1, 0)])


def _moe_up_body(be_ref, nb_ref, x_ref, wg_ref, wu_ref, o_ref, wg_s, wu_s):
    blk = pl.program_id(1)

    @pl.when(_expert_changed(be_ref))
    def _():
        wg_s[...] = wg_ref[0, 0].astype(BF16)
        wu_s[...] = wu_ref[0, 0].astype(BF16)

    @pl.when(blk < nb_ref[0])
    def _():
        x = x_ref[...]
        g = jnp.dot(x, wg_s[...], preferred_element_type=F32)
        u = jnp.dot(x, wu_s[...], preferred_element_type=F32)
        o_ref[...] = (g * jax.nn.sigmoid(g) * u).astype(o_ref.dtype)

    @pl.when(blk >= nb_ref[0])
    def _():
        o_ref[...] = jnp.zeros(o_ref.shape, o_ref.dtype)


def _moe_down_body(be_ref, nb_ref, a_ref, wd_ref, rw_ref, o_ref, wd_s):
    blk = pl.program_id(1)

    @pl.when(_expert_changed(be_ref))
    def _():
        wd_s[...] = wd_ref[0, 0].astype(BF16)

    @pl.when(blk < nb_ref[0])
    def _():
        y = jnp.dot(a_ref[...], wd_s[...], preferred_element_type=F32)
        o_ref[...] = y * rw_ref[...]

    @pl.when(blk >= nb_ref[0])
    def _():
        o_ref[...] = jnp.zeros(o_ref.shape, o_ref.dtype)


def _moe_ffn(x_buf, row_w, block_expert, n_used, w_gate, w_up, w_down, layer, tb):
    n_rows, d = x_buf.shape
    f = w_gate.shape[3]
    tf = _tile(f, 512)
    tn = _tile(d, 1024)
    n_blocks = n_rows // tb
    act = pl.pallas_call(
        _moe_up_body,
        grid_spec=pltpu.PrefetchScalarGridSpec(
            num_scalar_prefetch=2,
            grid=(f // tf, n_blocks),
            in_specs=[
                pl.BlockSpec((tb, d), lambda j, i, be, nb: (i, 0)),
                pl.BlockSpec((1, 1, d, tf), lambda j, i, be, nb: (layer, be[i], 0, j)),
                pl.BlockSpec((1, 1, d, tf), lambda j, i, be, nb: (layer, be[i], 0, j)),
            ],
            out_specs=pl.BlockSpec((tb, tf), lambda j, i, be, nb: (i, j)),
            scratch_shapes=[pltpu.VMEM((d, tf), BF16), pltpu.VMEM((d, tf), BF16)],
        ),
        out_shape=jax.ShapeDtypeStruct((n_rows, f), BF16),
        compiler_params=_cparams("arbitrary", "arbitrary"),
        name="expert_gate_up_swiglu",
    )(block_expert, n_used, x_buf, w_gate, w_up)
    return pl.pallas_call(
        _moe_down_body,
        grid_spec=pltpu.PrefetchScalarGridSpec(
            num_scalar_prefetch=2,
            grid=(d // tn, n_blocks),
            in_specs=[
                pl.BlockSpec((tb, f), lambda j, i, be, nb: (i, 0)),
                pl.BlockSpec((1, 1, f, tn), lambda j, i, be, nb: (layer, be[i], 0, j)),
                pl.BlockSpec((tb, 1), lambda j, i, be, nb: (i, 0)),
            ],
            out_specs=pl.BlockSpec((tb, tn), lambda j, i, be, nb: (i, j)),
            scratch_shapes=[pltpu.VMEM((f, tn), BF16)],
        ),
        out_shape=jax.ShapeDtypeStruct((n_rows, d), F32),
        compiler_params=_cparams("arbitrary", "arbitrary"),
        name="expert_down",
    )(block_expert, n_used, act, w_down, row_w)


def _final_norm_body(x_ref, g_ref, o_ref):
    x = x_ref[...]
    ms = jnp.mean(x * x, axis=-1, keepdims=True)
    o_ref[...] = x * lax.rsqrt(ms + NORM_EPS) * g_ref[...]


def _final_norm(x2, g):
    n, d = x2.shape
    tm = _tile(n, 512)
    return pl.pallas_call(
        _final_norm_body,
        grid=(n // tm,),
        in_specs=[pl.BlockSpec((tm, d), lambda i: (i, 0)), pl.BlockSpec((1, d), lambda i: (0, 0))],
        out_specs=pl.BlockSpec((tm, d), lambda i: (i, 0)),
        out_shape=jax.ShapeDtypeStruct((n, d), F32),
        compiler_params=_cparams("parallel"),
        name="final_rmsnorm",
    )(x2, g.reshape(1, d))


def _rope_tables(seq):
    half = MLA_ROPE_DIM // 2
    inv = jnp.power(ROPE_BASE, -2.0 * jnp.arange(half, dtype=F32) / MLA_ROPE_DIM)
    ang = jnp.arange(seq).astype(F32)[:, None] * inv[None, :]
    cos, sin = jnp.cos(ang), jnp.sin(ang)
    c64 = jnp.concatenate([cos, cos], axis=-1)
    s64 = jnp.concatenate([-sin, sin], axis=-1)
    return c64, s64


def _swap_halves(w):
    half = w.shape[-1] // 2
    return jnp.concatenate([w[..., half:], w[..., :half]], axis=-1)


def _pad_lanes(a, width):
    return jnp.pad(a, [(0, 0)] * (a.ndim - 1) + [(0, width - a.shape[-1])])


def _mixer_layer(x2, h, small, batch, seq, prm):
    n, d = x2.shape
    (w_qkv, w_lat, w_gates, b_forget, g_q, w_q, g_kv, w_kv, w_pool, pool_scale, w_branch, w_out,
     gate1) = prm
    c64, s64 = _rope_tables(seq)

    qkv = _matmul(h, w_qkv, BF16, name="fox_sb_qkv_proj").reshape(batch, seq, -1)
    latent = _matmul(h, w_lat, F32, tn=640, name="latent_pool_proj")
    gates = _matmul(h, w_gates, BF16, act="sigmoid", name="branch_gate_proj")

    fox_f = small[:, 2 * MLA_ROPE_DIM:2 * MLA_ROPE_DIM + N_HEADS].reshape(batch, seq, N_HEADS)
    log_f_cum = jnp.cumsum(jax.nn.log_sigmoid(fox_f + b_forget), axis=1)
    k_bias = -log_f_cum.transpose(0, 2, 1)[:, :, None, :]
    nh = N_HEADS
    y_fox = _softmax_attention(
        qkv, qkv, qkv,
        (lambda b, hh: (b, hh), lambda b, hh: (b, nh + hh), lambda b, hh: (b, 2 * nh + hh)),
        batch, seq, key_bias=k_bias)

    y_sb = _sb_attention(qkv, 3 * nh, 4 * nh, 5 * nh, batch, seq)

    cos_t = jnp.tile(_pad_lanes(c64, HEAD_DIM), (1, nh))
    sin_t = jnp.tile(_pad_lanes(s64, HEAD_DIM), (1, nh))
    q_nope, q_rope = _mla_q(latent, 0, g_q, w_q, cos_t, sin_t, seq)
    kv = _mla_kv(latent, 4, g_kv, w_kv, seq).reshape(batch, seq, -1)
    kr = small[:, :MLA_ROPE_DIM] * jnp.tile(c64, (batch, 1)) \
        + small[:, MLA_ROPE_DIM:2 * MLA_ROPE_DIM] * jnp.tile(s64, (batch, 1))
    kr = _pad_lanes(kr.astype(BF16), HEAD_DIM).reshape(batch, seq, HEAD_DIM)
    y_mla = _softmax_attention(
        q_nope.reshape(batch, seq, -1), kv, kv,
        (lambda b, hh: (b, hh), lambda b, hh: (b, hh), lambda b, hh: (b, nh + hh),
         lambda b, hh: (b, hh), lambda b, hh: (b, 0)),
        batch, seq, extra=(q_rope.reshape(batch, seq, -1), kr))

    y_pool = _pool_mixer(latent.reshape(batch, seq, -1), 1, w_pool, pool_scale, batch, seq)

    ys = [y.reshape(n, BRANCH_WIDTH) for y in (y_fox, y_sb, y_mla, y_pool)]
    merged = _merge(ys, gates, w_branch)
    return _matmul_residual(merged, w_out, x2, gate1, seq)


def _moe_layer(x2, h, logits, gate2, b_rg, b_re, w_gate, w_up, w_down, layer, batch, seq):
    n, d = x2.shape
    tb = 256
    g_logits = logits[:, :N_EXPERT_GROUPS] + b_rg
    g_idx = jnp.argmax(g_logits, axis=-1).astype(jnp.int32)[:, None]
    g_top = jnp.max(g_logits, axis=-1)
    p_group = jnp.exp(g_top - jax.nn.logsumexp(g_logits, axis=-1))
    e_logits = (logits[:, N_EXPERT_GROUPS:N_EXPERT_GROUPS + N_EXPERTS] + b_re).reshape(
        n, N_EXPERT_GROUPS, EXPERTS_PER_GROUP)
    e_logits = jnp.take_along_axis(e_logits, g_idx[:, :, None], axis=1)[:, 0]
    probs = jax.nn.softmax(e_logits, axis=-1)
    lane = jnp.arange(EXPERTS_PER_GROUP, dtype=jnp.int32)[None, :]
    i1 = jnp.argmax(probs, axis=-1).astype(jnp.int32)[:, None]
    p1 = jnp.max(probs, axis=-1, keepdims=True)
    rest = jnp.where(lane == i1, -jnp.inf, probs)
    i2 = jnp.argmax(rest, axis=-1).astype(jnp.int32)[:, None]
    p2 = jnp.max(rest, axis=-1, keepdims=True)
    top_p = jnp.concatenate([p1, p2], axis=-1)
    top_i = jnp.concatenate([i1, i2], axis=-1)
    weights = p_group[:, None] * top_p / jnp.sum(top_p, axis=-1, keepdims=True)

    expert_id = (g_idx * EXPERTS_PER_GROUP + top_i).reshape(-1).astype(jnp.int32)
    m = expert_id.shape[0]
    w_flat = weights.reshape(-1)

    order = jnp.argsort(expert_id).astype(jnp.int32)
    counts = jnp.sum(expert_id[:, None] == jnp.arange(N_EXPERTS, dtype=jnp.int32)[None, :],
                     axis=0).astype(jnp.int32)
    padded = ((counts + tb - 1) // tb) * tb
    start = jnp.cumsum(counts) - counts
    pend = jnp.cumsum(padded)
    pstart = pend - padded
    n_rows = m + N_EXPERTS * tb
    n_blocks = n_rows // tb
    block_start = jnp.arange(n_blocks, dtype=jnp.int32) * tb
    n_used = (pend[-1] // tb).astype(jnp.int32)
    block_expert = jnp.minimum(
        jnp.sum(pend[None, :] <= block_start[:, None], axis=1), N_EXPERTS - 1).astype(jnp.int32)
    last_used = block_expert[jnp.maximum(n_used - 1, 0)]
    block_expert = jnp.where(jnp.arange(n_blocks) < n_used, block_expert, last_used)

    row = jnp.arange(n_rows, dtype=jnp.int32)
    row_e = jnp.repeat(block_expert, tb)
    within = row - pstart[row_e]
    valid = (within < counts[row_e]) & (row < pend[-1])
    src_sorted = jnp.clip(start[row_e] + within, 0, m - 1)
    src_assign = order[src_sorted]
    row_tok = jnp.where(valid, src_assign // EXPERT_TOP_K, 0)
    row_w = jnp.where(valid, w_flat[src_assign], 0.0).astype(F32)

    e_sorted = expert_id[order]
    dest_sorted = pstart[e_sorted] + jnp.arange(m, dtype=jnp.int32) - start[e_sorted]
    dest = jnp.zeros((m,), jnp.int32).at[order].set(dest_sorted)

    x_buf = jnp.take(h, row_tok, axis=0)
    y = _moe_ffn(x_buf, row_w.reshape(n_rows, 1), block_expert, n_used.reshape(1),
                 w_gate, w_up, w_down, layer, tb)
    dest = dest.reshape(n, EXPERT_TOP_K)
    moe = jnp.take(y, dest[:, 0], axis=0) + jnp.take(y, dest[:, 1], axis=0)
    gate_rows = jnp.repeat(gate2[:, 0, :], seq, axis=0)
    return x2 + gate_rows * moe


def kernel(x, c, w_mod, b_mod, g_norm1, g_norm2, w_in, b_forget, g_q_norm, w_uq, g_kv_norm, w_ukv, w_pool, pool_scale, w_branch, w_out, w_route_group, b_route_group, w_route_expert, b_route_expert, w_gate, w_up, w_down, g_final):
    batch, seq, d = x.shape
    depth = w_mod.shape[0]
    n = batch * seq
    nh = N_HEADS

    c_pad = jnp.pad(c, ((0, 8 - batch), (0, 0)))
    mod = _modulation(c_pad, w_mod, b_mod)[:, :batch]

    hq = nh * HEAD_DIM
    o_fox, o_f = 0, 3 * hq
    o_sb = o_f + nh
    o_cq = o_sb + 3 * hq
    o_ckv = o_cq + MLA_Q_LORA
    o_kr = o_ckv + MLA_KV_LORA
    o_pool = o_kr + MLA_ROPE_DIM
    o_gate = o_pool + len(POOL_WINDOWS) * LANE
    attn_scale = HEAD_DIM ** -0.5
    mla_scale = (MLA_NOPE_DIM + MLA_ROPE_DIM) ** -0.5

    x2 = x.reshape(n, d)
    for l in range(depth):
        m6 = mod[l].reshape(batch, 6, 1, d)
        shift1, scale1, gate1, shift2, scale2, gate2 = [m6[:, t] for t in range(6)]
        wi = w_in[l]
        w_qkv = jnp.concatenate([
            wi[:, o_fox:o_fox + hq] * attn_scale, wi[:, o_fox + hq:o_fox + 3 * hq],
            wi[:, o_sb:o_sb + hq] * attn_scale, wi[:, o_sb + hq:o_sb + 3 * hq]], axis=1).astype(BF16)
        w_lat = jnp.concatenate([wi[:, o_cq:o_ckv], wi[:, o_pool:o_gate], wi[:, o_ckv:o_kr]],
                                axis=1).astype(BF16)
        w_gates = wi[:, o_gate:].astype(BF16)
        w_kr = wi[:, o_kr:o_pool]
        w_small1 = _pad_lanes(jnp.concatenate([w_kr, _swap_halves(w_kr), wi[:, o_f:o_sb]], axis=1),
                              SMALL_WIDTH)

        wq = w_uq[l].reshape(MLA_Q_LORA, nh, MLA_NOPE_DIM + MLA_ROPE_DIM) * mla_scale
        wq_rope = wq[:, :, MLA_NOPE_DIM:]
        w_q = jnp.concatenate([
            wq[:, :, :MLA_NOPE_DIM].reshape(MLA_Q_LORA, -1),
            _pad_lanes(wq_rope, HEAD_DIM).reshape(MLA_Q_LORA, -1),
            _pad_lanes(_swap_halves(wq_rope), HEAD_DIM).reshape(MLA_Q_LORA, -1)], axis=1).astype(BF16)
        wkv = w_ukv[l].reshape(MLA_KV_LORA, nh, 2 * HEAD_DIM)
        w_kv = jnp.concatenate([wkv[:, :, :HEAD_DIM].reshape(MLA_KV_LORA, -1),
                                wkv[:, :, HEAD_DIM:].reshape(MLA_KV_LORA, -1)], axis=1).astype(BF16)

        h, small = _norm_small(x2, g_norm1[l], scale1, shift1, w_small1, seq)
        prm = (w_qkv, w_lat, w_gates, b_forget[l], g_q_norm[l], w_q, g_kv_norm[l], w_kv,
               w_pool[l], pool_scale[l], w_branch[l].astype(BF16), w_out[l].astype(BF16), gate1)
        x2 = _mixer_layer(x2, h, small, batch, seq, prm)

        w_small2 = _pad_lanes(jnp.concatenate([w_route_group[l], w_route_expert[l]], axis=1),
                              SMALL_WIDTH)
        h, logits = _norm_small(x2, g_norm2[l], scale2, shift2, w_small2, seq)
        x2 = _moe_layer(x2, h, logits, gate2, b_route_group[l], b_route_expert[l],
                        w_gate, w_up, w_down, l, batch, seq)
    return _final_norm(x2, g_final).reshape(batch, seq, d)
```

```python
import functools

import numpy as np
import jax
import jax.numpy as jnp
from jax import lax
from jax.experimental import pallas as pl
from jax.experimental.pallas import tpu as pltpu

F32 = jnp.float32
BF16 = jnp.bfloat16

NORM_EPS = 1e-6
N_HEADS = 4
HEAD_DIM = 128
MLA_Q_LORA = 512
MLA_KV_LORA = 256
MLA_NOPE_DIM = 128
MLA_ROPE_DIM = 64
ROPE_BASE = 10000.0
POOL_WINDOWS = (2, 4, 8, 16)
POOL_HALO = 16
N_BRANCHES = 4
BRANCH_WIDTH = 512
N_EXPERT_GROUPS = 4
EXPERTS_PER_GROUP = 8
N_EXPERTS = N_EXPERT_GROUPS * EXPERTS_PER_GROUP
EXPERT_TOP_K = 2
SMALL_WIDTH = 256

LANE = 128
V7X_VMEM_BYTES = 64 * 1024 * 1024
VMEM_LIMIT = V7X_VMEM_BYTES - 12 * 1024 * 1024


def _cparams(*sem):
    return pltpu.CompilerParams(dimension_semantics=sem, vmem_limit_bytes=VMEM_LIMIT)


def _tile(n, pref):
    t = min(n, pref)
    assert n % t == 0, (n, pref)
    return t


def _mod_body(c_ref, w_ref, b_ref, o_ref):
    c = c_ref[...]
    ca = c * jax.nn.sigmoid(c)
    acc = jnp.dot(ca.astype(BF16), w_ref[0].astype(BF16), preferred_element_type=F32)
    o_ref[0] = acc + b_ref[0]


def _modulation(c_pad, w_mod, b_mod):
    depth, d, n = w_mod.shape
    tn = _tile(n, 1024)
    return pl.pallas_call(
        _mod_body,
        grid=(depth, n // tn),
        in_specs=[
            pl.BlockSpec((8, d), lambda l, j: (0, 0)),
            pl.BlockSpec((1, d, tn), lambda l, j: (l, 0, j)),
            pl.BlockSpec((1, 1, tn), lambda l, j: (l, 0, j)),
        ],
        out_specs=pl.BlockSpec((1, 8, tn), lambda l, j: (l, 0, j)),
        out_shape=jax.ShapeDtypeStruct((depth, 8, n), F32),
        compiler_params=_cparams("parallel", "parallel"),
        name="adaln_modulation",
    )(c_pad, w_mod, b_mod.reshape(depth, 1, n))


def _norm_small_body(x_ref, g_ref, sc_ref, sh_ref, w_ref, h_ref, s_ref):
    x = x_ref[...]
    ms = jnp.mean(x * x, axis=-1, keepdims=True)
    y = x * lax.rsqrt(ms + NORM_EPS) * g_ref[...]
    h = (y * (1.0 + sc_ref[0]) + sh_ref[0]).astype(BF16)
    h_ref[...] = h
    s_ref[...] = jnp.dot(h, w_ref[...], preferred_element_type=F32)


def _norm_small(x2, g, scale, shift, w_small, seq):
    n, d = x2.shape
    tm = _tile(seq, 512)
    per_b = seq // tm
    ws = w_small.shape[1]
    return pl.pallas_call(
        _norm_small_body,
        grid=(n // tm,),
        in_specs=[
            pl.BlockSpec((tm, d), lambda i: (i, 0)),
            pl.BlockSpec((1, d), lambda i: (0, 0)),
            pl.BlockSpec((1, 1, d), lambda i: (i // per_b, 0, 0)),
            pl.BlockSpec((1, 1, d), lambda i: (i // per_b, 0, 0)),
            pl.BlockSpec((d, ws), lambda i: (0, 0)),
        ],
        out_specs=[
            pl.BlockSpec((tm, d), lambda i: (i, 0)),
            pl.BlockSpec((tm, ws), lambda i: (i, 0)),
        ],
        out_shape=[
            jax.ShapeDtypeStruct((n, d), BF16),
            jax.ShapeDtypeStruct((n, ws), F32),
        ],
        compiler_params=_cparams("parallel"),
        name="norm_modulate_small_proj",
    )(x2, g.reshape(1, d), scale, shift, w_small.astype(BF16))


def _mm_body(a_ref, w_ref, o_ref, *, act):
    acc = jnp.dot(a_ref[...], w_ref[...], preferred_element_type=F32)
    if act == "sigmoid":
        acc = jax.nn.sigmoid(acc)
    o_ref[...] = acc.astype(o_ref.dtype)


def _matmul(a, w, out_dtype, act=None, tm=1024, tn=1024, name="matmul"):
    m, k = a.shape
    n = w.shape[1]
    tm = _tile(m, tm)
    tn = _tile(n, tn)
    return pl.pallas_call(
        functools.partial(_mm_body, act=act),
        grid=(m // tm, n // tn),
        in_specs=[
            pl.BlockSpec((tm, k), lambda i, j: (i, 0)),
            pl.BlockSpec((k, tn), lambda i, j: (0, j)),
        ],
        out_specs=pl.BlockSpec((tm, tn), lambda i, j: (i, j)),
        out_shape=jax.ShapeDtypeStruct((m, n), out_dtype),
        compiler_params=_cparams("parallel", "parallel"),
        name=name,
    )(a, w)


def _mm_residual_body(a_ref, w_ref, x_ref, gate_ref, o_ref):
    acc = jnp.dot(a_ref[...], w_ref[...], preferred_element_type=F32)
    o_ref[...] = x_ref[...] + gate_ref[0] * acc


def _matmul_residual(a, w, x2, gate, seq, tm=1024, tn=1024):
    m, k = a.shape
    n = w.shape[1]
    tm = _tile(seq, tm)
    tn = _tile(n, tn)
    per_b = seq // tm
    return pl.pallas_call(
        _mm_residual_body,
        grid=(m // tm, n // tn),
        in_specs=[
            pl.BlockSpec((tm, k), lambda i, j: (i, 0)),
            pl.BlockSpec((k, tn), lambda i, j: (0, j)),
            pl.BlockSpec((tm, tn), lambda i, j: (i, j)),
            pl.BlockSpec((1, 1, tn), lambda i, j: (i // per_b, 0, j)),
        ],
        out_specs=pl.BlockSpec((tm, tn), lambda i, j: (i, j)),
        out_shape=jax.ShapeDtypeStruct((m, n), F32),
        compiler_params=_cparams("parallel", "parallel"),
        name="out_proj_residual",
    )(a, w, x2, gate)


def _latent_norm(a_ref, g_ref):
    a = a_ref[...]
    ms = jnp.mean(a * a, axis=-1, keepdims=True)
    return (a * lax.rsqrt(ms + NORM_EPS) * g_ref[...]).astype(BF16)


def _mla_q_body(a_ref, g_ref, w_ref, cos_ref, sin_ref, nope_ref, rope_ref):
    acc = jnp.dot(_latent_norm(a_ref, g_ref), w_ref[...], preferred_element_type=F32)
    w = nope_ref.shape[1]
    nope_ref[...] = acc[:, :w].astype(BF16)
    rope_ref[...] = (acc[:, w:2 * w] * cos_ref[...] + acc[:, 2 * w:] * sin_ref[...]).astype(BF16)


def _mla_q(latent, col_block, g, w_q, cos_t, sin_t, seq):
    n = latent.shape[0]
    k = w_q.shape[0]
    w = w_q.shape[1] // 3
    tm = _tile(seq, 1024)
    per_b = seq // tm
    return pl.pallas_call(
        _mla_q_body,
        grid=(n // tm,),
        in_specs=[
            pl.BlockSpec((tm, k), lambda i: (i, col_block)),
            pl.BlockSpec((1, k), lambda i: (0, 0)),
            pl.BlockSpec((k, 3 * w), lambda i: (0, 0)),
            pl.BlockSpec((tm, w), lambda i: (i % per_b, 0)),
            pl.BlockSpec((tm, w), lambda i: (i % per_b, 0)),
        ],
        out_specs=[
            pl.BlockSpec((tm, w), lambda i: (i, 0)),
            pl.BlockSpec((tm, w), lambda i: (i, 0)),
        ],
        out_shape=[
            jax.ShapeDtypeStruct((n, w), BF16),
            jax.ShapeDtypeStruct((n, w), BF16),
        ],
        compiler_params=_cparams("parallel"),
        name="mla_q_up_rope",
    )(latent, g.reshape(1, k), w_q, cos_t, sin_t)


def _mla_kv_body(a_ref, g_ref, w_ref, o_ref):
    o_ref[...] = jnp.dot(_latent_norm(a_ref, g_ref), w_ref[...],
                         preferred_element_type=F32).astype(BF16)


def _mla_kv(latent, col_block, g, w_kv, seq):
    n = latent.shape[0]
    k, nout = w_kv.shape
    tm = _tile(seq, 1024)
    return pl.pallas_call(
        _mla_kv_body,
        grid=(n // tm,),
        in_specs=[
            pl.BlockSpec((tm, k), lambda i: (i, col_block)),
            pl.BlockSpec((1, k), lambda i: (0, 0)),
            pl.BlockSpec((k, nout), lambda i: (0, 0)),
        ],
        out_specs=pl.BlockSpec((tm, nout), lambda i: (i, 0)),
        out_shape=jax.ShapeDtypeStruct((n, nout), BF16),
        compiler_params=_cparams("parallel"),
        name="mla_kv_up",
    )(latent, g.reshape(1, k), w_kv)


def _softmax_attn_body(*refs, tq, tk, has_extra, has_bias):
    refs = list(refs)
    qm_ref = refs.pop(0)
    qe_ref = refs.pop(0) if has_extra else None
    km_ref = refs.pop(0)
    ke_ref = refs.pop(0) if has_extra else None
    v_ref = refs.pop(0)
    kb_ref = refs.pop(0) if has_bias else None
    o_ref, m_ref, acc_ref = refs
    qi = pl.program_id(1)
    m_ref[...] = jnp.full(m_ref.shape, -jnp.inf, F32)
    acc_ref[...] = jnp.zeros(acc_ref.shape, F32)
    chunks_per_q = tq // tk
    n_full = qi * chunks_per_q
    ones = jnp.ones((tk, HEAD_DIM), BF16)

    def head_chunk(h, k0, diag):
        lanes = slice(h * HEAD_DIM, (h + 1) * HEAD_DIM)
        q = qm_ref[0, :, lanes]
        k = km_ref[0, pl.ds(k0, tk), lanes]
        if has_extra:
            q = jnp.concatenate([q, qe_ref[0, :, lanes]], axis=-1)
            k = jnp.concatenate([k, ke_ref[0, pl.ds(k0, tk), :]], axis=-1)
        s = lax.dot_general(q, k, (((1,), (1,)), ((), ())), preferred_element_type=F32)
        if has_bias:
            s = s + kb_ref[0, h, :, pl.ds(k0, tk)]
        if diag is not None:
            row = lax.broadcasted_iota(jnp.int32, (tq, tk), 0)
            col = lax.broadcasted_iota(jnp.int32, (tq, tk), 1) + diag * tk
            s = jnp.where(row >= col, s, -jnp.inf)
        blocks = [s[:, j * LANE:(j + 1) * LANE] for j in range(tk // LANE)]
        mx = blocks[0]
        for blk in blocks[1:]:
            mx = jnp.maximum(mx, blk)
        m_prev = m_ref[h]
        m_new = jnp.maximum(m_prev, jnp.max(mx, axis=-1, keepdims=True))
        alpha = jnp.exp(m_prev - m_new)
        p = jnp.concatenate([jnp.exp(blk - m_new) for blk in blocks], axis=-1).astype(BF16)
        v_ext = jnp.concatenate([v_ref[0, pl.ds(k0, tk), lanes], ones], axis=-1)
        pv = jnp.dot(p, v_ext, preferred_element_type=F32)
        acc_ref[h, :, :HEAD_DIM] = alpha * acc_ref[h, :, :HEAD_DIM] + pv[:, :HEAD_DIM]
        acc_ref[h, :, HEAD_DIM:] = alpha * acc_ref[h, :, HEAD_DIM:] + pv[:, HEAD_DIM:]
        m_ref[h] = m_new

    def chunk(c, diag):
        k0 = pl.multiple_of(c * tk, tk)
        for h in range(N_HEADS):
            head_chunk(h, k0, diag)

    def full_chunk(c, carry):
        chunk(c, None)
        return carry

    lax.fori_loop(0, n_full, full_chunk, 0)
    for d in range(chunks_per_q):
        chunk(n_full + d, d)
    for h in range(N_HEADS):
        o_ref[0, :, h * HEAD_DIM:(h + 1) * HEAD_DIM] = (
            acc_ref[h, :, :HEAD_DIM] / acc_ref[h, :, HEAD_DIM:]).astype(o_ref.dtype)


def _softmax_attention(qm, q_blk, km, k_blk, v, v_blk, batch, seq, extra=None, key_bias=None):
    tq = _tile(seq, 512)
    tk = _tile(seq, 512)
    width = N_HEADS * HEAD_DIM
    args = [qm]
    specs = [pl.BlockSpec((1, tq, width), lambda b, i: (b, i, q_blk))]
    if extra is not None:
        args.append(extra[0])
        specs.append(pl.BlockSpec((1, tq, width), lambda b, i: (b, i, 0)))
    args.append(km)
    specs.append(pl.BlockSpec((1, seq, width), lambda b, i: (b, 0, k_blk)))
    if extra is not None:
        args.append(extra[1])
        specs.append(pl.BlockSpec((1, seq, HEAD_DIM), lambda b, i: (b, 0, 0)))
    args.append(v)
    specs.append(pl.BlockSpec((1, seq, width), lambda b, i: (b, 0, v_blk)))
    if key_bias is not None:
        args.append(key_bias)
        specs.append(pl.BlockSpec((1, N_HEADS, 1, seq), lambda b, i: (b, 0, 0, 0)))

    return pl.pallas_call(
        functools.partial(_softmax_attn_body, tq=tq, tk=tk, has_extra=extra is not None,
                          has_bias=key_bias is not None),
        grid=(batch, seq // tq),
        in_specs=specs,
        out_specs=pl.BlockSpec((1, tq, width), lambda b, i: (b, i, 0)),
        out_shape=jax.ShapeDtypeStruct((batch, seq, width), BF16),
        scratch_shapes=[
            pltpu.VMEM((N_HEADS, tq, HEAD_DIM), F32),
            pltpu.VMEM((N_HEADS, tq, 2 * HEAD_DIM), F32),
        ],
        compiler_params=_cparams("parallel", "arbitrary"),
        name="causal_softmax_attention",
    )(*args)


def _sb_attn_body(q_ref, k_ref, v_ref, tri_ref, o_ref, r_ref, acc_ref, *, tq, tk):
    qi = pl.program_id(1)
    r_ref[...] = jnp.zeros(r_ref.shape, F32)
    acc_ref[...] = jnp.zeros(acc_ref.shape, F32)
    chunks_per_q = tq // tk
    n_full = qi * chunks_per_q

    def head_chunk(h, k0, diag):
        lanes = slice(h * HEAD_DIM, (h + 1) * HEAD_DIM)
        z = lax.dot_general(q_ref[0, :, lanes], k_ref[0, pl.ds(k0, tk), lanes],
                            (((1,), (1,)), ((), ())), preferred_element_type=F32)
        sp = jnp.maximum(z, 0.0) + jnp.log(1.0 + jnp.exp(-jnp.abs(z)))
        log_beta = z - sp
        if diag is not None:
            row = lax.broadcasted_iota(jnp.int32, (tq, tk), 0)
            col = lax.broadcasted_iota(jnp.int32, (tq, tk), 1) + diag * tk
            strict = row > col
            sp = jnp.where(strict, sp, 0.0)
        hi = sp.astype(BF16)
        lo = (sp - hi.astype(F32)).astype(BF16)
        tri = tri_ref[...]
        e = jnp.dot(hi, tri, preferred_element_type=F32) + jnp.dot(lo, tri, preferred_element_type=F32)
        r = r_ref[h]
        a = jnp.concatenate(
            [jnp.exp(log_beta[:, j * LANE:(j + 1) * LANE] - e[:, j * LANE:(j + 1) * LANE] - r)
             for j in range(tk // LANE)], axis=-1)
        if diag is not None:
            a = jnp.where(strict, a, 0.0)
        acc_ref[h] += jnp.dot(a.astype(BF16), v_ref[0, pl.ds(k0, tk), lanes],
                              preferred_element_type=F32)
        r_ref[h] = r + e[:, tk:]

    def chunk(c, diag):
        k0 = pl.multiple_of(c * tk, tk)
        for h in range(N_HEADS):
            head_chunk(h, k0, diag)

    for d in reversed(range(chunks_per_q)):
        chunk(n_full + d, d)

    def full_chunk(i, carry):
        chunk(n_full - 1 - i, None)
        return carry

    lax.fori_loop(0, n_full, full_chunk, 0)
    for h in range(N_HEADS):
        o_ref[0, :, h * HEAD_DIM:(h + 1) * HEAD_DIM] = acc_ref[h].astype(o_ref.dtype)


def _sb_tri(tk):
    j_src = np.arange(tk)[:, None]
    j_dst = np.arange(tk + LANE)[None, :]
    return jnp.asarray((j_src > j_dst) | (j_dst >= tk), dtype=BF16)


def _sb_attention(qkv, q_blk, k_blk, v_blk, batch, seq):
    tq = _tile(seq, 512)
    tk = _tile(seq, 256)
    width = N_HEADS * HEAD_DIM
    return pl.pallas_call(
        functools.partial(_sb_attn_body, tq=tq, tk=tk),
        grid=(batch, seq // tq),
        in_specs=[
            pl.BlockSpec((1, tq, width), lambda b, i: (b, i, q_blk)),
            pl.BlockSpec((1, seq, width), lambda b, i: (b, 0, k_blk)),
            pl.BlockSpec((1, seq, width), lambda b, i: (b, 0, v_blk)),
            pl.BlockSpec((tk, tk + LANE), lambda b, i: (0, 0)),
        ],
        out_specs=pl.BlockSpec((1, tq, width), lambda b, i: (b, i, 0)),
        out_shape=jax.ShapeDtypeStruct((batch, seq, width), BF16),
        scratch_shapes=[
            pltpu.VMEM((N_HEADS, tq, HEAD_DIM), F32),
            pltpu.VMEM((N_HEADS, tq, HEAD_DIM), F32),
        ],
        compiler_params=_cparams("parallel", "arbitrary"),
        name="stick_breaking_attention",
    )(qkv, qkv, qkv, _sb_tri(tk))


def _pool_body(u_ref, halo_ref, w_ref, sc_ref, o_ref, ext_ref, *, ts):
    i = pl.program_id(1)
    halo = jnp.where(i > 0, halo_ref[0], 0.0)
    ext_ref[:POOL_HALO, :] = halo
    ext_ref[POOL_HALO:, :] = u_ref[0]
    pos = i * ts + lax.broadcasted_iota(jnp.int32, (ts, 1), 0)
    outs = []
    for g, win in enumerate(POOL_WINDOWS):
        lanes = slice(g * LANE, (g + 1) * LANE)
        tok = ext_ref[POOL_HALO:, lanes]
        tot = tok
        for back in range(1, win):
            tot = tot + ext_ref[POOL_HALO - back:POOL_HALO - back + ts, lanes]
        cnt = jnp.minimum(pos + 1, win).astype(F32)
        pooled = (tot / cnt - tok).astype(BF16)
        outs.append(jnp.dot(pooled, w_ref[g], preferred_element_type=F32))
    y = jnp.concatenate(outs, axis=-1) * sc_ref[...]
    o_ref[0] = y.astype(o_ref.dtype)


def _pool_mixer(latent3, col_block, w_pool, pool_scale, batch, seq):
    width = len(POOL_WINDOWS) * LANE
    ts = _tile(seq, 512)
    halo_per_tile = ts // POOL_HALO
    return pl.pallas_call(
        functools.partial(_pool_body, ts=ts),
        grid=(batch, seq // ts),
        in_specs=[
            pl.BlockSpec((1, ts, width), lambda b, i: (b, i, col_block)),
            pl.BlockSpec((1, POOL_HALO, width),
                         lambda b, i: (b, jnp.maximum(i * halo_per_tile - 1, 0), col_block)),
            pl.BlockSpec((len(POOL_WINDOWS), LANE, LANE), lambda b, i: (0, 0, 0)),
            pl.BlockSpec((1, width), lambda b, i: (0, 0)),
        ],
        out_specs=pl.BlockSpec((1, ts, width), lambda b, i: (b, i, 0)),
        out_shape=jax.ShapeDtypeStruct((batch, seq, width), BF16),
        scratch_shapes=[pltpu.VMEM((ts + POOL_HALO, width), F32)],
        compiler_params=_cparams("parallel", "arbitrary"),
        name="multiscale_pool",
    )(latent3, latent3, w_pool.astype(BF16), pool_scale.reshape(1, width))


def _merge_body(y0, y1, y2, y3, g0, g1, g2, g3, wb_ref, o_ref):
    acc = None
    for n, (y, g) in enumerate(((y0, g0), (y1, g1), (y2, g2), (y3, g3))):
        t = g[...].astype(F32) * jnp.dot(y[...], wb_ref[n], preferred_element_type=F32)
        acc = t if acc is None else acc + t
    o_ref[...] = acc.astype(o_ref.dtype)


def _merge(ys, gates, w_branch):
    n = ys[0].shape[0]
    d = w_branch.shape[2]
    tm = _tile(n, 1024)
    tn = _tile(d, 512)
    nj = d // tn
    y_spec = pl.BlockSpec((tm, BRANCH_WIDTH), lambda i, j: (i, 0))

    def g_spec(b):
        return pl.BlockSpec((tm, tn), lambda i, j: (i, b * nj + j))

    return pl.pallas_call(
        _merge_body,
        grid=(n // tm, nj),
        in_specs=[y_spec] * N_BRANCHES + [g_spec(b) for b in range(N_BRANCHES)] + [
            pl.BlockSpec((N_BRANCHES, BRANCH_WIDTH, tn), lambda i, j: (0, 0, j))],
        out_specs=pl.BlockSpec((tm, tn), lambda i, j: (i, j)),
        out_shape=jax.ShapeDtypeStruct((n, d), BF16),
        compiler_params=_cparams("parallel", "parallel"),
        name="gated_branch_merge",
    )(*ys, gates, gates, gates, gates, w_branch)


def _expert_changed(be_ref):
    blk = pl.program_id(1)
    return (blk == 0) | (be_ref[blk] != be_ref[jnp.maximum(blk - 1, 0)])


def _moe_up_body(be_ref, nb_ref, x_ref, wg_ref, wu_ref, o_ref, wg_s, wu_s):
    blk = pl.program_id(1)

    @pl.when(_expert_changed(be_ref))
    def _():
        wg_s[...] = wg_ref[0, 0].astype(BF16)
        wu_s[...] = wu_ref[0, 0].astype(BF16)

    @pl.when(blk < nb_ref[0])
    def _():
        x = x_ref[...]
        g = jnp.dot(x, wg_s[...], preferred_element_type=F32)
        u = jnp.dot(x, wu_s[...], preferred_element_type=F32)
        o_ref[...] = (g * jax.nn.sigmoid(g) * u).astype(o_ref.dtype)

    @pl.when(blk >= nb_ref[0])
    def _():
        o_ref[...] = jnp.zeros(o_ref.shape, o_ref.dtype)


def _moe_down_body(be_ref, nb_ref, a_ref, wd_ref, rw_ref, o_ref, wd_s):
    blk = pl.program_id(1)

    @pl.when(_expert_changed(be_ref))
    def _():
        wd_s[...] = wd_ref[0, 0].astype(BF16)

    @pl.when(blk < nb_ref[0])
    def _():
        y = jnp.dot(a_ref[...], wd_s[...], preferred_element_type=F32)
        o_ref[...] = y * rw_ref[...]

    @pl.when(blk >= nb_ref[0])
    def _():
        o_ref[...] = jnp.zeros(o_ref.shape, o_ref.dtype)


def _moe_ffn(x_buf, row_w, block_expert, n_used, w_gate, w_up, w_down, layer, tb):
    n_rows, d = x_buf.shape
    f = w_gate.shape[3]
    tf = _tile(f, 512)
    tn = _tile(d, 1024)
    n_blocks = n_rows // tb
    act = pl.pallas_call(
        _moe_up_body,
        grid_spec=pltpu.PrefetchScalarGridSpec(
            num_scalar_prefetch=2,
            grid=(f // tf, n_blocks),
            in_specs=[
                pl.BlockSpec((tb, d), lambda j, i, be, nb: (i, 0)),
                pl.BlockSpec((1, 1, d, tf), lambda j, i, be, nb: (layer, be[i], 0, j)),
                pl.BlockSpec((1, 1, d, tf), lambda j, i, be, nb: (layer, be[i], 0, j)),
            ],
            out_specs=pl.BlockSpec((tb, tf), lambda j, i, be, nb: (i, j)),
            scratch_shapes=[pltpu.VMEM((d, tf), BF16), pltpu.VMEM((d, tf), BF16)],
        ),
        out_shape=jax.ShapeDtypeStruct((n_rows, f), BF16),
        compiler_params=_cparams("arbitrary", "arbitrary"),
        name="expert_gate_up_swiglu",
    )(block_expert, n_used, x_buf, w_gate, w_up)
    return pl.pallas_call(
        _moe_down_body,
        grid_spec=pltpu.PrefetchScalarGridSpec(
            num_scalar_prefetch=2,
            grid=(d // tn, n_blocks),
            in_specs=[
                pl.BlockSpec((tb, f), lambda j, i, be, nb: (i, 0)),
                pl.BlockSpec((1, 1, f, tn), lambda j, i, be, nb: (layer, be[i], 0, j)),
                pl.BlockSpec((tb, 1), lambda j, i, be, nb: (i, 0)),
            ],
            out_specs=pl.BlockSpec((tb, tn), lambda j, i, be, nb: (i, j)),
            scratch_shapes=[pltpu.VMEM((f, tn), BF16)],
        ),
        out_shape=jax.ShapeDtypeStruct((n_rows, d), F32),
        compiler_params=_cparams("arbitrary", "arbitrary"),
        name="expert_down",
    )(block_expert, n_used, act, w_down, row_w)


def _final_norm_body(x_ref, g_ref, o_ref):
    x = x_ref[...]
    ms = jnp.mean(x * x, axis=-1, keepdims=True)
    o_ref[...] = x * lax.rsqrt(ms + NORM_EPS) * g_ref[...]


def _final_norm(x2, g):
    n, d = x2.shape
    tm = _tile(n, 512)
    return pl.pallas_call(
        _final_norm_body,
        grid=(n // tm,),
        in_specs=[pl.BlockSpec((tm, d), lambda i: (i, 0)), pl.BlockSpec((1, d), lambda i: (0, 0))],
        out_specs=pl.BlockSpec((tm, d), lambda i: (i, 0)),
        out_shape=jax.ShapeDtypeStruct((n, d), F32),
        compiler_params=_cparams("parallel"),
        name="final_rmsnorm",
    )(x2, g.reshape(1, d))


def _rope_tables(seq):
    half = MLA_ROPE_DIM // 2
    inv = jnp.power(ROPE_BASE, -2.0 * jnp.arange(half, dtype=F32) / MLA_ROPE_DIM)
    ang = jnp.arange(seq).astype(F32)[:, None] * inv[None, :]
    cos, sin = jnp.cos(ang), jnp.sin(ang)
    c64 = jnp.concatenate([cos, cos], axis=-1)
    s64 = jnp.concatenate([-sin, sin], axis=-1)
    return c64, s64


def _swap_halves(w):
    half = w.shape[-1] // 2
    return jnp.concatenate([w[..., half:], w[..., :half]], axis=-1)


def _pad_lanes(a, width):
    return jnp.pad(a, [(0, 0)] * (a.ndim - 1) + [(0, width - a.shape[-1])])


def _mixer_layer(x2, h, small, batch, seq, prm):
    n, d = x2.shape
    (w_qkv, w_lat, w_gates, b_forget, g_q, w_q, g_kv, w_kv, w_pool, pool_scale, w_branch, w_out,
     gate1) = prm
    c64, s64 = _rope_tables(seq)

    qkv = _matmul(h, w_qkv, BF16, name="fox_sb_qkv_proj").reshape(batch, seq, -1)
    latent = _matmul(h, w_lat, F32, tn=640, name="latent_pool_proj")
    gates = _matmul(h, w_gates, BF16, act="sigmoid", name="branch_gate_proj")

    fox_f = small[:, 2 * MLA_ROPE_DIM:2 * MLA_ROPE_DIM + N_HEADS].reshape(batch, seq, N_HEADS)
    log_f_cum = jnp.cumsum(jax.nn.log_sigmoid(fox_f + b_forget), axis=1)
    k_bias = -log_f_cum.transpose(0, 2, 1)[:, :, None, :]
    nh = N_HEADS
    y_fox = _softmax_attention(qkv, 0, qkv, 1, qkv, 2, batch, seq, key_bias=k_bias)

    y_sb = _sb_attention(qkv, 3, 4, 5, batch, seq)

    cos_t = jnp.tile(_pad_lanes(c64, HEAD_DIM), (1, nh))
    sin_t = jnp.tile(_pad_lanes(s64, HEAD_DIM), (1, nh))
    q_nope, q_rope = _mla_q(latent, 0, g_q, w_q, cos_t, sin_t, seq)
    kv = _mla_kv(latent, 4, g_kv, w_kv, seq).reshape(batch, seq, -1)
    kr = small[:, :MLA_ROPE_DIM] * jnp.tile(c64, (batch, 1)) \
        + small[:, MLA_ROPE_DIM:2 * MLA_ROPE_DIM] * jnp.tile(s64, (batch, 1))
    kr = _pad_lanes(kr.astype(BF16), HEAD_DIM).reshape(batch, seq, HEAD_DIM)
    y_mla = _softmax_attention(q_nope.reshape(batch, seq, -1), 0, kv, 0, kv, 1, batch, seq,
                               extra=(q_rope.reshape(batch, seq, -1), kr))

    y_pool = _pool_mixer(latent.reshape(batch, seq, -1), 1, w_pool, pool_scale, batch, seq)

    ys = [y.reshape(n, BRANCH_WIDTH) for y in (y_fox, y_sb, y_mla, y_pool)]
    merged = _merge(ys, gates, w_branch)
    return _matmul_residual(merged, w_out, x2, gate1, seq)


def _moe_layer(x2, h, logits, gate2, b_rg, b_re, w_gate, w_up, w_down, layer, batch, seq):
    n, d = x2.shape
    tb = 256
    g_logits = logits[:, :N_EXPERT_GROUPS] + b_rg
    g_idx = jnp.argmax(g_logits, axis=-1).astype(jnp.int32)[:, None]
    g_top = jnp.max(g_logits, axis=-1)
    p_group = jnp.exp(g_top - jax.nn.logsumexp(g_logits, axis=-1))
    e_logits = (logits[:, N_EXPERT_GROUPS:N_EXPERT_GROUPS + N_EXPERTS] + b_re).reshape(
        n, N_EXPERT_GROUPS, EXPERTS_PER_GROUP)
    e_logits = jnp.take_along_axis(e_logits, g_idx[:, :, None], axis=1)[:, 0]
    probs = jax.nn.softmax(e_logits, axis=-1)
    lane = jnp.arange(EXPERTS_PER_GROUP, dtype=jnp.int32)[None, :]
    i1 = jnp.argmax(probs, axis=-1).astype(jnp.int32)[:, None]
    p1 = jnp.max(probs, axis=-1, keepdims=True)
    rest = jnp.where(lane == i1, -jnp.inf, probs)
    i2 = jnp.argmax(rest, axis=-1).astype(jnp.int32)[:, None]
    p2 = jnp.max(rest, axis=-1, keepdims=True)
    top_p = jnp.concatenate([p1, p2], axis=-1)
    top_i = jnp.concatenate([i1, i2], axis=-1)
    weights = p_group[:, None] * top_p / jnp.sum(top_p, axis=-1, keepdims=True)

    expert_id = (g_idx * EXPERTS_PER_GROUP + top_i).reshape(-1).astype(jnp.int32)
    m = expert_id.shape[0]
    w_flat = weights.reshape(-1)

    order = jnp.argsort(expert_id).astype(jnp.int32)
    counts = jnp.sum(expert_id[:, None] == jnp.arange(N_EXPERTS, dtype=jnp.int32)[None, :],
                     axis=0).astype(jnp.int32)
    padded = ((counts + tb - 1) // tb) * tb
    start = jnp.cumsum(counts) - counts
    pend = jnp.cumsum(padded)
    pstart = pend - padded
    n_rows = m + N_EXPERTS * tb
    n_blocks = n_rows // tb
    block_start = jnp.arange(n_blocks, dtype=jnp.int32) * tb
    n_used = (pend[-1] // tb).astype(jnp.int32)
    block_expert = jnp.minimum(
        jnp.sum(pend[None, :] <= block_start[:, None], axis=1), N_EXPERTS - 1).astype(jnp.int32)
    last_used = block_expert[jnp.maximum(n_used - 1, 0)]
    block_expert = jnp.where(jnp.arange(n_blocks) < n_used, block_expert, last_used)

    row = jnp.arange(n_rows, dtype=jnp.int32)
    row_e = jnp.repeat(block_expert, tb)
    within = row - pstart[row_e]
    valid = (within < counts[row_e]) & (row < pend[-1])
    src_sorted = jnp.clip(start[row_e] + within, 0, m - 1)
    src_assign = order[src_sorted]
    row_tok = jnp.where(valid, src_assign // EXPERT_TOP_K, 0)
    row_w = jnp.where(valid, w_flat[src_assign], 0.0).astype(F32)

    e_sorted = expert_id[order]
    dest_sorted = pstart[e_sorted] + jnp.arange(m, dtype=jnp.int32) - start[e_sorted]
    dest = jnp.zeros((m,), jnp.int32).at[order].set(dest_sorted)

    x_buf = jnp.take(h, row_tok, axis=0)
    y = _moe_ffn(x_buf, row_w.reshape(n_rows, 1), block_expert, n_used.reshape(1),
                 w_gate, w_up, w_down, layer, tb)
    dest = dest.reshape(n, EXPERT_TOP_K)
    moe = jnp.take(y, dest[:, 0], axis=0) + jnp.take(y, dest[:, 1], axis=0)
    gate_rows = jnp.repeat(gate2[:, 0, :], seq, axis=0)
    return x2 + gate_rows * moe


def kernel(x, c, w_mod, b_mod, g_norm1, g_norm2, w_in, b_forget, g_q_norm, w_uq, g_kv_norm, w_ukv, w_pool, pool_scale, w_branch, w_out, w_route_group, b_route_group, w_route_expert, b_route_expert, w_gate, w_up, w_down, g_final):
    batch, seq, d = x.shape
    depth = w_mod.shape[0]
    n = batch * seq
    nh = N_HEADS

    c_pad = jnp.pad(c, ((0, 8 - batch), (0, 0)))
    mod = _modulation(c_pad, w_mod, b_mod)[:, :batch]

    hq = nh * HEAD_DIM
    o_fox, o_f = 0, 3 * hq
    o_sb = o_f + nh
    o_cq = o_sb + 3 * hq
    o_ckv = o_cq + MLA_Q_LORA
    o_kr = o_ckv + MLA_KV_LORA
    o_pool = o_kr + MLA_ROPE_DIM
    o_gate = o_pool + len(POOL_WINDOWS) * LANE
    attn_scale = HEAD_DIM ** -0.5
    mla_scale = (MLA_NOPE_DIM + MLA_ROPE_DIM) ** -0.5

    x2 = x.reshape(n, d)
    for l in range(depth):
        m6 = mod[l].reshape(batch, 6, 1, d)
        shift1, scale1, gate1, shift2, scale2, gate2 = [m6[:, t] for t in range(6)]
        wi = w_in[l]
        w_qkv = jnp.concatenate([
            wi[:, o_fox:o_fox + hq] * attn_scale, wi[:, o_fox + hq:o_fox + 3 * hq],
            wi[:, o_sb:o_sb + hq] * attn_scale, wi[:, o_sb + hq:o_sb + 3 * hq]], axis=1).astype(BF16)
        w_lat = jnp.concatenate([wi[:, o_cq:o_ckv], wi[:, o_pool:o_gate], wi[:, o_ckv:o_kr]],
                                axis=1).astype(BF16)
        w_gates = wi[:, o_gate:].astype(BF16)
        w_kr = wi[:, o_kr:o_pool]
        w_small1 = _pad_lanes(jnp.concatenate([w_kr, _swap_halves(w_kr), wi[:, o_f:o_sb]], axis=1),
                              SMALL_WIDTH)

        wq = w_uq[l].reshape(MLA_Q_LORA, nh, MLA_NOPE_DIM + MLA_ROPE_DIM) * mla_scale
        wq_rope = wq[:, :, MLA_NOPE_DIM:]
        w_q = jnp.concatenate([
            wq[:, :, :MLA_NOPE_DIM].reshape(MLA_Q_LORA, -1),
            _pad_lanes(wq_rope, HEAD_DIM).reshape(MLA_Q_LORA, -1),
            _pad_lanes(_swap_halves(wq_rope), HEAD_DIM).reshape(MLA_Q_LORA, -1)], axis=1).astype(BF16)
        wkv = w_ukv[l].reshape(MLA_KV_LORA, nh, 2 * HEAD_DIM)
        w_kv = jnp.concatenate([wkv[:, :, :HEAD_DIM].reshape(MLA_KV_LORA, -1),
                                wkv[:, :, HEAD_DIM:].reshape(MLA_KV_LORA, -1)], axis=1).astype(BF16)

        h, small = _norm_small(x2, g_norm1[l], scale1, shift1, w_small1, seq)
        prm = (w_qkv, w_lat, w_gates, b_forget[l], g_q_norm[l], w_q, g_kv_norm[l], w_kv,
               w_pool[l], pool_scale[l], w_branch[l].astype(BF16), w_out[l].astype(BF16), gate1)
        x2 = _mixer_layer(x2, h, small, batch, seq, prm)

        w_small2 = _pad_lanes(jnp.concatenate([w_route_group[l], w_route_expert[l]], axis=1),
                              SMALL_WIDTH)
        h, logits = _norm_small(x2, g_norm2[l], scale2, shift2, w_small2, seq)
        x2 = _moe_layer(x2, h, logits, gate2, b_route_group[l], b_route_expert[l],
                        w_gate, w_up, w_down, l, batch, seq)
    return _final_norm(x2, g_final).reshape(batch, seq, d)
```

```python
import functools

import numpy as np
import jax
import jax.numpy as jnp
from jax import lax
from jax.experimental import pallas as pl
from jax.experimental.pallas import tpu as pltpu

F32 = jnp.float32
BF16 = jnp.bfloat16

NORM_EPS = 1e-6
N_HEADS = 4
HEAD_DIM = 128
MLA_Q_LORA = 512
MLA_KV_LORA = 256
MLA_NOPE_DIM = 128
MLA_ROPE_DIM = 64
ROPE_BASE = 10000.0
POOL_WINDOWS = (2, 4, 8, 16)
POOL_HALO = 16
N_BRANCHES = 4
BRANCH_WIDTH = 512
N_EXPERT_GROUPS = 4
EXPERTS_PER_GROUP = 8
N_EXPERTS = N_EXPERT_GROUPS * EXPERTS_PER_GROUP
EXPERT_TOP_K = 2
SMALL_WIDTH = 256

LANE = 128
V7X_VMEM_BYTES = 64 * 1024 * 1024
VMEM_LIMIT = V7X_VMEM_BYTES - 12 * 1024 * 1024


def _cparams(*sem):
    return pltpu.CompilerParams(dimension_semantics=sem, vmem_limit_bytes=VMEM_LIMIT)


def _tile(n, pref):
    t = min(n, pref)
    assert n % t == 0, (n, pref)
    return t


def _mod_body(c_ref, w_ref, b_ref, o_ref):
    c = c_ref[...]
    ca = c * jax.nn.sigmoid(c)
    acc = jnp.dot(ca.astype(BF16), w_ref[0].astype(BF16), preferred_element_type=F32)
    o_ref[0] = acc + b_ref[0]


def _modulation(c_pad, w_mod, b_mod):
    depth, d, n = w_mod.shape
    tn = _tile(n, 1024)
    return pl.pallas_call(
        _mod_body,
        grid=(depth, n // tn),
        in_specs=[
            pl.BlockSpec((8, d), lambda l, j: (0, 0)),
            pl.BlockSpec((1, d, tn), lambda l, j: (l, 0, j)),
            pl.BlockSpec((1, 1, tn), lambda l, j: (l, 0, j)),
        ],
        out_specs=pl.BlockSpec((1, 8, tn), lambda l, j: (l, 0, j)),
        out_shape=jax.ShapeDtypeStruct((depth, 8, n), F32),
        compiler_params=_cparams("parallel", "parallel"),
        name="adaln_modulation",
    )(c_pad, w_mod, b_mod.reshape(depth, 1, n))


def _norm_small_body(x_ref, g_ref, sc_ref, sh_ref, w_ref, h_ref, s_ref):
    x = x_ref[...]
    ms = jnp.mean(x * x, axis=-1, keepdims=True)
    y = x * lax.rsqrt(ms + NORM_EPS) * g_ref[...]
    h = (y * (1.0 + sc_ref[0]) + sh_ref[0]).astype(BF16)
    h_ref[...] = h
    s_ref[...] = jnp.dot(h, w_ref[...], preferred_element_type=F32)


def _norm_small(x2, g, scale, shift, w_small, seq):
    n, d = x2.shape
    tm = _tile(seq, 512)
    per_b = seq // tm
    ws = w_small.shape[1]
    return pl.pallas_call(
        _norm_small_body,
        grid=(n // tm,),
        in_specs=[
            pl.BlockSpec((tm, d), lambda i: (i, 0)),
            pl.BlockSpec((1, d), lambda i: (0, 0)),
            pl.BlockSpec((1, 1, d), lambda i: (i // per_b, 0, 0)),
            pl.BlockSpec((1, 1, d), lambda i: (i // per_b, 0, 0)),
            pl.BlockSpec((d, ws), lambda i: (0, 0)),
        ],
        out_specs=[
            pl.BlockSpec((tm, d), lambda i: (i, 0)),
            pl.BlockSpec((tm, ws), lambda i: (i, 0)),
        ],
        out_shape=[
            jax.ShapeDtypeStruct((n, d), BF16),
            jax.ShapeDtypeStruct((n, ws), F32),
        ],
        compiler_params=_cparams("parallel"),
        name="norm_modulate_small_proj",
    )(x2, g.reshape(1, d), scale, shift, w_small.astype(BF16))


def _mm_body(a_ref, w_ref, o_ref, *, act):
    acc = jnp.dot(a_ref[...], w_ref[...], preferred_element_type=F32)
    if act == "sigmoid":
        acc = jax.nn.sigmoid(acc)
    o_ref[...] = acc.astype(o_ref.dtype)


def _matmul(a, w, out_dtype, act=None, tm=1024, tn=1024, name="matmul"):
    m, k = a.shape
    n = w.shape[1]
    tm = _tile(m, tm)
    tn = _tile(n, tn)
    return pl.pallas_call(
        functools.partial(_mm_body, act=act),
        grid=(m // tm, n // tn),
        in_specs=[
            pl.BlockSpec((tm, k), lambda i, j: (i, 0)),
            pl.BlockSpec((k, tn), lambda i, j: (0, j)),
        ],
        out_specs=pl.BlockSpec((tm, tn), lambda i, j: (i, j)),
        out_shape=jax.ShapeDtypeStruct((m, n), out_dtype),
        compiler_params=_cparams("parallel", "parallel"),
        name=name,
    )(a, w)


def _mm_residual_body(a_ref, w_ref, x_ref, gate_ref, o_ref):
    acc = jnp.dot(a_ref[...], w_ref[...], preferred_element_type=F32)
    o_ref[...] = x_ref[...] + gate_ref[0] * acc


def _matmul_residual(a, w, x2, gate, seq, tm=1024, tn=1024):
    m, k = a.shape
    n = w.shape[1]
    tm = _tile(seq, tm)
    tn = _tile(n, tn)
    per_b = seq // tm
    return pl.pallas_call(
        _mm_residual_body,
        grid=(m // tm, n // tn),
        in_specs=[
            pl.BlockSpec((tm, k), lambda i, j: (i, 0)),
            pl.BlockSpec((k, tn), lambda i, j: (0, j)),
            pl.BlockSpec((tm, tn), lambda i, j: (i, j)),
            pl.BlockSpec((1, 1, tn), lambda i, j: (i // per_b, 0, j)),
        ],
        out_specs=pl.BlockSpec((tm, tn), lambda i, j: (i, j)),
        out_shape=jax.ShapeDtypeStruct((m, n), F32),
        compiler_params=_cparams("parallel", "parallel"),
        name="out_proj_residual",
    )(a, w, x2, gate)


def _latent_norm(a_ref, g_ref):
    a = a_ref[...]
    ms = jnp.mean(a * a, axis=-1, keepdims=True)
    return (a * lax.rsqrt(ms + NORM_EPS) * g_ref[...]).astype(BF16)


def _mla_q_body(a_ref, g_ref, w_ref, cos_ref, sin_ref, nope_ref, rope_ref):
    acc = jnp.dot(_latent_norm(a_ref, g_ref), w_ref[...], preferred_element_type=F32)
    w = nope_ref.shape[1]
    nope_ref[...] = acc[:, :w].astype(BF16)
    rope_ref[...] = (acc[:, w:2 * w] * cos_ref[...] + acc[:, 2 * w:] * sin_ref[...]).astype(BF16)


def _mla_q(latent, col_block, g, w_q, cos_t, sin_t, seq):
    n = latent.shape[0]
    k = w_q.shape[0]
    w = w_q.shape[1] // 3
    tm = _tile(seq, 1024)
    per_b = seq // tm
    return pl.pallas_call(
        _mla_q_body,
        grid=(n // tm,),
        in_specs=[
            pl.BlockSpec((tm, k), lambda i: (i, col_block)),
            pl.BlockSpec((1, k), lambda i: (0, 0)),
            pl.BlockSpec((k, 3 * w), lambda i: (0, 0)),
            pl.BlockSpec((tm, w), lambda i: (i % per_b, 0)),
            pl.BlockSpec((tm, w), lambda i: (i % per_b, 0)),
        ],
        out_specs=[
            pl.BlockSpec((tm, w), lambda i: (i, 0)),
            pl.BlockSpec((tm, w), lambda i: (i, 0)),
        ],
        out_shape=[
            jax.ShapeDtypeStruct((n, w), BF16),
            jax.ShapeDtypeStruct((n, w), BF16),
        ],
        compiler_params=_cparams("parallel"),
        name="mla_q_up_rope",
    )(latent, g.reshape(1, k), w_q, cos_t, sin_t)


def _mla_kv_body(a_ref, g_ref, w_ref, o_ref):
    o_ref[...] = jnp.dot(_latent_norm(a_ref, g_ref), w_ref[...],
                         preferred_element_type=F32).astype(BF16)


def _mla_kv(latent, col_block, g, w_kv, seq):
    n = latent.shape[0]
    k, nout = w_kv.shape
    tm = _tile(seq, 1024)
    return pl.pallas_call(
        _mla_kv_body,
        grid=(n // tm,),
        in_specs=[
            pl.BlockSpec((tm, k), lambda i: (i, col_block)),
            pl.BlockSpec((1, k), lambda i: (0, 0)),
            pl.BlockSpec((k, nout), lambda i: (0, 0)),
        ],
        out_specs=pl.BlockSpec((tm, nout), lambda i: (i, 0)),
        out_shape=jax.ShapeDtypeStruct((n, nout), BF16),
        compiler_params=_cparams("parallel"),
        name="mla_kv_up",
    )(latent, g.reshape(1, k), w_kv)


def _softmax_attn_body(*refs, tq, tk, has_extra, has_bias):
    refs = list(refs)
    qm_ref = refs.pop(0)
    qe_ref = refs.pop(0) if has_extra else None
    km_ref = refs.pop(0)
    ke_ref = refs.pop(0) if has_extra else None
    v_ref = refs.pop(0)
    kb_ref = refs.pop(0) if has_bias else None
    o_ref, m_ref, acc_ref = refs
    qi = pl.program_id(1)
    m_ref[...] = jnp.full(m_ref.shape, -jnp.inf, F32)
    acc_ref[...] = jnp.zeros(acc_ref.shape, F32)
    chunks_per_q = tq // tk
    n_full = qi * chunks_per_q
    ones = jnp.ones((tk, HEAD_DIM), BF16)

    def head_chunk(h, k0, diag):
        lanes = slice(h * HEAD_DIM, (h + 1) * HEAD_DIM)
        q = qm_ref[0, :, lanes]
        k = km_ref[0, pl.ds(k0, tk), lanes]
        if has_extra:
            q = jnp.concatenate([q, qe_ref[0, :, lanes]], axis=-1)
            k = jnp.concatenate([k, ke_ref[0, pl.ds(k0, tk), :]], axis=-1)
        s = lax.dot_general(q, k, (((1,), (1,)), ((), ())), preferred_element_type=F32)
        if has_bias:
            s = s + kb_ref[0, h, :, pl.ds(k0, tk)]
        if diag is not None:
            row = lax.broadcasted_iota(jnp.int32, (tq, tk), 0)
            col = lax.broadcasted_iota(jnp.int32, (tq, tk), 1) + diag * tk
            s = jnp.where(row >= col, s, -jnp.inf)
        blocks = [s[:, j * LANE:(j + 1) * LANE] for j in range(tk // LANE)]
        mx = blocks[0]
        for blk in blocks[1:]:
            mx = jnp.maximum(mx, blk)
        m_prev = m_ref[h]
        m_new = jnp.maximum(m_prev, jnp.max(mx, axis=-1, keepdims=True))
        alpha = jnp.exp(m_prev - m_new)
        p = jnp.concatenate([jnp.exp(blk - m_new) for blk in blocks], axis=-1).astype(BF16)
        v_ext = jnp.concatenate([v_ref[0, pl.ds(k0, tk), lanes], ones], axis=-1)
        pv = jnp.dot(p, v_ext, preferred_element_type=F32)
        acc_ref[h, :, :HEAD_DIM] = alpha * acc_ref[h, :, :HEAD_DIM] + pv[:, :HEAD_DIM]
        acc_ref[h, :, HEAD_DIM:] = alpha * acc_ref[h, :, HEAD_DIM:] + pv[:, HEAD_DIM:]
        m_ref[h] = m_new

    def chunk(c, diag):
        k0 = pl.multiple_of(c * tk, tk)
        for h in range(N_HEADS):
            head_chunk(h, k0, diag)

    def full_chunk(c, carry):
        chunk(c, None)
        return carry

    lax.fori_loop(0, n_full, full_chunk, 0)
    for d in range(chunks_per_q):
        chunk(n_full + d, d)
    for h in range(N_HEADS):
        o_ref[0, :, h * HEAD_DIM:(h + 1) * HEAD_DIM] = (
            acc_ref[h, :, :HEAD_DIM] / acc_ref[h, :, HEAD_DIM:]).astype(o_ref.dtype)


def _softmax_attention(qm, q_blk, km, k_blk, v, v_blk, batch, seq, extra=None, key_bias=None):
    tq = _tile(seq, 512)
    tk = _tile(seq, 512)
    width = N_HEADS * HEAD_DIM
    args = [qm]
    specs = [pl.BlockSpec((1, tq, width), lambda b, i: (b, i, q_blk))]
    if extra is not None:
        args.append(extra[0])
        specs.append(pl.BlockSpec((1, tq, width), lambda b, i: (b, i, 0)))
    args.append(km)
    specs.append(pl.BlockSpec((1, seq, width), lambda b, i: (b, 0, k_blk)))
    if extra is not None:
        args.append(extra[1])
        specs.append(pl.BlockSpec((1, seq, HEAD_DIM), lambda b, i: (b, 0, 0)))
    args.append(v)
    specs.append(pl.BlockSpec((1, seq, width), lambda b, i: (b, 0, v_blk)))
    if key_bias is not None:
        args.append(key_bias)
        specs.append(pl.BlockSpec((1, N_HEADS, 1, seq), lambda b, i: (b, 0, 0, 0)))

    return pl.pallas_call(
        functools.partial(_softmax_attn_body, tq=tq, tk=tk, has_extra=extra is not None,
                          has_bias=key_bias is not None),
        grid=(batch, seq // tq),
        in_specs=specs,
        out_specs=pl.BlockSpec((1, tq, width), lambda b, i: (b, i, 0)),
        out_shape=jax.ShapeDtypeStruct((batch, seq, width), BF16),
        scratch_shapes=[
            pltpu.VMEM((N_HEADS, tq, HEAD_DIM), F32),
            pltpu.VMEM((N_HEADS, tq, 2 * HEAD_DIM), F32),
        ],
        compiler_params=_cparams("parallel", "arbitrary"),
        name="causal_softmax_attention",
    )(*args)


def _sb_attn_body(q_ref, k_ref, v_ref, tri_ref, o_ref, r_ref, acc_ref, *, tq, tk):
    qi = pl.program_id(1)
    r_ref[...] = jnp.zeros(r_ref.shape, F32)
    acc_ref[...] = jnp.zeros(acc_ref.shape, F32)
    chunks_per_q = tq // tk
    n_full = qi * chunks_per_q

    def head_chunk(h, k0, diag):
        lanes = slice(h * HEAD_DIM, (h + 1) * HEAD_DIM)
        z = lax.dot_general(q_ref[0, :, lanes], k_ref[0, pl.ds(k0, tk), lanes],
                            (((1,), (1,)), ((), ())), preferred_element_type=F32)
        sp = jnp.maximum(z, 0.0) + jnp.log(1.0 + jnp.exp(-jnp.abs(z)))
        log_beta = z - sp
        if diag is not None:
            row = lax.broadcasted_iota(jnp.int32, (tq, tk), 0)
            col = lax.broadcasted_iota(jnp.int32, (tq, tk), 1) + diag * tk
            strict = row > col
            sp = jnp.where(strict, sp, 0.0)
        hi = sp.astype(BF16)
        lo = (sp - hi.astype(F32)).astype(BF16)
        tri = tri_ref[...]
        e =jnp.dot(hi, tri, preferred_element_type=F32) + jnp.dot(lo, tri, preferred_element_type=F32)
        r = r_ref[h]
        a = jnp.concatenate(
            [jnp.exp(log_beta[:, j * LANE:(j + 1) * LANE] - e[:, j * LANE:(j + 1) * LANE] - r)
             for j in range(tk // LANE)], axis=-1)
        if diag is not None:
            a = jnp.where(strict, a, 0.0)
        acc_ref[h] += jnp.dot(a.astype(BF16), v_ref[0, pl.ds(k0, tk), lanes],
                              preferred_element_type=F32)
        r_ref[h] = r + e[:, tk:]

    def chunk(c, diag):
        k0 = pl.multiple_of(c * tk, tk)
        for h in range(N_HEADS):
            head_chunk(h, k0, diag)

    for d in reversed(range(chunks_per_q)):
        chunk(n_full + d, d)

    def full_chunk(i, carry):
        chunk(n_full - 1 - i, None)
        return carry

    lax.fori_loop(0, n_full, full_chunk, 0)
    for h in range(N_HEADS):
        o_ref[0, :, h * HEAD_DIM:(h + 1) * HEAD_DIM] = acc_ref[h].astype(o_ref.dtype)


def _sb_tri(tk):
    j_src = np.arange(tk)[:, None]
    j_dst = np.arange(tk + LANE)[None, :]
    return jnp.asarray((j_src > j_dst) | (j_dst >= tk), dtype=BF16)


def _sb_attention(qkv, q_blk, k_blk, v_blk, batch, seq):
    tq = _tile(seq, 512)
    tk = _tile(seq, 256)
    width = N_HEADS * HEAD_DIM
    return pl.pallas_call(
        functools.partial(_sb_attn_body, tq=tq, tk=tk),
        grid=(batch, seq // tq),
        in_specs=[
            pl.BlockSpec((1, tq, width), lambda b, i: (b, i, q_blk)),
            pl.BlockSpec((1, seq, width), lambda b, i: (b, 0, k_blk)),
            pl.BlockSpec((1, seq, width), lambda b, i: (b, 0, v_blk)),
            pl.BlockSpec((tk, tk + LANE), lambda b, i: (0, 0)),
        ],
        out_specs=pl.BlockSpec((1, tq, width), lambda b, i: (b, i, 0)),
        out_shape=jax.ShapeDtypeStruct((batch, seq, width), BF16),
        scratch_shapes=[
            pltpu.VMEM((N_HEADS, tq, HEAD_DIM), F32),
            pltpu.VMEM((N_HEADS, tq, HEAD_DIM), F32),
        ],
        compiler_params=_cparams("parallel", "arbitrary"),
        name="stick_breaking_attention",
    )(qkv, qkv, qkv, _sb_tri(tk))


def _pool_body(u_ref, halo_ref, w_ref, sc_ref, o_ref, ext_ref, *, ts):
    i = pl.program_id(1)
    halo = jnp.where(i > 0, halo_ref[0], 0.0)
    ext_ref[:POOL_HALO, :] = halo
    ext_ref[POOL_HALO:, :] = u_ref[0]
    pos = i * ts + lax.broadcasted_iota(jnp.int32, (ts, 1), 0)
    outs = []
    for g, win in enumerate(POOL_WINDOWS):
        lanes = slice(g * LANE, (g + 1) * LANE)
        tok = ext_ref[POOL_HALO:, lanes]
        tot = tok
        for back in range(1, win):
            tot = tot + ext_ref[POOL_HALO - back:POOL_HALO - back + ts, lanes]
        cnt = jnp.minimum(pos + 1, win).astype(F32)
        pooled = (tot / cnt - tok).astype(BF16)
        outs.append(jnp.dot(pooled, w_ref[g], preferred_element_type=F32))
    y = jnp.concatenate(outs, axis=-1) * sc_ref[...]
    o_ref[0] = y.astype(o_ref.dtype)


def _pool_mixer(latent3, col_block, w_pool, pool_scale, batch, seq):
    width = len(POOL_WINDOWS) * LANE
    ts = _tile(seq, 512)
    halo_per_tile = ts // POOL_HALO
    return pl.pallas_call(
        functools.partial(_pool_body, ts=ts),
        grid=(batch, seq // ts),
        in_specs=[
            pl.BlockSpec((1, ts, width), lambda b, i: (b, i, col_block)),
            pl.BlockSpec((1, POOL_HALO, width),
                         lambda b, i: (b, jnp.maximum(i * halo_per_tile - 1, 0), col_block)),
            pl.BlockSpec((len(POOL_WINDOWS), LANE, LANE), lambda b, i: (0, 0, 0)),
            pl.BlockSpec((1, width), lambda b, i: (0, 0)),
        ],
        out_specs=pl.BlockSpec((1, ts, width), lambda b, i: (b, i, 0)),
        out_shape=jax.ShapeDtypeStruct((batch, seq, width), BF16),
        scratch_shapes=[pltpu.VMEM((ts + POOL_HALO, width), F32)],
        compiler_params=_cparams("parallel", "arbitrary"),
        name="multiscale_pool",
    )(latent3, latent3, w_pool.astype(BF16), pool_scale.reshape(1, width))


def _merge_body(y0, y1, y2, y3, g0, g1, g2, g3, wb_ref, o_ref):
    acc = None
    for n, (y, g) in enumerate(((y0, g0), (y1, g1), (y2, g2), (y3, g3))):
        t = g[...].astype(F32) * jnp.dot(y[...], wb_ref[n], preferred_element_type=F32)
        acc = t if acc is None else acc + t
    o_ref[...] = acc.astype(o_ref.dtype)


def _merge(ys, gates, w_branch):
    n = ys[0].shape[0]
    d = w_branch.shape[2]
    tm = _tile(n, 1024)
    tn = _tile(d, 512)
    nj = d // tn
    y_spec = pl.BlockSpec((tm, BRANCH_WIDTH), lambda i, j: (i, 0))

    def g_spec(b):
        return pl.BlockSpec((tm, tn), lambda i, j: (i, b * nj + j))

    return pl.pallas_call(
        _merge_body,
        grid=(n // tm, nj),
        in_specs=[y_spec] * N_BRANCHES + [g_spec(b) for b in range(N_BRANCHES)] + [
            pl.BlockSpec((N_BRANCHES, BRANCH_WIDTH, tn), lambda i, j: (0, 0, j))],
        out_specs=pl.BlockSpec((tm, tn), lambda i, j: (i, j)),
        out_shape=jax.ShapeDtypeStruct((n, d), BF16),
        compiler_params=_cparams("parallel", "parallel"),
        name="gated_branch_merge",
    )(*ys, gates, gates, gates, gates, w_branch)


def _expert_changed(be_ref):
    blk = pl.program_id(1)
    return (blk == 0) | (be_ref[blk] != be_ref[jnp.maximum(blk - 1, 0)])


def _moe_up_body(be_ref, nb_ref, x_ref, wg_ref, wu_ref, o_ref, wg_s, wu_s):
    blk = pl.program_id(1)

    @pl.when(_expert_changed(be_ref))
    def _():
        wg_s[...] = wg_ref[0, 0].astype(BF16)
        wu_s[...] = wu_ref[0, 0].astype(BF16)

    @pl.when(blk < nb_ref[0])
    def _():
        x = x_ref[...]
        g = jnp.dot(x, wg_s[...], preferred_element_type=F32)
        u = jnp.dot(x, wu_s[...], preferred_element_type=F32)
        o_ref[...] = (g * jax.nn.sigmoid(g) * u).astype(o_ref.dtype)

    @pl.when(blk >= nb_ref[0])
    def _():
        o_ref[...] = jnp.zeros(o_ref.shape, o_ref.dtype)


def _moe_down_body(be_ref, nb_ref, a_ref, wd_ref, rw_ref, o_ref, wd_s):
    blk = pl.program_id(1)

    @pl.when(_expert_changed(be_ref))
    def _():
        wd_s[...] = wd_ref[0, 0].astype(BF16)

    @pl.when(blk < nb_ref[0])
    def _():
        y = jnp.dot(a_ref[...], wd_s[...], preferred_element_type=F32)
        o_ref[...] = (y * rw_ref[...]).astype(o_ref.dtype)

    @pl.when(blk >= nb_ref[0])
    def _():
        o_ref[...] = jnp.zeros(o_ref.shape, o_ref.dtype)


def _moe_ffn(x_buf, row_w, block_expert, n_used, w_gate, w_up, w_down, layer, tb):
    n_rows, d = x_buf.shape
    f = w_gate.shape[3]
    tf = _tile(f, 512)
    tn = _tile(d, 1024)
    n_blocks = n_rows // tb
    act = pl.pallas_call(
        _moe_up_body,
        grid_spec=pltpu.PrefetchScalarGridSpec(
            num_scalar_prefetch=2,
            grid=(f // tf, n_blocks),
            in_specs=[
                pl.BlockSpec((tb, d), lambda j, i, be, nb: (i, 0)),
                pl.BlockSpec((1, 1, d, tf), lambda j, i, be, nb: (layer, be[i], 0, j)),
                pl.BlockSpec((1, 1, d, tf), lambda j, i, be, nb: (layer, be[i], 0, j)),
            ],
            out_specs=pl.BlockSpec((tb, tf), lambda j, i, be, nb: (i, j)),
            scratch_shapes=[pltpu.VMEM((d, tf), BF16), pltpu.VMEM((d, tf), BF16)],
        ),
        out_shape=jax.ShapeDtypeStruct((n_rows, f), BF16),
        compiler_params=_cparams("arbitrary", "arbitrary"),
        name="expert_gate_up_swiglu",
    )(block_expert, n_used, x_buf, w_gate, w_up)
    return pl.pallas_call(
        _moe_down_body,
        grid_spec=pltpu.PrefetchScalarGridSpec(
            num_scalar_prefetch=2,
            grid=(d // tn, n_blocks),
            in_specs=[
                pl.BlockSpec((tb, f), lambda j, i, be, nb: (i, 0)),
                pl.BlockSpec((1, 1, f, tn), lambda j, i, be, nb: (layer, be[i], 0, j)),
                pl.BlockSpec((tb, 1), lambda j, i, be, nb: (i, 0)),
            ],
            out_specs=pl.BlockSpec((tb, tn), lambda j, i, be, nb: (i, j)),
            scratch_shapes=[pltpu.VMEM((f, tn), BF16)],
        ),
        out_shape=jax.ShapeDtypeStruct((n_rows, d), BF16),
        compiler_params=_cparams("arbitrary", "arbitrary"),
        name="expert_down",
    )(block_expert, n_used, act, w_down, row_w)


def _final_norm_body(x_ref, g_ref, o_ref):
    x = x_ref[...]
    ms = jnp.mean(x * x, axis=-1, keepdims=True)
    o_ref[...] = x * lax.rsqrt(ms + NORM_EPS) * g_ref[...]


def _final_norm(x2, g):
    n, d = x2.shape
    tm = _tile(n, 512)
    return pl.pallas_call(
        _final_norm_body,
        grid=(n // tm,),
        in_specs=[pl.BlockSpec((tm, d), lambda i: (i, 0)), pl.BlockSpec((1, d), lambda i: (0, 0))],
        out_specs=pl.BlockSpec((tm, d), lambda i: (i, 0)),
        out_shape=jax.ShapeDtypeStruct((n, d), F32),
        compiler_params=_cparams("parallel"),
        name="final_rmsnorm",
    )(x2, g.reshape(1, d))


def _rope_tables(seq):
    half = MLA_ROPE_DIM // 2
    inv = jnp.power(ROPE_BASE, -2.0 * jnp.arange(half, dtype=F32) / MLA_ROPE_DIM)
    ang = jnp.arange(seq).astype(F32)[:, None] * inv[None, :]
    cos, sin = jnp.cos(ang), jnp.sin(ang)
    c64 = jnp.concatenate([cos, cos], axis=-1)
    s64 = jnp.concatenate([-sin, sin], axis=-1)
    return c64, s64


def _swap_halves(w):
    half = w.shape[-1] // 2
    return jnp.concatenate([w[..., half:], w[..., :half]], axis=-1)


def _pad_lanes(a, width):
    return jnp.pad(a, [(0, 0)] * (a.ndim - 1) + [(0, width - a.shape[-1])])


def _mixer_layer(x2, h, small, batch, seq, prm):
    n, d = x2.shape
    (w_qkv, w_lat, w_gates, b_forget, g_q, w_q, g_kv, w_kv, w_pool, pool_scale, w_branch, w_out,
     gate1) = prm
    c64, s64 = _rope_tables(seq)

    qkv = _matmul(h, w_qkv, BF16, name="fox_sb_qkv_proj").reshape(batch, seq, -1)
    latent = _matmul(h, w_lat, F32, tn=640, name="latent_pool_proj")
    gates = _matmul(h, w_gates, BF16, act="sigmoid", name="branch_gate_proj")

    fox_f = small[:, 2 * MLA_ROPE_DIM:2 * MLA_ROPE_DIM + N_HEADS].reshape(batch, seq, N_HEADS)
    log_f_cum = jnp.cumsum(jax.nn.log_sigmoid(fox_f + b_forget), axis=1)
    k_bias = -log_f_cum.transpose(0, 2, 1)[:, :, None, :]
    nh = N_HEADS
    y_fox = _softmax_attention(qkv, 0, qkv, 1, qkv, 2, batch, seq, key_bias=k_bias)

    y_sb = _sb_attention(qkv, 3, 4, 5, batch, seq)

    cos_t = jnp.tile(_pad_lanes(c64, HEAD_DIM), (1, nh))
    sin_t = jnp.tile(_pad_lanes(s64, HEAD_DIM), (1, nh))
    q_nope, q_rope = _mla_q(latent, 0, g_q, w_q, cos_t, sin_t, seq)
    kv = _mla_kv(latent, 4, g_kv, w_kv, seq).reshape(batch, seq, -1)
    kr = small[:, :MLA_ROPE_DIM] * jnp.tile(c64, (batch, 1)) \
        + small[:, MLA_ROPE_DIM:2 * MLA_ROPE_DIM] * jnp.tile(s64, (batch, 1))
    kr = _pad_lanes(kr.astype(BF16), HEAD_DIM).reshape(batch, seq, HEAD_DIM)
    y_mla = _softmax_attention(q_nope.reshape(batch, seq, -1), 0, kv, 0, kv, 1, batch, seq,
                               extra=(q_rope.reshape(batch, seq, -1), kr))

    y_pool = _pool_mixer(latent.reshape(batch, seq, -1), 1, w_pool, pool_scale, batch, seq)

    ys = [y.reshape(n, BRANCH_WIDTH) for y in (y_fox, y_sb, y_mla, y_pool)]
    merged = _merge(ys, gates, w_branch)
    return _matmul_residual(merged, w_out, x2, gate1, seq)


def _moe_layer(x2, h, logits, gate2, b_rg, b_re, w_gate, w_up, w_down, layer, batch, seq):
    n, d = x2.shape
    tb = 512
    g_logits = logits[:, :N_EXPERT_GROUPS] + b_rg
    g_idx = jnp.argmax(g_logits, axis=-1).astype(jnp.int32)[:, None]
    g_top = jnp.max(g_logits, axis=-1)
    p_group = jnp.exp(g_top - jax.nn.logsumexp(g_logits, axis=-1))
    e_logits = (logits[:, N_EXPERT_GROUPS:N_EXPERT_GROUPS + N_EXPERTS] + b_re).reshape(
        n, N_EXPERT_GROUPS, EXPERTS_PER_GROUP)
    e_logits = jnp.take_along_axis(e_logits, g_idx[:, :, None], axis=1)[:, 0]
    probs = jax.nn.softmax(e_logits, axis=-1)
    lane = jnp.arange(EXPERTS_PER_GROUP, dtype=jnp.int32)[None, :]
    i1 = jnp.argmax(probs, axis=-1).astype(jnp.int32)[:, None]
    p1 = jnp.max(probs, axis=-1, keepdims=True)
    rest = jnp.where(lane == i1, -jnp.inf, probs)
    i2 = jnp.argmax(rest, axis=-1).astype(jnp.int32)[:, None]
    p2 = jnp.max(rest, axis=-1, keepdims=True)
    top_p = jnp.concatenate([p1, p2], axis=-1)
    top_i = jnp.concatenate([i1, i2], axis=-1)
    weights = p_group[:, None] * top_p / jnp.sum(top_p, axis=-1, keepdims=True)

    expert_id = (g_idx * EXPERTS_PER_GROUP + top_i).reshape(-1).astype(jnp.int32)
    m = expert_id.shape[0]
    w_flat = weights.reshape(-1)

    order = jnp.argsort(expert_id).astype(jnp.int32)
    rank = jnp.argsort(order).astype(jnp.int32)
    onehot = expert_id[:, None] == jnp.arange(N_EXPERTS, dtype=jnp.int32)[None, :]
    counts = jnp.sum(onehot, axis=0, dtype=jnp.int32)
    padded = ((counts + tb - 1) // tb) * tb
    start = jnp.cumsum(counts) - counts
    pend = jnp.cumsum(padded)
    shift = pend - padded - start
    n_rows = m + N_EXPERTS * tb
    n_blocks = n_rows // tb
    blk = jnp.arange(n_blocks, dtype=jnp.int32)
    n_used = (pend[-1] // tb).astype(jnp.int32)
    block_expert = jnp.minimum(
        jnp.sum(pend[None, :] <= (blk * tb)[:, None], axis=1), N_EXPERTS - 1).astype(jnp.int32)
    last_used = block_expert[jnp.maximum(n_used - 1, 0)]
    block_expert = jnp.where(blk < n_used, block_expert, last_used)

    rows = jnp.arange(n_rows, dtype=jnp.int32).reshape(n_blocks, tb)
    src_sorted = rows - shift[block_expert][:, None]
    valid = (src_sorted < (start + counts)[block_expert][:, None]) & (blk < n_used)[:, None]
    src_assign = order.at[jnp.clip(src_sorted, 0, m - 1).reshape(-1)].get(mode="promise_in_bounds")
    valid = valid.reshape(-1)
    row_tok = jnp.where(valid, src_assign // EXPERT_TOP_K, 0)
    row_w = jnp.where(valid, w_flat.at[src_assign].get(mode="promise_in_bounds"), 0.0).astype(F32)
    dest = (rank + jnp.sum(jnp.where(onehot, shift[None, :], 0), axis=1)).reshape(n, EXPERT_TOP_K)

    x_buf = h.at[row_tok].get(mode="promise_in_bounds")
    y = _moe_ffn(x_buf, row_w.reshape(n_rows, 1), block_expert, n_used.reshape(1),
                 w_gate, w_up, w_down, layer, tb)
    moe = (y.at[dest[:, 0]].get(mode="promise_in_bounds").astype(F32)
           + y.at[dest[:, 1]].get(mode="promise_in_bounds").astype(F32))
    return (x2.reshape(batch, seq, d) + gate2 * moe.reshape(batch, seq, d)).reshape(n, d)


def kernel(x, c, w_mod, b_mod, g_norm1, g_norm2, w_in, b_forget, g_q_norm, w_uq, g_kv_norm, w_ukv, w_pool, pool_scale, w_branch, w_out, w_route_group, b_route_group, w_route_expert, b_route_expert, w_gate, w_up, w_down, g_final):
    batch, seq, d = x.shape
    depth = w_mod.shape[0]
    n = batch * seq
    nh = N_HEADS

    c_pad = jnp.pad(c, ((0, 8 - batch), (0, 0)))
    mod = _modulation(c_pad, w_mod, b_mod)[:, :batch]

    hq = nh * HEAD_DIM
    o_fox, o_f = 0, 3 * hq
    o_sb = o_f + nh
    o_cq = o_sb + 3 * hq
    o_ckv = o_cq + MLA_Q_LORA
    o_kr = o_ckv + MLA_KV_LORA
    o_pool = o_kr + MLA_ROPE_DIM
    o_gate = o_pool + len(POOL_WINDOWS) * LANE
    attn_scale = HEAD_DIM ** -0.5
    mla_scale = (MLA_NOPE_DIM + MLA_ROPE_DIM) ** -0.5

    x2 = x.reshape(n, d)
    for l in range(depth):
        m6 = mod[l].reshape(batch, 6, 1, d)
        shift1, scale1, gate1, shift2, scale2, gate2 = [m6[:, t] for t in range(6)]
        wi = w_in[l]
        w_qkv = jnp.concatenate([
            wi[:, o_fox:o_fox + hq] * attn_scale, wi[:, o_fox + hq:o_fox + 3 * hq],
            wi[:, o_sb:o_sb + hq] * attn_scale, wi[:, o_sb + hq:o_sb + 3 * hq]], axis=1).astype(BF16)
        w_lat = jnp.concatenate([wi[:, o_cq:o_ckv], wi[:, o_pool:o_gate], wi[:, o_ckv:o_kr]],
                                axis=1).astype(BF16)
        w_gates = wi[:, o_gate:].astype(BF16)
        w_kr = wi[:, o_kr:o_pool]
        w_small1 = _pad_lanes(jnp.concatenate([w_kr, _swap_halves(w_kr), wi[:, o_f:o_sb]], axis=1),
                              SMALL_WIDTH)

        wq = w_uq[l].reshape(MLA_Q_LORA, nh, MLA_NOPE_DIM + MLA_ROPE_DIM) * mla_scale
        wq_rope = wq[:, :, MLA_NOPE_DIM:]
        w_q = jnp.concatenate([
            wq[:, :, :MLA_NOPE_DIM].reshape(MLA_Q_LORA, -1),
            _pad_lanes(wq_rope, HEAD_DIM).reshape(MLA_Q_LORA, -1),
            _pad_lanes(_swap_halves(wq_rope), HEAD_DIM).reshape(MLA_Q_LORA, -1)], axis=1).astype(BF16)
        wkv = w_ukv[l].reshape(MLA_KV_LORA, nh, 2 * HEAD_DIM)
        w_kv = jnp.concatenate([wkv[:, :, :HEAD_DIM].reshape(MLA_KV_LORA, -1),
                                wkv[:, :, HEAD_DIM:].reshape(MLA_KV_LORA, -1)], axis=1).astype(BF16)

        h, small = _norm_small(x2, g_norm1[l], scale1, shift1, w_small1, seq)
        prm = (w_qkv, w_lat, w_gates, b_forget[l], g_q_norm[l], w_q, g_kv_norm[l], w_kv,
               w_pool[l], pool_scale[l], w_branch[l].astype(BF16), w_out[l].astype(BF16), gate1)
        x2 = _mixer_layer(x2, h, small, batch, seq, prm)

        w_small2 = _pad_lanes(jnp.concatenate([w_route_group[l], w_route_expert[l]], axis=1),
                              SMALL_WIDTH)
        h, logits = _norm_small(x2, g_norm2[l], scale2, shift2, w_small2, seq)
        x2 = _moe_layer(x2, h, logits, gate2, b_route_group[l], b_route_expert[l],
                        w_gate, w_up, w_down, l, batch, seq)
    return _final_norm(x2, g_final).reshape(batch, seq, d)
```

```python
import functools

import numpy as np
import jax
import jax.numpy as jnp
from jax import lax
from jax.experimental import pallas as pl
from jax.experimental.pallas import tpu as pltpu

F32 = jnp.float32
BF16 = jnp.bfloat16

NORM_EPS = 1e-6
N_HEADS = 4
HEAD_DIM = 128
MLA_Q_LORA = 512
MLA_KV_LORA = 256
MLA_NOPE_DIM = 128
MLA_ROPE_DIM = 64
ROPE_BASE = 10000.0
POOL_WINDOWS = (2, 4, 8, 16)
POOL_HALO = 16
N_BRANCHES = 4
BRANCH_WIDTH = 512
N_EXPERT_GROUPS = 4
EXPERTS_PER_GROUP = 8
N_EXPERTS = N_EXPERT_GROUPS * EXPERTS_PER_GROUP
EXPERT_TOP_K = 2
SMALL_WIDTH = 256

LANE = 128
V7X_VMEM_BYTES = 64 * 1024 * 1024
VMEM_LIMIT = V7X_VMEM_BYTES - 12 * 1024 * 1024


def _cparams(*sem):
    return pltpu.CompilerParams(dimension_semantics=sem, vmem_limit_bytes=VMEM_LIMIT)


def _tile(n, pref):
    t = min(n, pref)
    assert n % t == 0, (n, pref)
    return t


def _mod_body(c_ref, w_ref, b_ref, o_ref):
    c = c_ref[...]
    ca = c * jax.nn.sigmoid(c)
    acc = jnp.dot(ca.astype(BF16), w_ref[0].astype(BF16), preferred_element_type=F32)
    o_ref[0] = acc + b_ref[0]


def _modulation(c_pad, w_mod, b_mod):
    depth, d, n = w_mod.shape
    tn = _tile(n, 1024)
    return pl.pallas_call(
        _mod_body,
        grid=(depth, n // tn),
        in_specs=[
            pl.BlockSpec((8, d), lambda l, j: (0, 0)),
            pl.BlockSpec((1, d, tn), lambda l, j: (l, 0, j)),
            pl.BlockSpec((1, 1, tn), lambda l, j: (l, 0, j)),
        ],
        out_specs=pl.BlockSpec((1, 8, tn), lambda l, j: (l, 0, j)),
        out_shape=jax.ShapeDtypeStruct((depth, 8, n), F32),
        compiler_params=_cparams("parallel", "parallel"),
        name="adaln_modulation",
    )(c_pad, w_mod, b_mod.reshape(depth, 1, n))


REGROUP_CHUNK = 1024


def _regroup_body(w_ref, *out_refs, plans):
    total = w_ref.shape[2]

    def cols(a, b):
        a0 = (a // LANE) * LANE
        b1 = min(-(-b // LANE) * LANE, total)
        return w_ref[0, :, a0:b1][:, a - a0:b - a0]

    for o_ref, plan in zip(out_refs, plans):
        for dst, a, b, scale in plan:
            if a is None:
                o_ref[0, :, dst:b] = jnp.zeros((o_ref.shape[1], b - dst), o_ref.dtype)
                continue
            for c0 in range(0, b - a, REGROUP_CHUNK):
                c1 = min(c0 + REGROUP_CHUNK, b - a)
                v = cols(a + c0, a + c1)
                if scale != 1.0:
                    v = v * scale
                o_ref[0, :, dst + c0:dst + c1] = v.astype(o_ref.dtype)


def _regroup_w_in(w_in, plans, widths):
    depth, d, total = w_in.shape
    tr = _tile(d, 256)
    return pl.pallas_call(
        functools.partial(_regroup_body, plans=plans),
        grid=(depth, d // tr),
        in_specs=[pl.BlockSpec((1, tr, total), lambda l, i: (l, i, 0))],
        out_specs=[pl.BlockSpec((1, tr, w), lambda l, i: (l, i, 0)) for w in widths],
        out_shape=[jax.ShapeDtypeStruct((depth, d, w), BF16) for w in widths],
        compiler_params=_cparams("parallel", "parallel"),
        name="regroup_w_in",
    )(w_in)


def _add_moe(x_ref, y0_ref, y1_ref, gate_ref):
    return x_ref[...] + gate_ref[0] * (y0_ref[...].astype(F32) + y1_ref[...].astype(F32))


def _moe_in_specs(tm, d, per_b):
    row = pl.BlockSpec((tm, d), lambda i: (i, 0))
    return [row, row, pl.BlockSpec((1, 1, d), lambda i: (i // per_b, 0, 0))]


def _norm_small_body(*refs, has_moe):
    if has_moe:
        x_ref, y0_ref, y1_ref, gate_ref, g_ref, sc_ref, sh_ref, w_ref, xo_ref, h_ref, s_ref = refs
        x = _add_moe(x_ref, y0_ref, y1_ref, gate_ref)
        xo_ref[...] = x
    else:
        x_ref, g_ref, sc_ref, sh_ref, w_ref, h_ref, s_ref = refs
        x = x_ref[...]
    ms = jnp.mean(x * x, axis=-1, keepdims=True)
    y = x * lax.rsqrt(ms + NORM_EPS) * g_ref[...]
    h = y * (1.0 + sc_ref[0]) + sh_ref[0]
    h_ref[...] = h.astype(h_ref.dtype)
    s_ref[...] = jnp.dot(h.astype(BF16), w_ref[...], preferred_element_type=F32)


def _norm_small(x2, g, scale, shift, w_small, seq, h_dtype, moe_in=None):
    n, d = x2.shape
    tm = _tile(seq, 512)
    per_b = seq // tm
    ws = w_small.shape[1]
    row = pl.BlockSpec((tm, d), lambda i: (i, 0))
    has_moe = moe_in is not None
    outs = pl.pallas_call(
        functools.partial(_norm_small_body, has_moe=has_moe),
        grid=(n // tm,),
        in_specs=[row] + (_moe_in_specs(tm, d, per_b) if has_moe else []) + [
            pl.BlockSpec((1, d), lambda i: (0, 0)),
            pl.BlockSpec((1, 1, d), lambda i: (i // per_b, 0, 0)),
            pl.BlockSpec((1, 1, d), lambda i: (i // per_b, 0, 0)),
            pl.BlockSpec((d, ws), lambda i: (0, 0)),
        ],
        out_specs=([row] if has_moe else []) + [row, pl.BlockSpec((tm, ws), lambda i: (i, 0))],
        out_shape=([jax.ShapeDtypeStruct((n, d), F32)] if has_moe else []) + [
            jax.ShapeDtypeStruct((n, d), h_dtype),
            jax.ShapeDtypeStruct((n, ws), F32),
        ],
        compiler_params=_cparams("parallel"),
        name="norm_modulate_small_proj",
    )(x2, *(moe_in if has_moe else ()), g.reshape(1, d), scale, shift, w_small)
    return outs if has_moe else (x2, *outs)


def _mm_body(a_ref, w_ref, o_ref, *, act):
    acc = jnp.dot(a_ref[...], w_ref[0], preferred_element_type=F32)
    if act == "sigmoid":
        acc = jax.nn.sigmoid(acc)
    o_ref[...] = acc.astype(o_ref.dtype)


def _matmul(a, w, layer, out_dtype, act=None, tm=1024, tn=1024, name="matmul"):
    m, k = a.shape
    n = w.shape[2]
    tm = _tile(m, tm)
    tn = _tile(n, tn)
    return pl.pallas_call(
        functools.partial(_mm_body, act=act),
        grid=(m // tm, n // tn),
        in_specs=[
            pl.BlockSpec((tm, k), lambda i, j: (i, 0)),
            pl.BlockSpec((1, k, tn), lambda i, j: (layer, 0, j)),
        ],
        out_specs=pl.BlockSpec((tm, tn), lambda i, j: (i, j)),
        out_shape=jax.ShapeDtypeStruct((m, n), out_dtype),
        compiler_params=_cparams("parallel", "parallel"),
        name=name,
    )(a, w)


def _mm_residual_body(a_ref, w_ref, x_ref, gate_ref, o_ref):
    acc = jnp.dot(a_ref[...], w_ref[...], preferred_element_type=F32)
    o_ref[...] = x_ref[...] + gate_ref[0] * acc


def _matmul_residual(a, w, x2, gate, seq, tm=1024, tn=1024):
    m, k = a.shape
    n = w.shape[1]
    tm = _tile(seq, tm)
    tn = _tile(n, tn)
    per_b = seq // tm
    return pl.pallas_call(
        _mm_residual_body,
        grid=(m // tm, n // tn),
        in_specs=[
            pl.BlockSpec((tm, k), lambda i, j: (i, 0)),
            pl.BlockSpec((k, tn), lambda i, j: (0, j)),
            pl.BlockSpec((tm, tn), lambda i, j: (i, j)),
            pl.BlockSpec((1, 1, tn), lambda i, j: (i // per_b, 0, j)),
        ],
        out_specs=pl.BlockSpec((tm, tn), lambda i, j: (i, j)),
        out_shape=jax.ShapeDtypeStruct((m, n), F32),
        compiler_params=_cparams("parallel", "parallel"),
        name="out_proj_residual",
    )(a, w, x2, gate)


def _latent_norm(a_ref, g_ref):
    a = a_ref[...]
    ms = jnp.mean(a * a, axis=-1, keepdims=True)
    return (a * lax.rsqrt(ms + NORM_EPS) * g_ref[...]).astype(BF16)


def _mla_q_body(a_ref, g_ref, w_ref, cos_ref, sin_ref, nope_ref, rope_ref):
    acc = jnp.dot(_latent_norm(a_ref, g_ref), w_ref[...], preferred_element_type=F32)
    w = nope_ref.shape[1]
    nope_ref[...] = acc[:, :w].astype(BF16)
    rope_ref[...] = (acc[:, w:2 * w] * cos_ref[...] + acc[:, 2 * w:] * sin_ref[...]).astype(BF16)


def _mla_q(latent, col_block, g, w_q, cos_t, sin_t, seq):
    n = latent.shape[0]
    k = w_q.shape[0]
    w = w_q.shape[1] // 3
    tm = _tile(seq, 1024)
    per_b = seq // tm
    return pl.pallas_call(
        _mla_q_body,
        grid=(n // tm,),
        in_specs=[
            pl.BlockSpec((tm, k), lambda i: (i, col_block)),
            pl.BlockSpec((1, k), lambda i: (0, 0)),
            pl.BlockSpec((k, 3 * w), lambda i: (0, 0)),
            pl.BlockSpec((tm, w), lambda i: (i % per_b, 0)),
            pl.BlockSpec((tm, w), lambda i: (i % per_b, 0)),
        ],
        out_specs=[
            pl.BlockSpec((tm, w), lambda i: (i, 0)),
            pl.BlockSpec((tm, w), lambda i: (i, 0)),
        ],
        out_shape=[
            jax.ShapeDtypeStruct((n, w), BF16),
            jax.ShapeDtypeStruct((n, w), BF16),
        ],
        compiler_params=_cparams("parallel"),
        name="mla_q_up_rope",
    )(latent, g.reshape(1, k), w_q, cos_t, sin_t)


def _mla_kv_body(a_ref, g_ref, w_ref, o_ref):
    o_ref[...] = jnp.dot(_latent_norm(a_ref, g_ref), w_ref[...],
                         preferred_element_type=F32).astype(BF16)


def _mla_kv(latent, col_block, g, w_kv, seq):
    n = latent.shape[0]
    k, nout = w_kv.shape
    tm = _tile(seq, 1024)
    return pl.pallas_call(
        _mla_kv_body,
        grid=(n // tm,),
        in_specs=[
            pl.BlockSpec((tm, k), lambda i: (i, col_block)),
            pl.BlockSpec((1, k), lambda i: (0, 0)),
            pl.BlockSpec((k, nout), lambda i: (0, 0)),
        ],
        out_specs=pl.BlockSpec((tm, nout), lambda i: (i, 0)),
        out_shape=jax.ShapeDtypeStruct((n, nout), BF16),
        compiler_params=_cparams("parallel"),
        name="mla_kv_up",
    )(latent, g.reshape(1, k), w_kv)


def _softmax_attn_body(*refs, tq, tk, has_extra, has_bias):
    refs = list(refs)
    qm_ref = refs.pop(0)
    qe_ref = refs.pop(0) if has_extra else None
    km_ref = refs.pop(0)
    ke_ref = refs.pop(0) if has_extra else None
    v_ref = refs.pop(0)
    kb_ref = refs.pop(0) if has_bias else None
    o_ref, m_ref, acc_ref = refs
    qi = pl.program_id(1)
    m_ref[...] = jnp.full(m_ref.shape, -jnp.inf, F32)
    acc_ref[...] = jnp.zeros(acc_ref.shape, F32)
    chunks_per_q = tq // tk
    n_full = qi * chunks_per_q
    ones = jnp.ones((tk, HEAD_DIM), BF16)

    def head_chunk(h, k0, diag):
        lanes = slice(h * HEAD_DIM, (h + 1) * HEAD_DIM)
        q = qm_ref[0, :, lanes]
        k = km_ref[0, pl.ds(k0, tk), lanes]
        if has_extra:
            q = jnp.concatenate([q, qe_ref[0, :, lanes]], axis=-1)
            k = jnp.concatenate([k, ke_ref[0, pl.ds(k0, tk), :]], axis=-1)
        s = lax.dot_general(q, k, (((1,), (1,)), ((), ())), preferred_element_type=F32)
        if has_bias:
            s = s + kb_ref[0, h, :, pl.ds(k0, tk)]
        if diag is not None:
            row = lax.broadcasted_iota(jnp.int32, (tq, tk), 0)
            col = lax.broadcasted_iota(jnp.int32, (tq, tk), 1) + diag * tk
            s = jnp.where(row >= col, s, -jnp.inf)
        blocks = [s[:, j * LANE:(j + 1) * LANE] for j in range(tk // LANE)]
        mx = blocks[0]
        for blk in blocks[1:]:
            mx = jnp.maximum(mx, blk)
        m_prev = m_ref[h]
        m_new = jnp.maximum(m_prev, jnp.max(mx, axis=-1, keepdims=True))
        alpha = jnp.exp(m_prev - m_new)
        p = jnp.concatenate([jnp.exp(blk - m_new) for blk in blocks], axis=-1).astype(BF16)
        v_ext = jnp.concatenate([v_ref[0, pl.ds(k0, tk), lanes], ones], axis=-1)
        pv = jnp.dot(p, v_ext, preferred_element_type=F32)
        acc_ref[h, :, :HEAD_DIM] = alpha * acc_ref[h, :, :HEAD_DIM] + pv[:, :HEAD_DIM]
        acc_ref[h, :, HEAD_DIM:] = alpha * acc_ref[h, :, HEAD_DIM:] + pv[:, HEAD_DIM:]
        m_ref[h] = m_new

    def chunk(c, diag):
        k0 = pl.multiple_of(c * tk, tk)
        for h in range(N_HEADS):
            head_chunk(h, k0, diag)

    def full_chunk(c, carry):
        chunk(c, None)
        return carry

    lax.fori_loop(0, n_full, full_chunk, 0)
    for d in range(chunks_per_q):
        chunk(n_full + d, d)
    for h in range(N_HEADS):
        o_ref[0, :, h * HEAD_DIM:(h + 1) * HEAD_DIM] = (
            acc_ref[h, :, :HEAD_DIM] / acc_ref[h, :, HEAD_DIM:]).astype(o_ref.dtype)


def _softmax_attention(qm, q_blk, km, k_blk, v, v_blk, batch, seq, extra=None, key_bias=None):
    tq = _tile(seq, 512)
    tk = _tile(seq, 512)
    width = N_HEADS * HEAD_DIM
    args = [qm]
    specs = [pl.BlockSpec((1, tq, width), lambda b, i: (b, i, q_blk))]
    if extra is not None:
        args.append(extra[0])
        specs.append(pl.BlockSpec((1, tq, width), lambda b, i: (b, i, 0)))
    args.append(km)
    specs.append(pl.BlockSpec((1, seq, width), lambda b, i: (b, 0, k_blk)))
    if extra is not None:
        args.append(extra[1])
        specs.append(pl.BlockSpec((1, seq, HEAD_DIM), lambda b, i: (b, 0, 0)))
    args.append(v)
    specs.append(pl.BlockSpec((1, seq, width), lambda b, i: (b, 0, v_blk)))
    if key_bias is not None:
        args.append(key_bias)
        specs.append(pl.BlockSpec((1, N_HEADS, 1, seq), lambda b, i: (b, 0, 0, 0)))

    return pl.pallas_call(
        functools.partial(_softmax_attn_body, tq=tq, tk=tk, has_extra=extra is not None,
                          has_bias=key_bias is not None),
        grid=(batch, seq // tq),
        in_specs=specs,
        out_specs=pl.BlockSpec((1, tq, width), lambda b, i: (b, i, 0)),
        out_shape=jax.ShapeDtypeStruct((batch, seq, width), BF16),
        scratch_shapes=[
            pltpu.VMEM((N_HEADS, tq, HEAD_DIM), F32),
            pltpu.VMEM((N_HEADS, tq, 2 * HEAD_DIM), F32),
        ],
        compiler_params=_cparams("parallel", "arbitrary"),
        name="causal_softmax_attention",
    )(*args)


def _sb_attn_body(q_ref, k_ref, v_ref, tri_ref, o_ref, r_ref, acc_ref, *, tq, tk):
    qi = pl.program_id(1)
    r_ref[...] = jnp.zeros(r_ref.shape, F32)
    acc_ref[...] = jnp.zeros(acc_ref.shape, F32)
    chunks_per_q = tq // tk
    n_full = qi * chunks_per_q

    def head_chunk(h, k0, diag):
        lanes = slice(h * HEAD_DIM, (h + 1) * HEAD_DIM)
        z = lax.dot_general(q_ref[0, :, lanes], k_ref[0, pl.ds(k0, tk), lanes],
                            (((1,), (1,)), ((), ())), preferred_element_type=F32)
        sp = jnp.maximum(z, 0.0) + jnp.log(1.0 + jnp.exp(-jnp.abs(z)))
        log_beta = z - sp
        if diag is not None:
            row = lax.broadcasted_iota(jnp.int32, (tq, tk), 0)
            col = lax.broadcasted_iota(jnp.int32, (tq, tk), 1) + diag * tk
            strict = row > col
            sp = jnp.where(strict, sp, 0.0)
        hi = sp.astype(BF16)
        lo = (sp - hi.astype(F32)).astype(BF16)
        tri = tri_ref[...]
        e =jnp.dot(hi, tri, preferred_element_type=F32) + jnp.dot(lo, tri, preferred_element_type=F32)
        r = r_ref[h]
        a = jnp.concatenate(
            [jnp.exp(log_beta[:, j * LANE:(j + 1) * LANE] - e[:, j * LANE:(j + 1) * LANE] - r)
             for j in range(tk // LANE)], axis=-1)
        if diag is not None:
            a = jnp.where(strict, a, 0.0)
        acc_ref[h] += jnp.dot(a.astype(BF16), v_ref[0, pl.ds(k0, tk), lanes],
                              preferred_element_type=F32)
        r_ref[h] = r + e[:, tk:]

    def chunk(c, diag):
        k0 = pl.multiple_of(c * tk, tk)
        for h in range(N_HEADS):
            head_chunk(h, k0, diag)

    for d in reversed(range(chunks_per_q)):
        chunk(n_full + d, d)

    def full_chunk(i, carry):
        chunk(n_full - 1 - i, None)
        return carry

    lax.fori_loop(0, n_full, full_chunk, 0)
    for h in range(N_HEADS):
        o_ref[0, :, h * HEAD_DIM:(h + 1) * HEAD_DIM] = acc_ref[h].astype(o_ref.dtype)


def _sb_tri(tk):
    j_src = np.arange(tk)[:, None]
    j_dst = np.arange(tk + LANE)[None, :]
    return jnp.asarray((j_src > j_dst) | (j_dst >= tk), dtype=BF16)


def _sb_attention(qkv, q_blk, k_blk, v_blk, batch, seq):
    tq = _tile(seq, 512)
    tk = _tile(seq, 256)
    width = N_HEADS * HEAD_DIM
    return pl.pallas_call(
        functools.partial(_sb_attn_body, tq=tq, tk=tk),
        grid=(batch, seq // tq),
        in_specs=[
            pl.BlockSpec((1, tq, width), lambda b, i: (b, i, q_blk)),
            pl.BlockSpec((1, seq, width), lambda b, i: (b, 0, k_blk)),
            pl.BlockSpec((1, seq, width), lambda b, i: (b, 0, v_blk)),
            pl.BlockSpec((tk, tk + LANE), lambda b, i: (0, 0)),
        ],
        out_specs=pl.BlockSpec((1, tq, width), lambda b, i: (b, i, 0)),
        out_shape=jax.ShapeDtypeStruct((batch, seq, width), BF16),
        scratch_shapes=[
            pltpu.VMEM((N_HEADS, tq, HEAD_DIM), F32),
            pltpu.VMEM((N_HEADS, tq, HEAD_DIM), F32),
        ],
        compiler_params=_cparams("parallel", "arbitrary"),
        name="stick_breaking_attention",
    )(qkv, qkv, qkv, _sb_tri(tk))


def _pool_body(u_ref, halo_ref, w_ref, sc_ref, o_ref, ext_ref, *, ts):
    i = pl.program_id(1)
    halo = jnp.where(i > 0, halo_ref[0], 0.0)
    ext_ref[:POOL_HALO, :] = halo
    ext_ref[POOL_HALO:, :] = u_ref[0]
    pos = i * ts + lax.broadcasted_iota(jnp.int32, (ts, 1), 0)
    outs = []
    for g, win in enumerate(POOL_WINDOWS):
        lanes = slice(g * LANE, (g + 1) * LANE)
        tok = ext_ref[POOL_HALO:, lanes]
        tot = tok
        for back in range(1, win):
            tot = tot + ext_ref[POOL_HALO - back:POOL_HALO - back + ts, lanes]
        cnt = jnp.minimum(pos + 1, win).astype(F32)
        pooled = (tot / cnt - tok).astype(BF16)
        outs.append(jnp.dot(pooled, w_ref[g], preferred_element_type=F32))
    y = jnp.concatenate(outs, axis=-1) * sc_ref[...]
    o_ref[0] = y.astype(o_ref.dtype)


def _pool_mixer(latent3, col_block, w_pool, pool_scale, batch, seq):
    width = len(POOL_WINDOWS) * LANE
    ts = _tile(seq, 512)
    halo_per_tile = ts // POOL_HALO
    return pl.pallas_call(
        functools.partial(_pool_body, ts=ts),
        grid=(batch, seq // ts),
        in_specs=[
            pl.BlockSpec((1, ts, width), lambda b, i: (b, i, col_block)),
            pl.BlockSpec((1, POOL_HALO, width),
                         lambda b, i: (b, jnp.maximum(i * halo_per_tile - 1, 0), col_block)),
            pl.BlockSpec((len(POOL_WINDOWS), LANE, LANE), lambda b, i: (0, 0, 0)),
            pl.BlockSpec((1, width), lambda b, i: (0, 0)),
        ],
        out_specs=pl.BlockSpec((1, ts, width), lambda b, i: (b, i, 0)),
        out_shape=jax.ShapeDtypeStruct((batch, seq, width), BF16),
        scratch_shapes=[pltpu.VMEM((ts + POOL_HALO, width), F32)],
        compiler_params=_cparams("parallel", "arbitrary"),
        name="multiscale_pool",
    )(latent3, latent3, w_pool.astype(BF16), pool_scale.reshape(1, width))


def _merge_body(y0, y1, y2, y3, g0, g1, g2, g3, wb_ref, o_ref):
    acc = None
    for n, (y, g) in enumerate(((y0, g0), (y1, g1), (y2, g2), (y3, g3))):
        t = g[...].astype(F32) * jnp.dot(y[...], wb_ref[n], preferred_element_type=F32)
        acc = t if acc is None else acc + t
    o_ref[...] = acc.astype(o_ref.dtype)


def _merge(ys, gates, w_branch):
    n = ys[0].shape[0]
    d = w_branch.shape[2]
    tm = _tile(n, 1024)
    tn = _tile(d, 512)
    nj = d // tn
    y_spec = pl.BlockSpec((tm, BRANCH_WIDTH), lambda i, j: (i, 0))

    def g_spec(b):
        return pl.BlockSpec((tm, tn), lambda i, j: (i, b * nj + j))

    return pl.pallas_call(
        _merge_body,
        grid=(n // tm, nj),
        in_specs=[y_spec] * N_BRANCHES + [g_spec(b) for b in range(N_BRANCHES)] + [
            pl.BlockSpec((N_BRANCHES, BRANCH_WIDTH, tn), lambda i, j: (0, 0, j))],
        out_specs=pl.BlockSpec((tm, tn), lambda i, j: (i, j)),
        out_shape=jax.ShapeDtypeStruct((n, d), BF16),
        compiler_params=_cparams("parallel", "parallel"),
        name="gated_branch_merge",
    )(*ys, gates, gates, gates, gates, w_branch)


def _expert_changed(be_ref):
    blk = pl.program_id(1)
    return (blk == 0) | (be_ref[blk] != be_ref[jnp.maximum(blk - 1, 0)])


def _moe_up_body(be_ref, nb_ref, x_ref, wg_ref, wu_ref, o_ref, wg_s, wu_s):
    blk = pl.program_id(1)

    @pl.when(_expert_changed(be_ref))
    def _():
        wg_s[...] = wg_ref[0, 0].astype(BF16)
        wu_s[...] = wu_ref[0, 0].astype(BF16)

    @pl.when(blk < nb_ref[0])
    def _():
        x = x_ref[...].astype(BF16)
        g = jnp.dot(x, wg_s[...], preferred_element_type=F32)
        u = jnp.dot(x, wu_s[...], preferred_element_type=F32)
        o_ref[...] = (g * jax.nn.sigmoid(g) * u).astype(o_ref.dtype)

    @pl.when(blk >= nb_ref[0])
    def _():
        o_ref[...] = jnp.zeros(o_ref.shape, o_ref.dtype)


def _moe_down_body(be_ref, nb_ref, a_ref, wd_ref, rw_ref, o_ref, wd_s):
    blk = pl.program_id(1)

    @pl.when(_expert_changed(be_ref))
    def _():
        wd_s[...] = wd_ref[0, 0].astype(BF16)

    @pl.when(blk < nb_ref[0])
    def _():
        y = jnp.dot(a_ref[...], wd_s[...], preferred_element_type=F32)
        o_ref[...] = (y * rw_ref[...]).astype(o_ref.dtype)

    @pl.when(blk >= nb_ref[0])
    def _():
        o_ref[...] = jnp.zeros(o_ref.shape, o_ref.dtype)


def _moe_ffn(x_buf, row_w, block_expert, n_used, w_gate, w_up, w_down, layer, tb):
    n_rows, d = x_buf.shape
    f = w_gate.shape[3]
    tf = _tile(f, 512)
    tn = _tile(d, 1024)
    n_blocks = n_rows // tb
    act = pl.pallas_call(
        _moe_up_body,
        grid_spec=pltpu.PrefetchScalarGridSpec(
            num_scalar_prefetch=2,
            grid=(f // tf, n_blocks),
            in_specs=[
                pl.BlockSpec((tb, d), lambda j, i, be, nb: (i, 0)),
                pl.BlockSpec((1, 1, d, tf), lambda j, i, be, nb: (layer, be[i], 0, j)),
                pl.BlockSpec((1, 1, d, tf), lambda j, i, be, nb: (layer, be[i], 0, j)),
            ],
            out_specs=pl.BlockSpec((tb, tf), lambda j, i, be, nb: (i, j)),
            scratch_shapes=[pltpu.VMEM((d, tf), BF16), pltpu.VMEM((d, tf), BF16)],
        ),
        out_shape=jax.ShapeDtypeStruct((n_rows, f), BF16),
        compiler_params=_cparams("arbitrary", "arbitrary"),
        name="expert_gate_up_swiglu",
    )(block_expert, n_used, x_buf, w_gate, w_up)
    return pl.pallas_call(
        _moe_down_body,
        grid_spec=pltpu.PrefetchScalarGridSpec(
            num_scalar_prefetch=2,
            grid=(d // tn, n_blocks),
            in_specs=[
                pl.BlockSpec((tb, f), lambda j, i, be, nb: (i, 0)),
                pl.BlockSpec((1, 1, f, tn), lambda j, i, be, nb: (layer, be[i], 0, j)),
                pl.BlockSpec((tb, 1), lambda j, i, be, nb: (i, 0)),
            ],
            out_specs=pl.BlockSpec((tb, tn), lambda j, i, be, nb: (i, j)),
            scratch_shapes=[pltpu.VMEM((f, tn), BF16)],
        ),
        out_shape=jax.ShapeDtypeStruct((n_rows, d), BF16),
        compiler_params=_cparams("arbitrary", "arbitrary"),
        name="expert_down",
    )(block_expert, n_used, act, w_down, row_w)


def _final_norm_body(x_ref, y0_ref, y1_ref, gate_ref, g_ref, o_ref):
    x = _add_moe(x_ref, y0_ref, y1_ref, gate_ref)
    ms = jnp.mean(x * x, axis=-1, keepdims=True)
    o_ref[...] = x * lax.rsqrt(ms + NORM_EPS) * g_ref[...]


def _final_norm(x2, moe_in, g, seq):
    n, d = x2.shape
    tm = _tile(seq, 512)
    row = pl.BlockSpec((tm, d), lambda i: (i, 0))
    return pl.pallas_call(
        _final_norm_body,
        grid=(n // tm,),
        in_specs=[row] + _moe_in_specs(tm, d, seq // tm) + [pl.BlockSpec((1, d), lambda i: (0, 0))],
        out_specs=row,
        out_shape=jax.ShapeDtypeStruct((n, d), F32),
        compiler_params=_cparams("parallel"),
        name="final_rmsnorm",
    )(x2, *moe_in, g.reshape(1, d))


def _rope_tables(seq):
    half = MLA_ROPE_DIM // 2
    inv = jnp.power(ROPE_BASE, -2.0 * jnp.arange(half, dtype=F32) / MLA_ROPE_DIM)
    ang = jnp.arange(seq).astype(F32)[:, None] * inv[None, :]
    cos, sin = jnp.cos(ang), jnp.sin(ang)
    c64 = jnp.concatenate([cos, cos], axis=-1)
    s64 = jnp.concatenate([-sin, sin], axis=-1)
    return c64, s64


def _swap_halves(w):
    half = w.shape[-1] // 2
    return jnp.concatenate([w[..., half:], w[..., :half]], axis=-1)


def _pad_lanes(a, width):
    return jnp.pad(a, [(0, 0)] * (a.ndim - 1) + [(0, width - a.shape[-1])])


def _mixer_layer(x2, h, small, layer, batch, seq, prm):
    n, d = x2.shape
    (w_qkv, w_lat, w_gates, b_forget, g_q, w_q, g_kv, w_kv, w_pool, pool_scale, w_branch, w_out,
     gate1) = prm
    c64, s64 = _rope_tables(seq)

    qkv = _matmul(h, w_qkv, layer, BF16, name="fox_sb_qkv_proj").reshape(batch, seq, -1)
    latent = _matmul(h, w_lat, layer, F32, tn=640, name="latent_pool_proj")
    gates = _matmul(h, w_gates, layer, BF16, act="sigmoid", name="branch_gate_proj")

    fox_f = small[:, 2 * MLA_ROPE_DIM:2 * MLA_ROPE_DIM + N_HEADS].reshape(batch, seq, N_HEADS)
    log_f_cum = jnp.cumsum(jax.nn.log_sigmoid(fox_f + b_forget), axis=1)
    k_bias = -log_f_cum.transpose(0, 2, 1)[:, :, None, :]
    nh = N_HEADS
    y_fox = _softmax_attention(qkv, 0, qkv, 1, qkv, 2, batch, seq, key_bias=k_bias)

    y_sb = _sb_attention(qkv, 3, 4, 5, batch, seq)

    cos_t = jnp.tile(_pad_lanes(c64, HEAD_DIM), (1, nh))
    sin_t = jnp.tile(_pad_lanes(s64, HEAD_DIM), (1, nh))
    q_nope, q_rope = _mla_q(latent, 0, g_q, w_q, cos_t, sin_t, seq)
    kv = _mla_kv(latent, 4, g_kv, w_kv, seq).reshape(batch, seq, -1)
    kr = small[:, :MLA_ROPE_DIM] * jnp.tile(c64, (batch, 1)) \
        + small[:, MLA_ROPE_DIM:2 * MLA_ROPE_DIM] * jnp.tile(s64, (batch, 1))
    kr = _pad_lanes(kr.astype(BF16), HEAD_DIM).reshape(batch, seq, HEAD_DIM)
    y_mla = _softmax_attention(q_nope.reshape(batch, seq, -1), 0, kv, 0, kv, 1, batch, seq,
                               extra=(q_rope.reshape(batch, seq, -1), kr))

    y_pool = _pool_mixer(latent.reshape(batch, seq, -1), 1, w_pool, pool_scale, batch, seq)

    ys = [y.reshape(n, BRANCH_WIDTH) for y in (y_fox, y_sb, y_mla, y_pool)]
    merged = _merge(ys, gates, w_branch)
    return _matmul_residual(merged, w_out, x2, gate1, seq)


def _moe_layer(h, logits, gate2, b_rg, b_re, w_gate, w_up, w_down, layer):
    n, d = h.shape
    tb = 512
    g_logits = logits[:, :N_EXPERT_GROUPS] + b_rg
    g_idx = jnp.argmax(g_logits, axis=-1).astype(jnp.int32)[:, None]
    g_top = jnp.max(g_logits, axis=-1)
    p_group = jnp.exp(g_top - jax.nn.logsumexp(g_logits, axis=-1))
    e_logits = (logits[:, N_EXPERT_GROUPS:N_EXPERT_GROUPS + N_EXPERTS] + b_re).reshape(
        n, N_EXPERT_GROUPS, EXPERTS_PER_GROUP)
    e_logits = jnp.take_along_axis(e_logits, g_idx[:, :, None], axis=1)[:, 0]
    probs = jax.nn.softmax(e_logits, axis=-1)
    lane = jnp.arange(EXPERTS_PER_GROUP, dtype=jnp.int32)[None, :]
    i1 = jnp.argmax(probs, axis=-1).astype(jnp.int32)[:, None]
    p1 = jnp.max(probs, axis=-1, keepdims=True)
    rest = jnp.where(lane == i1, -jnp.inf, probs)
    i2 = jnp.argmax(rest, axis=-1).astype(jnp.int32)[:, None]
    p2 = jnp.max(rest, axis=-1, keepdims=True)
    top_p = jnp.concatenate([p1, p2], axis=-1)
    top_i = jnp.concatenate([i1, i2], axis=-1)
    weights = p_group[:, None] * top_p / jnp.sum(top_p, axis=-1, keepdims=True)

    expert_id = (g_idx * EXPERTS_PER_GROUP + top_i).reshape(-1).astype(jnp.int32)
    m = expert_id.shape[0]
    w_flat = weights.reshape(-1)

    order = jnp.argsort(expert_id).astype(jnp.int32)
    rank = jnp.argsort(order).astype(jnp.int32)
    onehot = expert_id[:, None] == jnp.arange(N_EXPERTS, dtype=jnp.int32)[None, :]
    counts = jnp.sum(onehot, axis=0, dtype=jnp.int32)
    padded = ((counts + tb - 1) // tb) * tb
    start = jnp.cumsum(counts) - counts
    pend = jnp.cumsum(padded)
    shift = pend - padded - start
    n_rows = m + N_EXPERTS * tb
    n_blocks = n_rows // tb
    blk = jnp.arange(n_blocks, dtype=jnp.int32)
    n_used = (pend[-1] // tb).astype(jnp.int32)
    block_expert = jnp.minimum(
        jnp.sum(pend[None, :] <= (blk * tb)[:, None], axis=1), N_EXPERTS - 1).astype(jnp.int32)
    last_used = block_expert[jnp.maximum(n_used - 1, 0)]
    block_expert = jnp.where(blk < n_used, block_expert, last_used)

    rows = jnp.arange(n_rows, dtype=jnp.int32).reshape(n_blocks, tb)
    src_sorted = rows - shift[block_expert][:, None]
    valid = (src_sorted < (start + counts)[block_expert][:, None]) & (blk < n_used)[:, None]
    src_assign = order.at[jnp.clip(src_sorted, 0, m - 1).reshape(-1)].get(mode="promise_in_bounds")
    valid = valid.reshape(-1)
    row_tok = jnp.where(valid, src_assign // EXPERT_TOP_K, 0)
    row_w = jnp.where(valid, w_flat.at[src_assign].get(mode="promise_in_bounds"), 0.0).astype(F32)
    dest = (rank + jnp.sum(jnp.where(onehot, shift[None, :], 0), axis=1)).reshape(n, EXPERT_TOP_K)

    x_buf = h.at[row_tok].get(mode="promise_in_bounds")
    y = _moe_ffn(x_buf, row_w.reshape(n_rows, 1), block_expert, n_used.reshape(1),
                 w_gate, w_up, w_down, layer, tb)
    return (y.at[dest[:, 0]].get(mode="promise_in_bounds"),
            y.at[dest[:, 1]].get(mode="promise_in_bounds"), gate2)


def kernel(x, c, w_mod, b_mod, g_norm1, g_norm2, w_in, b_forget, g_q_norm, w_uq, g_kv_norm, w_ukv, w_pool, pool_scale, w_branch, w_out, w_route_group, b_route_group, w_route_expert, b_route_expert, w_gate, w_up, w_down, g_final):
    batch, seq, d = x.shape
    depth = w_mod.shape[0]
    n = batch * seq
    nh = N_HEADS

    c_pad = jnp.pad(c, ((0, 8 - batch), (0, 0)))
    mod = _modulation(c_pad, w_mod, b_mod)[:, :batch]

    hq = nh * HEAD_DIM
    o_fox, o_f = 0, 3 * hq
    o_sb = o_f + nh
    o_cq = o_sb + 3 * hq
    o_ckv = o_cq + MLA_Q_LORA
    o_kr = o_ckv + MLA_KV_LORA
    o_pool = o_kr + MLA_ROPE_DIM
    o_gate = o_pool + len(POOL_WINDOWS) * LANE
    attn_scale = HEAD_DIM ** -0.5
    mla_scale = (MLA_NOPE_DIM + MLA_ROPE_DIM) ** -0.5

    half = MLA_ROPE_DIM // 2
    plans = (
        [(0, o_fox, o_fox + hq, attn_scale), (hq, o_fox + hq, o_fox + 3 * hq, 1.0),
         (3 * hq, o_sb, o_sb + hq, attn_scale), (4 * hq, o_sb + hq, o_sb + 3 * hq, 1.0)],
        [(0, o_cq, o_ckv, 1.0), (MLA_Q_LORA, o_pool, o_gate, 1.0),
         (MLA_Q_LORA + o_gate - o_pool, o_ckv, o_kr, 1.0)],
        [(0, o_gate, w_in.shape[2], 1.0)],
        [(0, None, SMALL_WIDTH, 0.0), (0, o_kr, o_pool, 1.0), (MLA_ROPE_DIM, o_kr + half, o_pool, 1.0),
         (MLA_ROPE_DIM + half, o_kr, o_kr + half, 1.0), (2 * MLA_ROPE_DIM, o_f, o_sb, 1.0)],
    )
    w_qkv, w_lat, w_gates, w_small1 = _regroup_w_in(
        w_in, plans, (6 * hq, o_kr - o_cq + o_gate - o_pool, w_in.shape[2] - o_gate, SMALL_WIDTH))

    x2 = x.reshape(n, d)
    pending = None
    for l in range(depth):
        m6 = mod[l].reshape(batch, 6, 1, d)
        shift1, scale1, gate1, shift2, scale2, gate2 = [m6[:, t] for t in range(6)]

        wq = w_uq[l].reshape(MLA_Q_LORA, nh, MLA_NOPE_DIM + MLA_ROPE_DIM) * mla_scale
        wq_rope = wq[:, :, MLA_NOPE_DIM:]
        w_q = jnp.concatenate([
            wq[:, :, :MLA_NOPE_DIM].reshape(MLA_Q_LORA, -1),
            _pad_lanes(wq_rope, HEAD_DIM).reshape(MLA_Q_LORA, -1),
            _pad_lanes(_swap_halves(wq_rope), HEAD_DIM).reshape(MLA_Q_LORA, -1)], axis=1).astype(BF16)
        wkv = w_ukv[l].reshape(MLA_KV_LORA, nh, 2 * HEAD_DIM)
        w_kv = jnp.concatenate([wkv[:, :, :HEAD_DIM].reshape(MLA_KV_LORA, -1),
                                wkv[:, :, HEAD_DIM:].reshape(MLA_KV_LORA, -1)], axis=1).astype(BF16)

        x2, h, small = _norm_small(x2, g_norm1[l], scale1, shift1, w_small1[l], seq, BF16,
                                   moe_in=pending)
        prm = (w_qkv, w_lat, w_gates, b_forget[l], g_q_norm[l], w_q, g_kv_norm[l], w_kv,
               w_pool[l], pool_scale[l], w_branch[l].astype(BF16), w_out[l].astype(BF16), gate1)
        x2 = _mixer_layer(x2, h, small, l, batch, seq, prm)

        w_small2 = _pad_lanes(jnp.concatenate([w_route_group[l], w_route_expert[l]], axis=1),
                              SMALL_WIDTH).astype(BF16)
        _, h, logits = _norm_small(x2, g_norm2[l], scale2, shift2, w_small2, seq, F32)
        pending = _moe_layer(h, logits, gate2, b_route_group[l], b_route_expert[l],
                             w_gate, w_up, w_down, l)
    return _final_norm(x2, pending, g_final, seq).reshape(batch, seq, d)
```

```python
import functools

import numpy as np
import jax
import jax.numpy as jnp
from jax import lax
from jax.experimental import pallas as pl
from jax.experimental.pallas import tpu as pltpu

F32 = jnp.float32
BF16 = jnp.bfloat16

NORM_EPS = 1e-6
N_HEADS = 4
HEAD_DIM = 128
MLA_Q_LORA = 512
MLA_KV_LORA = 256
MLA_NOPE_DIM = 128
MLA_ROPE_DIM = 64
ROPE_BASE = 10000.0
POOL_WINDOWS = (2, 4, 8, 16)
POOL_HALO = 16
N_BRANCHES = 4
BRANCH_WIDTH = 512
N_EXPERT_GROUPS = 4
EXPERTS_PER_GROUP = 8
N_EXPERTS = N_EXPERT_GROUPS * EXPERTS_PER_GROUP
EXPERT_TOP_K = 2
SMALL_WIDTH = 256

LANE = 128
V7X_VMEM_BYTES = 64 * 1024 * 1024
VMEM_LIMIT = V7X_VMEM_BYTES - 12 * 1024 * 1024


def _cparams(*sem):
    return pltpu.CompilerParams(dimension_semantics=sem, vmem_limit_bytes=VMEM_LIMIT)


def _tile(n, pref):
    t = min(n, pref)
    assert n % t == 0, (n, pref)
    return t


def _mod_body(c_ref, w_ref, b_ref, o_ref):
    c = c_ref[...]
    ca = c * jax.nn.sigmoid(c)
    acc = jnp.dot(ca.astype(BF16), w_ref[0].astype(BF16), preferred_element_type=F32)
    o_ref[0] = acc + b_ref[0]


def _modulation(c_pad, w_mod, b_mod):
    depth, d, n = w_mod.shape
    tn = _tile(n, 1024)
    return pl.pallas_call(
        _mod_body,
        grid=(depth, n // tn),
        in_specs=[
            pl.BlockSpec((8, d), lambda l, j: (0, 0)),
            pl.BlockSpec((1, d, tn), lambda l, j: (l, 0, j)),
            pl.BlockSpec((1, 1, tn), lambda l, j: (l, 0, j)),
        ],
        out_specs=pl.BlockSpec((1, 8, tn), lambda l, j: (l, 0, j)),
        out_shape=jax.ShapeDtypeStruct((depth, 8, n), F32),
        compiler_params=_cparams("parallel", "parallel"),
        name="adaln_modulation",
    )(c_pad, w_mod, b_mod.reshape(depth, 1, n))


REGROUP_CHUNK = 1024


def _regroup_body(w_ref, *out_refs, plans):
    total = w_ref.shape[2]

    def cols(a, b):
        a0 = (a // LANE) * LANE
        b1 = min(-(-b // LANE) * LANE, total)
        return w_ref[0, :, a0:b1][:, a - a0:b - a0]

    for o_ref, plan in zip(out_refs, plans):
        for dst, a, b, scale in plan:
            if a is None:
                o_ref[0, :, dst:b] = jnp.zeros((o_ref.shape[1], b - dst), o_ref.dtype)
                continue
            for c0 in range(0, b - a, REGROUP_CHUNK):
                c1 = min(c0 + REGROUP_CHUNK, b - a)
                v = cols(a + c0, a + c1)
                if scale != 1.0:
                    v = v * scale
                o_ref[0, :, dst + c0:dst + c1] = v.astype(o_ref.dtype)


def _regroup_w_in(w_in, plans, widths):
    depth, d, total = w_in.shape
    tr = _tile(d, 256)
    return pl.pallas_call(
        functools.partial(_regroup_body, plans=plans),
        grid=(depth, d // tr),
        in_specs=[pl.BlockSpec((1, tr, total), lambda l, i: (l, i, 0))],
        out_specs=[pl.BlockSpec((1, tr, w), lambda l, i: (l, i, 0)) for w in widths],
        out_shape=[jax.ShapeDtypeStruct((depth, d, w), BF16) for w in widths],
        compiler_params=_cparams("parallel", "parallel"),
        name="regroup_w_in",
    )(w_in)


def _add_moe(x_ref, y0_ref, y1_ref, gate_ref):
    return x_ref[...] + gate_ref[0] * (y0_ref[...].astype(F32) + y1_ref[...].astype(F32))


def _moe_in_specs(tm, d, per_b):
    row = pl.BlockSpec((tm, d), lambda i: (i, 0))
    return [row, row, pl.BlockSpec((1, 1, d), lambda i: (i // per_b, 0, 0))]


def _norm_small_body(*refs, has_moe):
    if has_moe:
        x_ref, y0_ref, y1_ref, gate_ref, g_ref, sc_ref, sh_ref, w_ref, xo_ref, h_ref, s_ref = refs
        x = _add_moe(x_ref, y0_ref, y1_ref, gate_ref)
        xo_ref[...] = x
    else:
        x_ref, g_ref, sc_ref, sh_ref, w_ref, h_ref, s_ref = refs
        x = x_ref[...]
    ms = jnp.mean(x * x, axis=-1, keepdims=True)
    y = x * lax.rsqrt(ms + NORM_EPS) * g_ref[...]
    h = y * (1.0 + sc_ref[0]) + sh_ref[0]
    h_ref[...] = h.astype(h_ref.dtype)
    s_ref[...] = jnp.dot(h.astype(BF16), w_ref[...], preferred_element_type=F32)


def _norm_small(x2, g, scale, shift, w_small, seq, moe_in=None):
    n, d = x2.shape
    tm = _tile(seq, 512)
    per_b = seq // tm
    ws = w_small.shape[1]
    row = pl.BlockSpec((tm, d), lambda i: (i, 0))
    has_moe = moe_in is not None
    outs = pl.pallas_call(
        functools.partial(_norm_small_body, has_moe=has_moe),
        grid=(n // tm,),
        in_specs=[row] + (_moe_in_specs(tm, d, per_b) if has_moe else []) + [
            pl.BlockSpec((1, d), lambda i: (0, 0)),
            pl.BlockSpec((1, 1, d), lambda i: (i // per_b, 0, 0)),
            pl.BlockSpec((1, 1, d), lambda i: (i // per_b, 0, 0)),
            pl.BlockSpec((d, ws), lambda i: (0, 0)),
        ],
        out_specs=([row] if has_moe else []) + [row, pl.BlockSpec((tm, ws), lambda i: (i, 0))],
        out_shape=([jax.ShapeDtypeStruct((n, d), F32)] if has_moe else []) + [
            jax.ShapeDtypeStruct((n, d), BF16),
            jax.ShapeDtypeStruct((n, ws), F32),
        ],
        compiler_params=_cparams("parallel"),
        name="norm_modulate_small_proj",
    )(x2, *(moe_in if has_moe else ()), g.reshape(1, d), scale, shift, w_small)
    return outs if has_moe else (x2, *outs)


def _mm_body(a_ref, w_ref, o_ref, *, act):
    acc = jnp.dot(a_ref[...], w_ref[0], preferred_element_type=F32)
    if act == "sigmoid":
        acc = jax.nn.sigmoid(acc)
    o_ref[...] = acc.astype(o_ref.dtype)


def _matmul(a, w, layer, out_dtype, act=None, tm=1024, tn=1024, name="matmul"):
    m, k = a.shape
    n = w.shape[2]
    tm = _tile(m, tm)
    tn = _tile(n, tn)
    return pl.pallas_call(
        functools.partial(_mm_body, act=act),
        grid=(m // tm, n // tn),
        in_specs=[
            pl.BlockSpec((tm, k), lambda i, j: (i, 0)),
            pl.BlockSpec((1, k, tn), lambda i, j: (layer, 0, j)),
        ],
        out_specs=pl.BlockSpec((tm, tn), lambda i, j: (i, j)),
        out_shape=jax.ShapeDtypeStruct((m, n), out_dtype),
        compiler_params=_cparams("parallel", "parallel"),
        name=name,
    )(a, w)


def _mm_residual_body(a_ref, w_ref, x_ref, gate_ref, o_ref):
    acc = jnp.dot(a_ref[...], w_ref[...], preferred_element_type=F32)
    o_ref[...] = x_ref[...] + gate_ref[0] * acc


def _matmul_residual(a, w, x2, gate, seq, tm=1024, tn=1024):
    m, k = a.shape
    n = w.shape[1]
    tm = _tile(seq, tm)
    tn = _tile(n, tn)
    per_b = seq // tm
    return pl.pallas_call(
        _mm_residual_body,
        grid=(m // tm, n // tn),
        in_specs=[
            pl.BlockSpec((tm, k), lambda i, j: (i, 0)),
            pl.BlockSpec((k, tn), lambda i, j: (0, j)),
            pl.BlockSpec((tm, tn), lambda i, j: (i, j)),
            pl.BlockSpec((1, 1, tn), lambda i, j: (i // per_b, 0, j)),
        ],
        out_specs=pl.BlockSpec((tm, tn), lambda i, j: (i, j)),
        out_shape=jax.ShapeDtypeStruct((m, n), F32),
        compiler_params=_cparams("parallel", "parallel"),
        name="out_proj_residual",
    )(a, w, x2, gate)


def _latent_norm(a_ref, g_ref):
    a = a_ref[...]
    ms = jnp.mean(a * a, axis=-1, keepdims=True)
    return (a * lax.rsqrt(ms + NORM_EPS) * g_ref[...]).astype(BF16)


def _mla_q_body(a_ref, g_ref, w_ref, cos_ref, sin_ref, nope_ref, rope_ref):
    acc = jnp.dot(_latent_norm(a_ref, g_ref), w_ref[...], preferred_element_type=F32)
    w = nope_ref.shape[1]
    nope_ref[...] = acc[:, :w].astype(BF16)
    rope_ref[...] = (acc[:, w:2 * w] * cos_ref[...] + acc[:, 2 * w:] * sin_ref[...]).astype(BF16)


def _mla_q(latent, col_block, g, w_q, cos_t, sin_t, seq):
    n = latent.shape[0]
    k = w_q.shape[0]
    w = w_q.shape[1] // 3
    tm = _tile(seq, 1024)
    per_b = seq // tm
    return pl.pallas_call(
        _mla_q_body,
        grid=(n // tm,),
        in_specs=[
            pl.BlockSpec((tm, k), lambda i: (i, col_block)),
            pl.BlockSpec((1, k), lambda i: (0, 0)),
            pl.BlockSpec((k, 3 * w), lambda i: (0, 0)),
            pl.BlockSpec((tm, w), lambda i: (i % per_b, 0)),
            pl.BlockSpec((tm, w), lambda i: (i % per_b, 0)),
        ],
        out_specs=[
            pl.BlockSpec((tm, w), lambda i: (i, 0)),
            pl.BlockSpec((tm, w), lambda i: (i, 0)),
        ],
        out_shape=[
            jax.ShapeDtypeStruct((n, w), BF16),
            jax.ShapeDtypeStruct((n, w), BF16),
        ],
        compiler_params=_cparams("parallel"),
        name="mla_q_up_rope",
    )(latent, g.reshape(1, k), w_q, cos_t, sin_t)


def _mla_kv_body(a_ref, g_ref, w_ref, o_ref):
    o_ref[...] = jnp.dot(_latent_norm(a_ref, g_ref), w_ref[...],
                         preferred_element_type=F32).astype(BF16)


def _mla_kv(latent, col_block, g, w_kv, seq):
    n = latent.shape[0]
    k, nout = w_kv.shape
    tm = _tile(seq, 1024)
    return pl.pallas_call(
        _mla_kv_body,
        grid=(n // tm,),
        in_specs=[
            pl.BlockSpec((tm, k), lambda i: (i, col_block)),
            pl.BlockSpec((1, k), lambda i: (0, 0)),
            pl.BlockSpec((k, nout), lambda i: (0, 0)),
        ],
        out_specs=pl.BlockSpec((tm, nout), lambda i: (i, 0)),
        out_shape=jax.ShapeDtypeStruct((n, nout), BF16),
        compiler_params=_cparams("parallel"),
        name="mla_kv_up",
    )(latent, g.reshape(1, k), w_kv)


def _softmax_attn_body(*refs, tq, tk, has_extra, has_bias):
    refs = list(refs)
    qm_ref = refs.pop(0)
    qe_ref = refs.pop(0) if has_extra else None
    km_ref = refs.pop(0)
    ke_ref = refs.pop(0) if has_extra else None
    v_ref = refs.pop(0)
    kb_ref = refs.pop(0) if has_bias else None
    o_ref, m_ref, acc_ref = refs
    qi = pl.program_id(1)
    m_ref[...] = jnp.full(m_ref.shape, -jnp.inf, F32)
    acc_ref[...] = jnp.zeros(acc_ref.shape, F32)
    chunks_per_q = tq // tk
    n_full = qi * chunks_per_q
    ones = jnp.ones((tk, HEAD_DIM), BF16)

    def head_chunk(h, k0, diag):
        lanes = slice(h * HEAD_DIM, (h + 1) * HEAD_DIM)
        q = qm_ref[0, :, lanes]
        k = km_ref[0, pl.ds(k0, tk), lanes]
        if has_extra:
            q = jnp.concatenate([q, qe_ref[0, :, lanes]], axis=-1)
            k = jnp.concatenate([k, ke_ref[0, pl.ds(k0, tk), :]], axis=-1)
        s = lax.dot_general(q, k, (((1,), (1,)), ((), ())), preferred_element_type=F32)
        if has_bias:
            s = s + kb_ref[0, h, :, pl.ds(k0, tk)]
        if diag is not None:
            row = lax.broadcasted_iota(jnp.int32, (tq, tk), 0)
            col = lax.broadcasted_iota(jnp.int32, (tq, tk), 1) + diag * tk
            s = jnp.where(row >= col, s, -jnp.inf)
        blocks = [s[:, j * LANE:(j + 1) * LANE] for j in range(tk // LANE)]
        mx = blocks[0]
        for blk in blocks[1:]:
            mx = jnp.maximum(mx, blk)
        m_prev = m_ref[h]
        m_new = jnp.maximum(m_prev, jnp.max(mx, axis=-1, keepdims=True))
        alpha = jnp.exp(m_prev - m_new)
        p = jnp.concatenate([jnp.exp(blk - m_new) for blk in blocks], axis=-1).astype(BF16)
        v_ext = jnp.concatenate([v_ref[0, pl.ds(k0, tk), lanes], ones], axis=-1)
        pv = jnp.dot(p, v_ext, preferred_element_type=F32)
        acc_ref[h, :, :HEAD_DIM] = alpha * acc_ref[h, :, :HEAD_DIM] + pv[:, :HEAD_DIM]
        acc_ref[h, :, HEAD_DIM:] = alpha * acc_ref[h, :, HEAD_DIM:] + pv[:, HEAD_DIM:]
        m_ref[h] = m_new

    def chunk(c, diag):
        k0 = pl.multiple_of(c * tk, tk)
        for h in range(N_HEADS):
            head_chunk(h, k0, diag)

    def full_chunk(c, carry):
        chunk(c, None)
        return carry

    lax.fori_loop(0, n_full, full_chunk, 0)
    for d in range(chunks_per_q):
        chunk(n_full + d, d)
    for h in range(N_HEADS):
        o_ref[0, :, h * HEAD_DIM:(h + 1) * HEAD_DIM] = (
            acc_ref[h, :, :HEAD_DIM] / acc_ref[h, :, HEAD_DIM:]).astype(o_ref.dtype)


def _softmax_attention(qm, q_blk, km, k_blk, v, v_blk, batch, seq, extra=None, key_bias=None):
    tq = _tile(seq, 512)
    tk = _tile(seq, 512)
    width = N_HEADS * HEAD_DIM
    args = [qm]
    specs = [pl.BlockSpec((1, tq, width), lambda b, i: (b, i, q_blk))]
    if extra is not None:
        args.append(extra[0])
        specs.append(pl.BlockSpec((1, tq, width), lambda b, i: (b, i, 0)))
    args.append(km)
    specs.append(pl.BlockSpec((1, seq, width), lambda b, i: (b, 0, k_blk)))
    if extra is not None:
        args.append(extra[1])
        specs.append(pl.BlockSpec((1, seq, HEAD_DIM), lambda b, i: (b, 0, 0)))
    args.append(v)
    specs.append(pl.BlockSpec((1, seq, width), lambda b, i: (b, 0, v_blk)))
    if key_bias is not None:
        args.append(key_bias)
        specs.append(pl.BlockSpec((1, N_HEADS, 1, seq), lambda b, i: (b, 0, 0, 0)))

    return pl.pallas_call(
        functools.partial(_softmax_attn_body, tq=tq, tk=tk, has_extra=extra is not None,
                          has_bias=key_bias is not None),
        grid=(batch, seq // tq),
        in_specs=specs,
        out_specs=pl.BlockSpec((1, tq, width), lambda b, i: (b, i, 0)),
        out_shape=jax.ShapeDtypeStruct((batch, seq, width), BF16),
        scratch_shapes=[
            pltpu.VMEM((N_HEADS, tq, HEAD_DIM), F32),
            pltpu.VMEM((N_HEADS, tq, 2 * HEAD_DIM), F32),
        ],
        compiler_params=_cparams("parallel", "arbitrary"),
        name="causal_softmax_attention",
    )(*args)


def _sb_attn_body(q_ref, k_ref, v_ref, tri_ref, o_ref, r_ref, acc_ref, *, tq, tk):
    qi = pl.program_id(1)
    r_ref[...] = jnp.zeros(r_ref.shape, F32)
    acc_ref[...] = jnp.zeros(acc_ref.shape, F32)
    chunks_per_q = tq // tk
    n_full = qi * chunks_per_q

    def head_chunk(h, k0, diag):
        lanes = slice(h * HEAD_DIM, (h + 1) * HEAD_DIM)
        z = lax.dot_general(q_ref[0, :, lanes], k_ref[0, pl.ds(k0, tk), lanes],
                            (((1,), (1,)), ((), ())), preferred_element_type=F32)
        sp = jnp.maximum(z, 0.0) + jnp.log(1.0 + jnp.exp(-jnp.abs(z)))
        log_beta = z - sp
        if diag is not None:
            row = lax.broadcasted_iota(jnp.int32, (tq, tk), 0)
            col = lax.broadcasted_iota(jnp.int32, (tq, tk), 1) + diag * tk
            strict = row > col
            sp = jnp.where(strict, sp, 0.0)
        hi = sp.astype(BF16)
        lo = (sp - hi.astype(F32)).astype(BF16)
        tri = tri_ref[...]
        e =jnp.dot(hi, tri, preferred_element_type=F32) + jnp.dot(lo, tri, preferred_element_type=F32)
        r = r_ref[h]
        a = jnp.concatenate(
            [jnp.exp(log_beta[:, j * LANE:(j + 1) * LANE] - e[:, j * LANE:(j + 1) * LANE] - r)
             for j in range(tk // LANE)], axis=-1)
        if diag is not None:
            a = jnp.where(strict, a, 0.0)
        acc_ref[h] += jnp.dot(a.astype(BF16), v_ref[0, pl.ds(k0, tk), lanes],
                              preferred_element_type=F32)
        r_ref[h] = r + e[:, tk:]

    def chunk(c, diag):
        k0 = pl.multiple_of(c * tk, tk)
        for h in range(N_HEADS):
            head_chunk(h, k0, diag)

    for d in reversed(range(chunks_per_q)):
        chunk(n_full + d, d)

    def full_chunk(i, carry):
        chunk(n_full - 1 - i, None)
        return carry

    lax.fori_loop(0, n_full, full_chunk, 0)
    for h in range(N_HEADS):
        o_ref[0, :, h * HEAD_DIM:(h + 1) * HEAD_DIM] = acc_ref[h].astype(o_ref.dtype)


def _sb_tri(tk):
    j_src = np.arange(tk)[:, None]
    j_dst = np.arange(tk + LANE)[None, :]
    return jnp.asarray((j_src > j_dst) | (j_dst >= tk), dtype=BF16)


def _sb_attention(qkv, q_blk, k_blk, v_blk, batch, seq):
    tq = _tile(seq, 512)
    tk = _tile(seq, 256)
    width = N_HEADS * HEAD_DIM
    return pl.pallas_call(
        functools.partial(_sb_attn_body, tq=tq, tk=tk),
        grid=(batch, seq // tq),
        in_specs=[
            pl.BlockSpec((1, tq, width), lambda b, i: (b, i, q_blk)),
            pl.BlockSpec((1, seq, width), lambda b, i: (b, 0, k_blk)),
            pl.BlockSpec((1, seq, width), lambda b, i: (b, 0, v_blk)),
            pl.BlockSpec((tk, tk + LANE), lambda b, i: (0, 0)),
        ],
        out_specs=pl.BlockSpec((1, tq, width), lambda b, i: (b, i, 0)),
        out_shape=jax.ShapeDtypeStruct((batch, seq, width), BF16),
        scratch_shapes=[
            pltpu.VMEM((N_HEADS, tq, HEAD_DIM), F32),
            pltpu.VMEM((N_HEADS, tq, HEAD_DIM), F32),
        ],
        compiler_params=_cparams("parallel", "arbitrary"),
        name="stick_breaking_attention",
    )(qkv, qkv, qkv, _sb_tri(tk))


def _pool_body(u_ref, halo_ref, w_ref, sc_ref, o_ref, ext_ref, *, ts):
    i = pl.program_id(1)
    halo = jnp.where(i > 0, halo_ref[0], 0.0)
    ext_ref[:POOL_HALO, :] = halo
    ext_ref[POOL_HALO:, :] = u_ref[0]
    pos = i * ts + lax.broadcasted_iota(jnp.int32, (ts, 1), 0)
    outs = []
    for g, win in enumerate(POOL_WINDOWS):
        lanes = slice(g * LANE, (g + 1) * LANE)
        tok = ext_ref[POOL_HALO:, lanes]
        tot = tok
        for back in range(1, win):
            tot = tot + ext_ref[POOL_HALO - back:POOL_HALO - back + ts, lanes]
        cnt = jnp.minimum(pos + 1, win).astype(F32)
        pooled = (tot / cnt - tok).astype(BF16)
        outs.append(jnp.dot(pooled, w_ref[g], preferred_element_type=F32))
    y = jnp.concatenate(outs, axis=-1) * sc_ref[...]
    o_ref[0] = y.astype(o_ref.dtype)


def _pool_mixer(latent3, col_block, w_pool, pool_scale, batch, seq):
    width = len(POOL_WINDOWS) * LANE
    ts = _tile(seq, 512)
    halo_per_tile = ts // POOL_HALO
    return pl.pallas_call(
        functools.partial(_pool_body, ts=ts),
        grid=(batch, seq // ts),
        in_specs=[
            pl.BlockSpec((1, ts, width), lambda b, i: (b, i, col_block)),
            pl.BlockSpec((1, POOL_HALO, width),
                         lambda b, i: (b, jnp.maximum(i * halo_per_tile - 1, 0), col_block)),
            pl.BlockSpec((len(POOL_WINDOWS), LANE, LANE), lambda b, i: (0, 0, 0)),
            pl.BlockSpec((1, width), lambda b, i: (0, 0)),
        ],
        out_specs=pl.BlockSpec((1, ts, width), lambda b, i: (b, i, 0)),
        out_shape=jax.ShapeDtypeStruct((batch, seq, width), BF16),
        scratch_shapes=[pltpu.VMEM((ts + POOL_HALO, width), F32)],
        compiler_params=_cparams("parallel", "arbitrary"),
        name="multiscale_pool",
    )(latent3, latent3, w_pool.astype(BF16), pool_scale.reshape(1, width))


def _merge_body(y0, y1, y2, y3, g0, g1, g2, g3, wb_ref, o_ref):
    acc = None
    for n, (y, g) in enumerate(((y0, g0), (y1, g1), (y2, g2), (y3, g3))):
        t = g[...].astype(F32) * jnp.dot(y[...], wb_ref[n], preferred_element_type=F32)
        acc = t if acc is None else acc + t
    o_ref[...] = acc.astype(o_ref.dtype)


def _merge(ys, gates, w_branch):
    n = ys[0].shape[0]
    d = w_branch.shape[2]
    tm = _tile(n, 1024)
    tn = _tile(d, 512)
    nj = d // tn
    y_spec = pl.BlockSpec((tm, BRANCH_WIDTH), lambda i, j: (i, 0))

    def g_spec(b):
        return pl.BlockSpec((tm, tn), lambda i, j: (i, b * nj + j))

    return pl.pallas_call(
        _merge_body,
        grid=(n // tm, nj),
        in_specs=[y_spec] * N_BRANCHES + [g_spec(b) for b in range(N_BRANCHES)] + [
            pl.BlockSpec((N_BRANCHES, BRANCH_WIDTH, tn), lambda i, j: (0, 0, j))],
        out_specs=pl.BlockSpec((tm, tn), lambda i, j: (i, j)),
        out_shape=jax.ShapeDtypeStruct((n, d), BF16),
        compiler_params=_cparams("parallel", "parallel"),
        name="gated_branch_merge",
    )(*ys, gates, gates, gates, gates, w_branch)


def _expert_changed(be_ref):
    blk = pl.program_id(1)
    return (blk == 0) | (be_ref[blk] != be_ref[jnp.maximum(blk - 1, 0)])


def _moe_up_body(be_ref, nb_ref, x_ref, wg_ref, wu_ref, o_ref, wg_s, wu_s):
    blk = pl.program_id(1)

    @pl.when(_expert_changed(be_ref))
    def _():
        wg_s[...] = wg_ref[0, 0].astype(BF16)
        wu_s[...] = wu_ref[0, 0].astype(BF16)

    @pl.when(blk < nb_ref[0])
    def _():
        x = x_ref[...]
        g = jnp.dot(x, wg_s[...], preferred_element_type=F32)
        u = jnp.dot(x, wu_s[...], preferred_element_type=F32)
        o_ref[...] = (g * jax.nn.sigmoid(g) * u).astype(o_ref.dtype)

    @pl.when(blk >= nb_ref[0])
    def _():
        o_ref[...] = jnp.zeros(o_ref.shape, o_ref.dtype)


def _moe_down_body(be_ref, nb_ref, a_ref, wd_ref, rw_ref, o_ref, wd_s):
    blk = pl.program_id(1)

    @pl.when(_expert_changed(be_ref))
    def _():
        wd_s[...] = wd_ref[0, 0].astype(BF16)

    @pl.when(blk < nb_ref[0])
    def _():
        y = jnp.dot(a_ref[...], wd_s[...], preferred_element_type=F32)
        o_ref[...] = (y * rw_ref[...]).astype(o_ref.dtype)

    @pl.when(blk >= nb_ref[0])
    def _():
        o_ref[...] = jnp.zeros(o_ref.shape, o_ref.dtype)


def _moe_ffn(x_buf, row_w, block_expert, n_used, w_gate, w_up, w_down, layer, tb):
    n_rows, d = x_buf.shape
    f = w_gate.shape[3]
    tf = _tile(f, 512)
    tn = _tile(d, 2048)
    n_blocks = n_rows // tb

    def used(i, nb):
        return jnp.minimum(i, nb[0] - 1)

    act = pl.pallas_call(
        _moe_up_body,
        grid_spec=pltpu.PrefetchScalarGridSpec(
            num_scalar_prefetch=2,
            grid=(f // tf, n_blocks),
            in_specs=[
                pl.BlockSpec((tb, d), lambda j, i, be, nb: (used(i, nb), 0)),
                pl.BlockSpec((1, 1, d, tf), lambda j, i, be, nb: (layer, be[i], 0, j)),
                pl.BlockSpec((1, 1, d, tf), lambda j, i, be, nb: (layer, be[i], 0, j)),
            ],
            out_specs=pl.BlockSpec((tb, tf), lambda j, i, be, nb: (i, j)),
            scratch_shapes=[pltpu.VMEM((d, tf), BF16), pltpu.VMEM((d, tf), BF16)],
        ),
        out_shape=jax.ShapeDtypeStruct((n_rows, f), BF16),
        compiler_params=_cparams("arbitrary", "arbitrary"),
        name="expert_gate_up_swiglu",
    )(block_expert, n_used, x_buf, w_gate, w_up)
    return pl.pallas_call(
        _moe_down_body,
        grid_spec=pltpu.PrefetchScalarGridSpec(
            num_scalar_prefetch=2,
            grid=(d // tn, n_blocks),
            in_specs=[
                pl.BlockSpec((tb, f), lambda j, i, be, nb: (used(i, nb), 0)),
                pl.BlockSpec((1, 1, f, tn), lambda j, i, be, nb: (layer, be[i], 0, j)),
                pl.BlockSpec((tb, 1), lambda j, i, be, nb: (used(i, nb), 0)),
            ],
            out_specs=pl.BlockSpec((tb, tn), lambda j, i, be, nb: (i, j)),
            scratch_shapes=[pltpu.VMEM((f, tn), BF16)],
        ),
        out_shape=jax.ShapeDtypeStruct((n_rows, d), BF16),
        compiler_params=_cparams("arbitrary", "arbitrary"),
        name="expert_down",
    )(block_expert, n_used, act, w_down, row_w)


def _final_norm_body(x_ref, y0_ref, y1_ref, gate_ref, g_ref, o_ref):
    x = _add_moe(x_ref, y0_ref, y1_ref, gate_ref)
    ms = jnp.mean(x * x, axis=-1, keepdims=True)
    o_ref[...] = x * lax.rsqrt(ms + NORM_EPS) * g_ref[...]


def _final_norm(x2, moe_in, g, seq):
    n, d = x2.shape
    tm = _tile(seq, 512)
    row = pl.BlockSpec((tm, d), lambda i: (i, 0))
    return pl.pallas_call(
        _final_norm_body,
        grid=(n // tm,),
        in_specs=[row] + _moe_in_specs(tm, d, seq // tm) + [pl.BlockSpec((1, d), lambda i: (0, 0))],
        out_specs=row,
        out_shape=jax.ShapeDtypeStruct((n, d), F32),
        compiler_params=_cparams("parallel"),
        name="final_rmsnorm",
    )(x2, *moe_in, g.reshape(1, d))


def _rope_tables(seq):
    half = MLA_ROPE_DIM // 2
    inv = jnp.power(ROPE_BASE, -2.0 * jnp.arange(half, dtype=F32) / MLA_ROPE_DIM)
    ang = jnp.arange(seq).astype(F32)[:, None] * inv[None, :]
    cos, sin = jnp.cos(ang), jnp.sin(ang)
    c64 = jnp.concatenate([cos, cos], axis=-1)
    s64 = jnp.concatenate([-sin, sin], axis=-1)
    return c64, s64


def _swap_halves(w):
    half = w.shape[-1] // 2
    return jnp.concatenate([w[..., half:], w[..., :half]], axis=-1)


def _pad_lanes(a, width):
    return jnp.pad(a, [(0, 0)] * (a.ndim - 1) + [(0, width - a.shape[-1])])


def _mixer_layer(x2, h, small, layer, batch, seq, prm):
    n, d = x2.shape
    (w_qkv, w_lat, w_gates, b_forget, g_q, w_q, g_kv, w_kv, w_pool, pool_scale, w_branch, w_out,
     gate1) = prm
    c64, s64 = _rope_tables(seq)

    qkv = _matmul(h, w_qkv, layer, BF16, name="fox_sb_qkv_proj").reshape(batch, seq, -1)
    latent = _matmul(h, w_lat, layer, F32, tn=640, name="latent_pool_proj")
    gates = _matmul(h, w_gates, layer, BF16, act="sigmoid", name="branch_gate_proj")

    fox_f = small[:, 2 * MLA_ROPE_DIM:2 * MLA_ROPE_DIM + N_HEADS].reshape(batch, seq, N_HEADS)
    log_f_cum = jnp.cumsum(jax.nn.log_sigmoid(fox_f + b_forget), axis=1)
    k_bias = -log_f_cum.transpose(0, 2, 1)[:, :, None, :]
    nh = N_HEADS
    y_fox = _softmax_attention(qkv, 0, qkv, 1, qkv, 2, batch, seq, key_bias=k_bias)

    y_sb = _sb_attention(qkv, 3, 4, 5, batch, seq)

    cos_t = jnp.tile(_pad_lanes(c64, HEAD_DIM), (1, nh))
    sin_t = jnp.tile(_pad_lanes(s64, HEAD_DIM), (1, nh))
    q_nope, q_rope = _mla_q(latent, 0, g_q, w_q, cos_t, sin_t, seq)
    kv = _mla_kv(latent, 4, g_kv, w_kv, seq).reshape(batch, seq, -1)
    kr = small[:, :MLA_ROPE_DIM] * jnp.tile(c64, (batch, 1)) \
        + small[:, MLA_ROPE_DIM:2 * MLA_ROPE_DIM] * jnp.tile(s64, (batch, 1))
    kr = _pad_lanes(kr.astype(BF16), HEAD_DIM).reshape(batch, seq, HEAD_DIM)
    y_mla = _softmax_attention(q_nope.reshape(batch, seq, -1), 0, kv, 0, kv, 1, batch, seq,
                               extra=(q_rope.reshape(batch, seq, -1), kr))

    y_pool = _pool_mixer(latent.reshape(batch, seq, -1), 1, w_pool, pool_scale, batch, seq)

    ys = [y.reshape(n, BRANCH_WIDTH) for y in (y_fox, y_sb, y_mla, y_pool)]
    merged = _merge(ys, gates, w_branch)
    return _matmul_residual(merged, w_out, x2, gate1, seq)


def _moe_layer(h, logits, gate2, b_rg, b_re, w_gate, w_up, w_down, layer):
    n, d = h.shape
    tb = 512
    g_logits = logits[:, :N_EXPERT_GROUPS] + b_rg
    g_idx = jnp.argmax(g_logits, axis=-1).astype(jnp.int32)[:, None]
    g_top = jnp.max(g_logits, axis=-1)
    p_group = jnp.exp(g_top - jax.nn.logsumexp(g_logits, axis=-1))
    e_logits = (logits[:, N_EXPERT_GROUPS:N_EXPERT_GROUPS + N_EXPERTS] + b_re).reshape(
        n, N_EXPERT_GROUPS, EXPERTS_PER_GROUP)
    e_logits = jnp.take_along_axis(e_logits, g_idx[:, :, None], axis=1)[:, 0]
    probs = jax.nn.softmax(e_logits, axis=-1)
    lane = jnp.arange(EXPERTS_PER_GROUP, dtype=jnp.int32)[None, :]
    i1 = jnp.argmax(probs, axis=-1).astype(jnp.int32)[:, None]
    p1 = jnp.max(probs, axis=-1, keepdims=True)
    rest = jnp.where(lane == i1, -jnp.inf, probs)
    i2 = jnp.argmax(rest, axis=-1).astype(jnp.int32)[:, None]
    p2 = jnp.max(rest, axis=-1, keepdims=True)
    top_p = jnp.concatenate([p1, p2], axis=-1)
    top_i = jnp.concatenate([i1, i2], axis=-1)
    weights = p_group[:, None] * top_p / jnp.sum(top_p, axis=-1, keepdims=True)

    expert_id = (g_idx * EXPERTS_PER_GROUP + top_i).reshape(-1).astype(jnp.int32)
    m = expert_id.shape[0]
    w_flat = weights.reshape(-1)

    order = jnp.argsort(expert_id).astype(jnp.int32)
    rank = jnp.argsort(order).astype(jnp.int32)
    onehot = expert_id[:, None] == jnp.arange(N_EXPERTS, dtype=jnp.int32)[None, :]
    counts = jnp.sum(onehot, axis=0, dtype=jnp.int32)
    padded = ((counts + tb - 1) // tb) * tb
    start = jnp.cumsum(counts) - counts
    pend = jnp.cumsum(padded)
    shift = pend - padded - start
    n_rows = m + N_EXPERTS * tb
    n_blocks = n_rows // tb
    blk = jnp.arange(n_blocks, dtype=jnp.int32)
    n_used = (pend[-1] // tb).astype(jnp.int32)
    block_expert = jnp.minimum(
        jnp.sum(pend[None, :] <= (blk * tb)[:, None], axis=1), N_EXPERTS - 1).astype(jnp.int32)
    last_used = block_expert[jnp.maximum(n_used - 1, 0)]
    block_expert = jnp.where(blk < n_used, block_expert, last_used)

    rows = jnp.arange(n_rows, dtype=jnp.int32).reshape(n_blocks, tb)
    src_sorted = rows - shift[block_expert][:, None]
    valid = (src_sorted < (start + counts)[block_expert][:, None]) & (blk < n_used)[:, None]
    src_assign = order.at[jnp.clip(src_sorted, 0, m - 1).reshape(-1)].get(mode="promise_in_bounds")
    valid = valid.reshape(-1)
    row_tok = jnp.where(valid, src_assign // EXPERT_TOP_K, rows.reshape(-1) % n)
    row_w = jnp.where(valid, w_flat.at[src_assign].get(mode="promise_in_bounds"), 0.0).astype(F32)
    dest = (rank + jnp.sum(jnp.where(onehot, shift[None, :], 0), axis=1)).reshape(n, EXPERT_TOP_K)

    x_buf = h.at[row_tok].get(mode="promise_in_bounds")
    y = _moe_ffn(x_buf, row_w.reshape(n_rows, 1), block_expert, n_used.reshape(1),
                 w_gate, w_up, w_down, layer, tb)
    return (y.at[dest[:, 0]].get(mode="promise_in_bounds"),
            y.at[dest[:, 1]].get(mode="promise_in_bounds"), gate2)


def kernel(x, c, w_mod, b_mod, g_norm1, g_norm2, w_in, b_forget, g_q_norm, w_uq, g_kv_norm, w_ukv, w_pool, pool_scale, w_branch, w_out, w_route_group, b_route_group, w_route_expert, b_route_expert, w_gate, w_up, w_down, g_final):
    batch, seq, d = x.shape
    depth = w_mod.shape[0]
    n = batch * seq
    nh = N_HEADS

    c_pad = jnp.pad(c, ((0, 8 - batch), (0, 0)))
    mod = _modulation(c_pad, w_mod, b_mod)[:, :batch]

    hq = nh * HEAD_DIM
    o_fox, o_f = 0, 3 * hq
    o_sb = o_f + nh
    o_cq = o_sb + 3 * hq
    o_ckv = o_cq + MLA_Q_LORA
    o_kr = o_ckv + MLA_KV_LORA
    o_pool = o_kr + MLA_ROPE_DIM
    o_gate = o_pool + len(POOL_WINDOWS) * LANE
    attn_scale = HEAD_DIM ** -0.5
    mla_scale = (MLA_NOPE_DIM + MLA_ROPE_DIM) ** -0.5

    half = MLA_ROPE_DIM // 2
    plans = (
        [(0, o_fox, o_fox + hq, attn_scale), (hq, o_fox + hq, o_fox + 3 * hq, 1.0),
         (3 * hq, o_sb, o_sb + hq, attn_scale), (4 * hq, o_sb + hq, o_sb + 3 * hq, 1.0)],
        [(0, o_cq, o_ckv, 1.0), (MLA_Q_LORA, o_pool, o_gate, 1.0),
         (MLA_Q_LORA + o_gate - o_pool, o_ckv, o_kr, 1.0)],
        [(0, o_gate, w_in.shape[2], 1.0)],
        [(0, None, SMALL_WIDTH, 0.0), (0, o_kr, o_pool, 1.0), (MLA_ROPE_DIM, o_kr + half, o_pool, 1.0),
         (MLA_ROPE_DIM + half, o_kr, o_kr + half, 1.0), (2 * MLA_ROPE_DIM, o_f, o_sb, 1.0)],
    )
    w_qkv, w_lat, w_gates, w_small1 = _regroup_w_in(
        w_in, plans, (6 * hq, o_kr - o_cq + o_gate - o_pool, w_in.shape[2] - o_gate, SMALL_WIDTH))

    x2 = x.reshape(n, d)
    pending = None
    for l in range(depth):
        m6 = mod[l].reshape(batch, 6, 1, d)
        shift1, scale1, gate1, shift2, scale2, gate2 = [m6[:, t] for t in range(6)]

        wq = w_uq[l].reshape(MLA_Q_LORA, nh, MLA_NOPE_DIM + MLA_ROPE_DIM) * mla_scale
        wq_rope = wq[:, :, MLA_NOPE_DIM:]
        w_q = jnp.concatenate([
            wq[:, :, :MLA_NOPE_DIM].reshape(MLA_Q_LORA, -1),
            _pad_lanes(wq_rope, HEAD_DIM).reshape(MLA_Q_LORA, -1),
            _pad_lanes(_swap_halves(wq_rope), HEAD_DIM).reshape(MLA_Q_LORA, -1)], axis=1).astype(BF16)
        wkv = w_ukv[l].reshape(MLA_KV_LORA, nh, 2 * HEAD_DIM)
        w_kv = jnp.concatenate([wkv[:, :, :HEAD_DIM].reshape(MLA_KV_LORA, -1),
                                wkv[:, :, HEAD_DIM:].reshape(MLA_KV_LORA, -1)], axis=1).astype(BF16)

        x2, h, small = _norm_small(x2, g_norm1[l], scale1, shift1, w_small1[l], seq,
                                   moe_in=pending)
        prm = (w_qkv, w_lat, w_gates, b_forget[l], g_q_norm[l], w_q, g_kv_norm[l], w_kv,
               w_pool[l], pool_scale[l], w_branch[l].astype(BF16), w_out[l].astype(BF16), gate1)
        x2 = _mixer_layer(x2, h, small, l, batch, seq, prm)

        w_small2 = _pad_lanes(jnp.concatenate([w_route_group[l], w_route_expert[l]], axis=1),
                              SMALL_WIDTH).astype(BF16)
        _, h, logits = _norm_small(x2, g_norm2[l], scale2, shift2, w_small2, seq)
        pending = _moe_layer(h, logits, gate2, b_route_group[l], b_route_expert[l],
                             w_gate, w_up, w_down, l)
    return _final_norm(x2, pending, g_final, seq).reshape(batch, seq, d)
```

```python
import functools

import numpy as np
import jax
import jax.numpy as jnp
from jax import lax
from jax.experimental import pallas as pl
from jax.experimental.pallas import tpu as pltpu

F32 = jnp.float32
BF16 = jnp.bfloat16

NORM_EPS = 1e-6
N_HEADS = 4
HEAD_DIM = 128
MLA_Q_LORA = 512
MLA_KV_LORA = 256
MLA_NOPE_DIM = 128
MLA_ROPE_DIM = 64
ROPE_BASE = 10000.0
POOL_WINDOWS = (2, 4, 8, 16)
POOL_HALO = 16
N_BRANCHES = 4
BRANCH_WIDTH = 512
N_EXPERT_GROUPS = 4
EXPERTS_PER_GROUP = 8
N_EXPERTS = N_EXPERT_GROUPS * EXPERTS_PER_GROUP
EXPERT_TOP_K = 2
SMALL_WIDTH = 256

LANE = 128
V7X_VMEM_BYTES = 64 * 1024 * 1024
VMEM_LIMIT = V7X_VMEM_BYTES - 12 * 1024 * 1024


def _cparams(*sem, vmem=VMEM_LIMIT):
    return pltpu.CompilerParams(dimension_semantics=sem, vmem_limit_bytes=vmem)


def _tile(n, pref):
    t = min(n, pref)
    assert n % t == 0, (n, pref)
    return t


def _mod_body(c_ref, w_ref, b_ref, o_ref):
    c = c_ref[...]
    ca = c * jax.nn.sigmoid(c)
    acc = jnp.dot(ca.astype(BF16), w_ref[0].astype(BF16), preferred_element_type=F32)
    o_ref[0] = acc + b_ref[0]


def _modulation(c_pad, w_mod, b_mod):
    depth, d, n = w_mod.shape
    tn = _tile(n, 1024)
    return pl.pallas_call(
        _mod_body,
        grid=(depth, n // tn),
        in_specs=[
            pl.BlockSpec((8, d), lambda l, j: (0, 0)),
            pl.BlockSpec((1, d, tn), lambda l, j: (l, 0, j)),
            pl.BlockSpec((1, 1, tn), lambda l, j: (l, 0, j)),
        ],
        out_specs=pl.BlockSpec((1, 8, tn), lambda l, j: (l, 0, j)),
        out_shape=jax.ShapeDtypeStruct((depth, 8, n), F32),
        compiler_params=_cparams("parallel", "parallel"),
        name="adaln_modulation",
    )(c_pad, w_mod, b_mod.reshape(depth, 1, n))


REGROUP_CHUNK = 1024


def _regroup_body(w_ref, *out_refs, plans):
    total = w_ref.shape[2]

    def cols(a, b):
        a0 = (a // LANE) * LANE
        b1 = min(-(-b // LANE) * LANE, total)
        return w_ref[0, :, a0:b1][:, a - a0:b - a0]

    for o_ref, plan in zip(out_refs, plans):
        for dst, a, b, scale in plan:
            if a is None:
                o_ref[0, :, dst:b] = jnp.zeros((o_ref.shape[1], b - dst), o_ref.dtype)
                continue
            for c0 in range(0, b - a, REGROUP_CHUNK):
                c1 = min(c0 + REGROUP_CHUNK, b - a)
                v = cols(a + c0, a + c1)
                if scale != 1.0:
                    v = v * scale
                o_ref[0, :, dst + c0:dst + c1] = v.astype(o_ref.dtype)


def _regroup_w_in(w_in, plans, widths):
    depth, d, total = w_in.shape
    tr = _tile(d, 256)
    return pl.pallas_call(
        functools.partial(_regroup_body, plans=plans),
        grid=(depth, d // tr),
        in_specs=[pl.BlockSpec((1, tr, total), lambda l, i: (l, i, 0))],
        out_specs=[pl.BlockSpec((1, tr, w), lambda l, i: (l, i, 0)) for w in widths],
        out_shape=[jax.ShapeDtypeStruct((depth, d, w), BF16) for w in widths],
        compiler_params=_cparams("parallel", "parallel"),
        name="regroup_w_in",
    )(w_in)


def _add_moe(x_ref, y0_ref, y1_ref, gate_ref):
    return x_ref[...] + gate_ref[0] * (y0_ref[...].astype(F32) + y1_ref[...].astype(F32))


def _moe_in_specs(tm, d, per_b):
    row = pl.BlockSpec((tm, d), lambda i: (i, 0))
    return [row, row, pl.BlockSpec((1, 1, d), lambda i: (i // per_b, 0, 0))]


def _norm_small_body(*refs, has_moe):
    if has_moe:
        x_ref, y0_ref, y1_ref, gate_ref, g_ref, sc_ref, sh_ref, w_ref, xo_ref, h_ref, s_ref = refs
        x = _add_moe(x_ref, y0_ref, y1_ref, gate_ref)
        xo_ref[...] = x
    else:
        x_ref, g_ref, sc_ref, sh_ref, w_ref, h_ref, s_ref = refs
        x = x_ref[...]
    ms = jnp.mean(x * x, axis=-1, keepdims=True)
    y = x * lax.rsqrt(ms + NORM_EPS) * g_ref[...]
    h = y * (1.0 + sc_ref[0]) + sh_ref[0]
    h_ref[...] = h.astype(h_ref.dtype)
    s_ref[...] = jnp.dot(h.astype(BF16), w_ref[...], preferred_element_type=F32)


def _norm_small(x2, g, scale, shift, w_small, seq, moe_in=None):
    n, d = x2.shape
    tm = _tile(seq, 512)
    per_b = seq // tm
    ws = w_small.shape[1]
    row = pl.BlockSpec((tm, d), lambda i: (i, 0))
    has_moe = moe_in is not None
    outs = pl.pallas_call(
        functools.partial(_norm_small_body, has_moe=has_moe),
        grid=(n // tm,),
        in_specs=[row] + (_moe_in_specs(tm, d, per_b) if has_moe else []) + [
            pl.BlockSpec((1, d), lambda i: (0, 0)),
            pl.BlockSpec((1, 1, d), lambda i: (i // per_b, 0, 0)),
            pl.BlockSpec((1, 1, d), lambda i: (i // per_b, 0, 0)),
            pl.BlockSpec((d, ws), lambda i: (0, 0)),
        ],
        out_specs=([row] if has_moe else []) + [row, pl.BlockSpec((tm, ws), lambda i: (i, 0))],
        out_shape=([jax.ShapeDtypeStruct((n, d), F32)] if has_moe else []) + [
            jax.ShapeDtypeStruct((n, d), BF16),
            jax.ShapeDtypeStruct((n, ws), F32),
        ],
        compiler_params=_cparams("parallel"),
        name="norm_modulate_small_proj",
    )(x2, *(moe_in if has_moe else ()), g.reshape(1, d), scale, shift, w_small)
    return outs if has_moe else (x2, *outs)


def _mm_body(a_ref, w_ref, o_ref, *, act):
    acc = jnp.dot(a_ref[...], w_ref[0], preferred_element_type=F32)
    if act == "sigmoid":
        acc = jax.nn.sigmoid(acc)
    o_ref[...] = acc.astype(o_ref.dtype)


def _matmul(a, w, layer, out_dtype, act=None, tm=1024, tn=1024, name="matmul"):
    m, k = a.shape
    n = w.shape[2]
    tm = _tile(m, tm)
    tn = _tile(n, tn)
    return pl.pallas_call(
        functools.partial(_mm_body, act=act),
        grid=(m // tm, n // tn),
        in_specs=[
            pl.BlockSpec((tm, k), lambda i, j: (i, 0)),
            pl.BlockSpec((1, k, tn), lambda i, j: (layer, 0, j)),
        ],
        out_specs=pl.BlockSpec((tm, tn), lambda i, j: (i, j)),
        out_shape=jax.ShapeDtypeStruct((m, n), out_dtype),
        compiler_params=_cparams("parallel", "parallel"),
        name=name,
    )(a, w)


def _mm_residual_body(a_ref, w_ref, x_ref, gate_ref, o_ref):
    acc = jnp.dot(a_ref[...], w_ref[...], preferred_element_type=F32)
    o_ref[...] = x_ref[...] + gate_ref[0] * acc


def _matmul_residual(a, w, x2, gate, seq, tm=1024, tn=1024):
    m, k = a.shape
    n = w.shape[1]
    tm = _tile(seq, tm)
    tn = _tile(n, tn)
    per_b = seq // tm
    return pl.pallas_call(
        _mm_residual_body,
        grid=(m // tm, n // tn),
        in_specs=[
            pl.BlockSpec((tm, k), lambda i, j: (i, 0)),
            pl.BlockSpec((k, tn), lambda i, j: (0, j)),
            pl.BlockSpec((tm, tn), lambda i, j: (i, j)),
            pl.BlockSpec((1, 1, tn), lambda i, j: (i // per_b, 0, j)),
        ],
        out_specs=pl.BlockSpec((tm, tn), lambda i, j: (i, j)),
        out_shape=jax.ShapeDtypeStruct((m, n), F32),
        compiler_params=_cparams("parallel", "parallel"),
        name="out_proj_residual",
    )(a, w, x2, gate)


def _latent_norm(a_ref, g_ref):
    a = a_ref[...]
    ms = jnp.mean(a * a, axis=-1, keepdims=True)
    return (a * lax.rsqrt(ms + NORM_EPS) * g_ref[...]).astype(BF16)


def _mla_q_body(a_ref, g_ref, w_ref, cos_ref, sin_ref, nope_ref, rope_ref):
    acc = jnp.dot(_latent_norm(a_ref, g_ref), w_ref[...], preferred_element_type=F32)
    w = nope_ref.shape[1]
    nope_ref[...] = acc[:, :w].astype(BF16)
    rope_ref[...] = (acc[:, w:2 * w] * cos_ref[...] + acc[:, 2 * w:] * sin_ref[...]).astype(BF16)


def _mla_q(latent, col_block, g, w_q, cos_t, sin_t, seq):
    n = latent.shape[0]
    k = w_q.shape[0]
    w = w_q.shape[1] // 3
    tm = _tile(seq, 1024)
    per_b = seq // tm
    return pl.pallas_call(
        _mla_q_body,
        grid=(n // tm,),
        in_specs=[
            pl.BlockSpec((tm, k), lambda i: (i, col_block)),
            pl.BlockSpec((1, k), lambda i: (0, 0)),
            pl.BlockSpec((k, 3 * w), lambda i: (0, 0)),
            pl.BlockSpec((tm, w), lambda i: (i % per_b, 0)),
            pl.BlockSpec((tm, w), lambda i: (i % per_b, 0)),
        ],
        out_specs=[
            pl.BlockSpec((tm, w), lambda i: (i, 0)),
            pl.BlockSpec((tm, w), lambda i: (i, 0)),
        ],
        out_shape=[
            jax.ShapeDtypeStruct((n, w), BF16),
            jax.ShapeDtypeStruct((n, w), BF16),
        ],
        compiler_params=_cparams("parallel"),
        name="mla_q_up_rope",
    )(latent, g.reshape(1, k), w_q, cos_t, sin_t)


def _mla_kv_body(a_ref, g_ref, w_ref, o_ref):
    o_ref[...] = jnp.dot(_latent_norm(a_ref, g_ref), w_ref[...],
                         preferred_element_type=F32).astype(BF16)


def _mla_kv(latent, col_block, g, w_kv, seq):
    n = latent.shape[0]
    k, nout = w_kv.shape
    tm = _tile(seq, 1024)
    return pl.pallas_call(
        _mla_kv_body,
        grid=(n // tm,),
        in_specs=[
            pl.BlockSpec((tm, k), lambda i: (i, col_block)),
            pl.BlockSpec((1, k), lambda i: (0, 0)),
            pl.BlockSpec((k, nout), lambda i: (0, 0)),
        ],
        out_specs=pl.BlockSpec((tm, nout), lambda i: (i, 0)),
        out_shape=jax.ShapeDtypeStruct((n, nout), BF16),
        compiler_params=_cparams("parallel"),
        name="mla_kv_up",
    )(latent, g.reshape(1, k), w_kv)


def _softmax_attn_body(*refs, tq, tk, has_extra, has_bias):
    refs = list(refs)
    qm_ref = refs.pop(0)
    qe_ref = refs.pop(0) if has_extra else None
    km_ref = refs.pop(0)
    ke_ref = refs.pop(0) if has_extra else None
    v_ref = refs.pop(0)
    kb_ref = refs.pop(0) if has_bias else None
    o_ref, m_ref, acc_ref = refs
    qi = pl.program_id(1)
    m_ref[...] = jnp.full(m_ref.shape, -jnp.inf, F32)
    acc_ref[...] = jnp.zeros(acc_ref.shape, F32)
    chunks_per_q = tq // tk
    n_full = qi * chunks_per_q
    ones = jnp.ones((tk, HEAD_DIM), BF16)

    def head_chunk(h, k0, diag):
        lanes = slice(h * HEAD_DIM, (h + 1) * HEAD_DIM)
        q = qm_ref[0, :, lanes]
        k = km_ref[0, pl.ds(k0, tk), lanes]
        if has_extra:
            q = jnp.concatenate([q, qe_ref[0, :, lanes]], axis=-1)
            k = jnp.concatenate([k, ke_ref[0, pl.ds(k0, tk), :]], axis=-1)
        s = lax.dot_general(q, k, (((1,), (1,)), ((), ())), preferred_element_type=F32)
        if has_bias:
            s = s + kb_ref[0, h, :, pl.ds(k0, tk)]
        if diag is not None:
            row = lax.broadcasted_iota(jnp.int32, (tq, tk), 0)
            col = lax.broadcasted_iota(jnp.int32, (tq, tk), 1) + diag * tk
            s = jnp.where(row >= col, s, -jnp.inf)
        blocks = [s[:, j * LANE:(j + 1) * LANE] for j in range(tk // LANE)]
        mx = blocks[0]
        for blk in blocks[1:]:
            mx = jnp.maximum(mx, blk)
        m_prev = m_ref[h]
        m_new = jnp.maximum(m_prev, jnp.max(mx, axis=-1, keepdims=True))
        alpha = jnp.exp(m_prev - m_new)
        p = jnp.concatenate([jnp.exp(blk - m_new) for blk in blocks], axis=-1).astype(BF16)
        v_ext = jnp.concatenate([v_ref[0, pl.ds(k0, tk), lanes], ones], axis=-1)
        pv = jnp.dot(p, v_ext, preferred_element_type=F32)
        acc_ref[h, :, :HEAD_DIM] = alpha * acc_ref[h, :, :HEAD_DIM] + pv[:, :HEAD_DIM]
        acc_ref[h, :, HEAD_DIM:] = alpha * acc_ref[h, :, HEAD_DIM:] + pv[:, HEAD_DIM:]
        m_ref[h] = m_new

    def chunk(c, diag):
        k0 = pl.multiple_of(c * tk, tk)
        for h in range(N_HEADS):
            head_chunk(h, k0, diag)

    def full_chunk(c, carry):
        chunk(c, None)
        return carry

    lax.fori_loop(0, n_full, full_chunk, 0)
    for d in range(chunks_per_q):
        chunk(n_full + d, d)
    for h in range(N_HEADS):
        o_ref[0, :, h * HEAD_DIM:(h + 1) * HEAD_DIM] = (
            acc_ref[h, :, :HEAD_DIM] / acc_ref[h, :, HEAD_DIM:]).astype(o_ref.dtype)


def _softmax_attention(qm, q_blk, km, k_blk, v, v_blk, batch, seq, extra=None, key_bias=None):
    tq = _tile(seq, 512)
    tk = _tile(seq, 512)
    width = N_HEADS * HEAD_DIM
    args = [qm]
    specs = [pl.BlockSpec((1, tq, width), lambda b, i: (b, i, q_blk))]
    if extra is not None:
        args.append(extra[0])
        specs.append(pl.BlockSpec((1, tq, width), lambda b, i: (b, i, 0)))
    args.append(km)
    specs.append(pl.BlockSpec((1, seq, width), lambda b, i: (b, 0, k_blk)))
    if extra is not None:
        args.append(extra[1])
        specs.append(pl.BlockSpec((1, seq, HEAD_DIM), lambda b, i: (b, 0, 0)))
    args.append(v)
    specs.append(pl.BlockSpec((1, seq, width), lambda b, i: (b, 0, v_blk)))
    if key_bias is not None:
        args.append(key_bias)
        specs.append(pl.BlockSpec((1, N_HEADS, 1, seq), lambda b, i: (b, 0, 0, 0)))

    return pl.pallas_call(
        functools.partial(_softmax_attn_body, tq=tq, tk=tk, has_extra=extra is not None,
                          has_bias=key_bias is not None),
        grid=(batch, seq // tq),
        in_specs=specs,
        out_specs=pl.BlockSpec((1, tq, width), lambda b, i: (b, i, 0)),
        out_shape=jax.ShapeDtypeStruct((batch, seq, width), BF16),
        scratch_shapes=[
            pltpu.VMEM((N_HEADS, tq, HEAD_DIM), F32),
            pltpu.VMEM((N_HEADS, tq, 2 * HEAD_DIM), F32),
        ],
        compiler_params=_cparams("parallel", "arbitrary"),
        name="causal_softmax_attention",
    )(*args)


SB_EXP_UNDERFLOW = 110.0


def _sb_attn_body(q_ref, k_ref, v_ref, tri_ref, o_ref, r_ref, acc_ref, *, tq, tk):
    qi = pl.program_id(1)
    r_ref[...] = jnp.zeros(r_ref.shape, F32)
    acc_ref[...] = jnp.zeros(acc_ref.shape, F32)
    chunks_per_q = tq // tk
    n_full = qi * chunks_per_q

    def head_chunk(h, k0, diag):
        lanes = slice(h * HEAD_DIM, (h + 1) * HEAD_DIM)
        z = lax.dot_general(q_ref[0, :, lanes], k_ref[0, pl.ds(k0, tk), lanes],
                            (((1,), (1,)), ((), ())), preferred_element_type=F32)
        sp = jnp.maximum(z, 0.0) + jnp.log(1.0 + jnp.exp(-jnp.abs(z)))
        log_beta = z - sp
        if diag is not None:
            row = lax.broadcasted_iota(jnp.int32, (tq, tk), 0)
            col = lax.broadcasted_iota(jnp.int32, (tq, tk), 1) + diag * tk
            strict = row > col
            sp = jnp.where(strict, sp, 0.0)
        hi = sp.astype(BF16)
        lo = (sp - hi.astype(F32)).astype(BF16)
        tri = tri_ref[...]
        e =jnp.dot(hi, tri, preferred_element_type=F32) + jnp.dot(lo, tri, preferred_element_type=F32)
        r = r_ref[h]
        a = jnp.concatenate(
            [jnp.exp(log_beta[:, j * LANE:(j + 1) * LANE] - e[:, j * LANE:(j + 1) * LANE] - r)
             for j in range(tk // LANE)], axis=-1)
        if diag is not None:
            a = jnp.where(strict, a, 0.0)
        acc_ref[h] += jnp.dot(a.astype(BF16), v_ref[0, pl.ds(k0, tk), lanes],
                              preferred_element_type=F32)
        r_ref[h] = r + e[:, tk:]

    def chunk(c, diag):
        k0 = pl.multiple_of(c * tk, tk)
        for h in range(N_HEADS):
            head_chunk(h, k0, diag)

    for d in reversed(range(chunks_per_q)):
        chunk(n_full + d, d)

    def more(carry):
        i, r_min = carry
        return jnp.logical_and(i < n_full, r_min < SB_EXP_UNDERFLOW)

    def earlier_chunk(carry):
        i, _ = carry
        chunk(n_full - 1 - i, None)
        return i + 1, jnp.min(r_ref[...])

    lax.while_loop(more, earlier_chunk, (jnp.int32(0), jnp.min(r_ref[...])))
    for h in range(N_HEADS):
        o_ref[0, :, h * HEAD_DIM:(h + 1) * HEAD_DIM] = acc_ref[h].astype(o_ref.dtype)


def _sb_tri(tk):
    j_src = np.arange(tk)[:, None]
    j_dst = np.arange(tk + LANE)[None, :]
    return jnp.asarray((j_src > j_dst) | (j_dst >= tk), dtype=BF16)


def _sb_attention(qkv, q_blk, k_blk, v_blk, batch, seq):
    tq = _tile(seq, 512)
    tk = _tile(seq, 256)
    width = N_HEADS * HEAD_DIM
    return pl.pallas_call(
        functools.partial(_sb_attn_body, tq=tq, tk=tk),
        grid=(batch, seq // tq),
        in_specs=[
            pl.BlockSpec((1, tq, width), lambda b, i: (b, i, q_blk)),
            pl.BlockSpec((1, seq, width), lambda b, i: (b, 0, k_blk)),
            pl.BlockSpec((1, seq, width), lambda b, i: (b, 0, v_blk)),
            pl.BlockSpec((tk, tk + LANE), lambda b, i: (0, 0)),
        ],
        out_specs=pl.BlockSpec((1, tq, width), lambda b, i: (b, i, 0)),
        out_shape=jax.ShapeDtypeStruct((batch, seq, width), BF16),
        scratch_shapes=[
            pltpu.VMEM((N_HEADS, tq, HEAD_DIM), F32),
            pltpu.VMEM((N_HEADS, tq, HEAD_DIM), F32),
        ],
        compiler_params=_cparams("parallel", "arbitrary"),
        name="stick_breaking_attention",
    )(qkv, qkv, qkv, _sb_tri(tk))


def _pool_body(u_ref, halo_ref, w_ref, sc_ref, o_ref, ext_ref, *, ts):
    i = pl.program_id(1)
    halo = jnp.where(i > 0, halo_ref[0], 0.0)
    ext_ref[:POOL_HALO, :] = halo
    ext_ref[POOL_HALO:, :] = u_ref[0]
    pos = i * ts + lax.broadcasted_iota(jnp.int32, (ts, 1), 0)
    outs = []
    for g, win in enumerate(POOL_WINDOWS):
        lanes = slice(g * LANE, (g + 1) * LANE)
        tok = ext_ref[POOL_HALO:, lanes]
        tot = tok
        for back in range(1, win):
            tot = tot + ext_ref[POOL_HALO - back:POOL_HALO - back + ts, lanes]
        cnt = jnp.minimum(pos + 1, win).astype(F32)
        pooled = (tot / cnt - tok).astype(BF16)
        outs.append(jnp.dot(pooled, w_ref[g], preferred_element_type=F32))
    y = jnp.concatenate(outs, axis=-1) * sc_ref[...]
    o_ref[0] = y.astype(o_ref.dtype)


def _pool_mixer(latent3, col_block, w_pool, pool_scale, batch, seq):
    width = len(POOL_WINDOWS) * LANE
    ts = _tile(seq, 512)
    halo_per_tile = ts // POOL_HALO
    return pl.pallas_call(
        functools.partial(_pool_body, ts=ts),
        grid=(batch, seq // ts),
        in_specs=[
            pl.BlockSpec((1, ts, width), lambda b, i: (b, i, col_block)),
            pl.BlockSpec((1, POOL_HALO, width),
                         lambda b, i: (b, jnp.maximum(i * halo_per_tile - 1, 0), col_block)),
            pl.BlockSpec((len(POOL_WINDOWS), LANE, LANE), lambda b, i: (0, 0, 0)),
            pl.BlockSpec((1, width), lambda b, i: (0, 0)),
        ],
        out_specs=pl.BlockSpec((1, ts, width), lambda b, i: (b, i, 0)),
        out_shape=jax.ShapeDtypeStruct((batch, seq, width), BF16),
        scratch_shapes=[pltpu.VMEM((ts + POOL_HALO, width), F32)],
        compiler_params=_cparams("parallel", "arbitrary"),
        name="multiscale_pool",
    )(latent3, latent3, w_pool.astype(BF16), pool_scale.reshape(1, width))


def _merge_body(y0, y1, y2, y3, g0, g1, g2, g3, wb_ref, o_ref):
    acc = None
    for n, (y, g) in enumerate(((y0, g0), (y1, g1), (y2, g2), (y3, g3))):
        t = g[...].astype(F32) * jnp.dot(y[...], wb_ref[n], preferred_element_type=F32)
        acc = t if acc is None else acc + t
    o_ref[...] = acc.astype(o_ref.dtype)


def _merge(ys, gates, w_branch):
    n = ys[0].shape[0]
    d = w_branch.shape[2]
    tm = _tile(n, 1024)
    tn = _tile(d, 512)
    nj = d // tn
    y_spec = pl.BlockSpec((tm, BRANCH_WIDTH), lambda i, j: (i, 0))

    def g_spec(b):
        return pl.BlockSpec((tm, tn), lambda i, j: (i, b * nj + j))

    return pl.pallas_call(
        _merge_body,
        grid=(n // tm, nj),
        in_specs=[y_spec] * N_BRANCHES + [g_spec(b) for b in range(N_BRANCHES)] + [
            pl.BlockSpec((N_BRANCHES, BRANCH_WIDTH, tn), lambda i, j: (0, 0, j))],
        out_specs=pl.BlockSpec((tm, tn), lambda i, j: (i, j)),
        out_shape=jax.ShapeDtypeStruct((n, d), BF16),
        compiler_params=_cparams("parallel", "parallel"),
        name="gated_branch_merge",
    )(*ys, gates, gates, gates, gates, w_branch)


def _expert_changed(be_ref):
    blk = pl.program_id(1)
    return (blk == 0) | (be_ref[blk] != be_ref[jnp.maximum(blk - 1, 0)])


def _moe_up_body(be_ref, nb_ref, x_ref, wg_ref, wu_ref, o_ref, wg_s, wu_s):
    blk = pl.program_id(1)

    @pl.when(_expert_changed(be_ref))
    def _():
        wg_s[...] = wg_ref[0, 0].astype(BF16)
        wu_s[...] = wu_ref[0, 0].astype(BF16)

    @pl.when(blk < nb_ref[0])
    def _():
        x = x_ref[...]
        g = jnp.dot(x, wg_s[...], preferred_element_type=F32)
        u = jnp.dot(x, wu_s[...], preferred_element_type=F32)
        o_ref[...] = (g * jax.nn.sigmoid(g) * u).astype(o_ref.dtype)

    @pl.when(blk >= nb_ref[0])
    def _():
        o_ref[...] = jnp.zeros(o_ref.shape, o_ref.dtype)


def _moe_down_body(be_ref, nb_ref, a_ref, wd_ref, rw_ref, o_ref, wd_s):
    blk = pl.program_id(1)

    @pl.when(_expert_changed(be_ref))
    def _():
        wd_s[...] = wd_ref[0, 0].astype(BF16)

    @pl.when(blk < nb_ref[0])
    def _():
        y = jnp.dot(a_ref[...], wd_s[...], preferred_element_type=F32)
        o_ref[...] = (y * rw_ref[...]).astype(o_ref.dtype)

    @pl.when(blk >= nb_ref[0])
    def _():
        o_ref[...] = jnp.zeros(o_ref.shape, o_ref.dtype)


def _moe_ffn(x_buf, row_w, block_expert, n_used, w_gate, w_up, w_down, layer, tb):
    n_rows, d = x_buf.shape
    f = w_gate.shape[3]
    tf = _tile(f, 1024)
    tn = _tile(d, 2048)
    n_blocks = n_rows // tb
    up_vmem = V7X_VMEM_BYTES - 4 * 1024 * 1024

    def used(i, nb):
        return jnp.minimum(i, nb[0] - 1)

    act = pl.pallas_call(
        _moe_up_body,
        grid_spec=pltpu.PrefetchScalarGridSpec(
            num_scalar_prefetch=2,
            grid=(f // tf, n_blocks),
            in_specs=[
                pl.BlockSpec((tb, d), lambda j, i, be, nb: (used(i, nb), 0)),
                pl.BlockSpec((1, 1, d, tf), lambda j, i, be, nb: (layer, be[i], 0, j)),
                pl.BlockSpec((1, 1, d, tf), lambda j, i, be, nb: (layer, be[i], 0, j)),
            ],
            out_specs=pl.BlockSpec((tb, tf), lambda j, i, be, nb: (i, j)),
            scratch_shapes=[pltpu.VMEM((d, tf), BF16), pltpu.VMEM((d, tf), BF16)],
        ),
        out_shape=jax.ShapeDtypeStruct((n_rows, f), BF16),
        compiler_params=_cparams("arbitrary", "arbitrary", vmem=up_vmem),
        name="expert_gate_up_swiglu",
    )(block_expert, n_used, x_buf, w_gate, w_up)
    return pl.pallas_call(
        _moe_down_body,
        grid_spec=pltpu.PrefetchScalarGridSpec(
            num_scalar_prefetch=2,
            grid=(d // tn, n_blocks),
            in_specs=[
                pl.BlockSpec((tb, f), lambda j, i, be, nb: (used(i, nb), 0)),
                pl.BlockSpec((1, 1, f, tn), lambda j, i, be, nb: (layer, be[i], 0, j)),
                pl.BlockSpec((tb, 1), lambda j, i, be, nb: (used(i, nb), 0)),
            ],
            out_specs=pl.BlockSpec((tb, tn), lambda j, i, be, nb: (i, j)),
            scratch_shapes=[pltpu.VMEM((f, tn), BF16)],
        ),
        out_shape=jax.ShapeDtypeStruct((n_rows, d), BF16),
        compiler_params=_cparams("arbitrary", "arbitrary"),
        name="expert_down",
    )(block_expert, n_used, act, w_down, row_w)


def _final_norm_body(x_ref, y0_ref, y1_ref, gate_ref, g_ref, o_ref):
    x = _add_moe(x_ref, y0_ref, y1_ref, gate_ref)
    ms = jnp.mean(x * x, axis=-1, keepdims=True)
    o_ref[...] = x * lax.rsqrt(ms + NORM_EPS) * g_ref[...]


def _final_norm(x2, moe_in, g, seq):
    n, d = x2.shape
    tm = _tile(seq, 512)
    row = pl.BlockSpec((tm, d), lambda i: (i, 0))
    return pl.pallas_call(
        _final_norm_body,
        grid=(n // tm,),
        in_specs=[row] + _moe_in_specs(tm, d, seq // tm) + [pl.BlockSpec((1, d), lambda i: (0, 0))],
        out_specs=row,
        out_shape=jax.ShapeDtypeStruct((n, d), F32),
        compiler_params=_cparams("parallel"),
        name="final_rmsnorm",
    )(x2, *moe_in, g.reshape(1, d))


def _rope_tables(seq):
    half = MLA_ROPE_DIM // 2
    inv = jnp.power(ROPE_BASE, -2.0 * jnp.arange(half, dtype=F32) / MLA_ROPE_DIM)
    ang = jnp.arange(seq).astype(F32)[:, None] * inv[None, :]
    cos, sin = jnp.cos(ang), jnp.sin(ang)
    c64 = jnp.concatenate([cos, cos], axis=-1)
    s64 = jnp.concatenate([-sin, sin], axis=-1)
    return c64, s64


def _swap_halves(w):
    half = w.shape[-1] // 2
    return jnp.concatenate([w[..., half:], w[..., :half]], axis=-1)


def _pad_lanes(a, width):
    return jnp.pad(a, [(0, 0)] * (a.ndim - 1) + [(0, width - a.shape[-1])])


def _mixer_layer(x2, h, small, layer, batch, seq, prm):
    n, d = x2.shape
    (w_qkv, w_lat, w_gates, b_forget, g_q, w_q, g_kv, w_kv, w_pool, pool_scale, w_branch, w_out,
     gate1) = prm
    c64, s64 = _rope_tables(seq)

    qkv = _matmul(h, w_qkv, layer, BF16, name="fox_sb_qkv_proj").reshape(batch, seq, -1)
    latent = _matmul(h, w_lat, layer, F32, tn=640, name="latent_pool_proj")
    gates = _matmul(h, w_gates, layer, BF16, act="sigmoid", name="branch_gate_proj")

    fox_f = small[:, 2 * MLA_ROPE_DIM:2 * MLA_ROPE_DIM + N_HEADS].reshape(batch, seq, N_HEADS)
    log_f_cum = jnp.cumsum(jax.nn.log_sigmoid(fox_f + b_forget), axis=1)
    k_bias = -log_f_cum.transpose(0, 2, 1)[:, :, None, :]
    nh = N_HEADS
    y_fox = _softmax_attention(qkv, 0, qkv, 1, qkv, 2, batch, seq, key_bias=k_bias)

    y_sb = _sb_attention(qkv, 3, 4, 5, batch, seq)

    cos_t = jnp.tile(_pad_lanes(c64, HEAD_DIM), (1, nh))
    sin_t = jnp.tile(_pad_lanes(s64, HEAD_DIM), (1, nh))
    q_nope, q_rope = _mla_q(latent, 0, g_q, w_q, cos_t, sin_t, seq)
    kv = _mla_kv(latent, 4, g_kv, w_kv, seq).reshape(batch, seq, -1)
    kr = small[:, :MLA_ROPE_DIM] * jnp.tile(c64, (batch, 1)) \
        + small[:, MLA_ROPE_DIM:2 * MLA_ROPE_DIM] * jnp.tile(s64, (batch, 1))
    kr = _pad_lanes(kr.astype(BF16), HEAD_DIM).reshape(batch, seq, HEAD_DIM)
    y_mla = _softmax_attention(q_nope.reshape(batch, seq, -1), 0, kv, 0, kv, 1, batch, seq,
                               extra=(q_rope.reshape(batch, seq, -1), kr))

    y_pool = _pool_mixer(latent.reshape(batch, seq, -1), 1, w_pool, pool_scale, batch, seq)

    ys = [y.reshape(n, BRANCH_WIDTH) for y in (y_fox, y_sb, y_mla, y_pool)]
    merged = _merge(ys, gates, w_branch)
    return _matmul_residual(merged, w_out, x2, gate1, seq)


def _moe_layer(h, logits, gate2, b_rg, b_re, w_gate, w_up, w_down, layer):
    n, d = h.shape
    tb = 512
    g_logits = logits[:, :N_EXPERT_GROUPS] + b_rg
    g_idx = jnp.argmax(g_logits, axis=-1).astype(jnp.int32)[:, None]
    g_top = jnp.max(g_logits, axis=-1)
    p_group = jnp.exp(g_top - jax.nn.logsumexp(g_logits, axis=-1))
    e_logits = (logits[:, N_EXPERT_GROUPS:N_EXPERT_GROUPS + N_EXPERTS] + b_re).reshape(
        n, N_EXPERT_GROUPS, EXPERTS_PER_GROUP)
    e_logits = jnp.take_along_axis(e_logits, g_idx[:, :, None], axis=1)[:, 0]
    probs = jax.nn.softmax(e_logits, axis=-1)
    lane = jnp.arange(EXPERTS_PER_GROUP, dtype=jnp.int32)[None, :]
    i1 = jnp.argmax(probs, axis=-1).astype(jnp.int32)[:, None]
    p1 = jnp.max(probs, axis=-1, keepdims=True)
    rest = jnp.where(lane == i1, -jnp.inf, probs)
    i2 = jnp.argmax(rest, axis=-1).astype(jnp.int32)[:, None]
    p2 = jnp.max(rest, axis=-1, keepdims=True)
    top_p = jnp.concatenate([p1, p2], axis=-1)
    top_i = jnp.concatenate([i1, i2], axis=-1)
    weights = p_group[:, None] * top_p / jnp.sum(top_p, axis=-1, keepdims=True)

    expert_id = (g_idx * EXPERTS_PER_GROUP + top_i).reshape(-1).astype(jnp.int32)
    m = expert_id.shape[0]
    w_flat = weights.reshape(-1)

    order = jnp.argsort(expert_id).astype(jnp.int32)
    rank = jnp.argsort(order).astype(jnp.int32)
    onehot = expert_id[:, None] == jnp.arange(N_EXPERTS, dtype=jnp.int32)[None, :]
    counts = jnp.sum(onehot, axis=0, dtype=jnp.int32)
    padded = ((counts + tb - 1) // tb) * tb
    start = jnp.cumsum(counts) - counts
    pend = jnp.cumsum(padded)
    shift = pend - padded - start
    n_rows = m + N_EXPERTS * tb
    n_blocks = n_rows // tb
    blk = jnp.arange(n_blocks, dtype=jnp.int32)
    n_used = (pend[-1] // tb).astype(jnp.int32)
    block_expert = jnp.minimum(
        jnp.sum(pend[None, :] <= (blk * tb)[:, None], axis=1), N_EXPERTS - 1).astype(jnp.int32)
    last_used = block_expert[jnp.maximum(n_used - 1, 0)]
    block_expert = jnp.where(blk < n_used, block_expert, last_used)

    rows = jnp.arange(n_rows, dtype=jnp.int32).reshape(n_blocks, tb)
    src_sorted = rows - shift[block_expert][:, None]
    valid = (src_sorted < (start + counts)[block_expert][:, None]) & (blk < n_used)[:, None]
    src_assign = order.at[jnp.clip(src_sorted, 0, m - 1).reshape(-1)].get(mode="promise_in_bounds")
    valid = valid.reshape(-1)
    row_tok = jnp.where(valid, src_assign // EXPERT_TOP_K, rows.reshape(-1) % n)
    row_w = jnp.where(valid, w_flat.at[src_assign].get(mode="promise_in_bounds"), 0.0).astype(F32)
    dest = (rank + jnp.sum(jnp.where(onehot, shift[None, :], 0), axis=1)).reshape(n, EXPERT_TOP_K)

    x_buf = h.at[row_tok].get(mode="promise_in_bounds")
    y = _moe_ffn(x_buf, row_w.reshape(n_rows, 1), block_expert, n_used.reshape(1),
                 w_gate, w_up, w_down, layer, tb)
    return (y.at[dest[:, 0]].get(mode="promise_in_bounds"),
            y.at[dest[:, 1]].get(mode="promise_in_bounds"), gate2)


def kernel(x, c, w_mod, b_mod, g_norm1, g_norm2, w_in, b_forget, g_q_norm, w_uq, g_kv_norm, w_ukv, w_pool, pool_scale, w_branch, w_out, w_route_group, b_route_group, w_route_expert, b_route_expert, w_gate, w_up, w_down, g_final):
    batch, seq, d = x.shape
    depth = w_mod.shape[0]
    n = batch * seq
    nh = N_HEADS

    c_pad = jnp.pad(c, ((0, 8 - batch), (0, 0)))
    mod = _modulation(c_pad, w_mod, b_mod)[:, :batch]

    hq = nh * HEAD_DIM
    o_fox, o_f = 0, 3 * hq
    o_sb = o_f + nh
    o_cq = o_sb + 3 * hq
    o_ckv = o_cq + MLA_Q_LORA
    o_kr = o_ckv + MLA_KV_LORA
    o_pool = o_kr + MLA_ROPE_DIM
    o_gate = o_pool + len(POOL_WINDOWS) * LANE
    attn_scale = HEAD_DIM ** -0.5
    mla_scale = (MLA_NOPE_DIM + MLA_ROPE_DIM) ** -0.5

    half = MLA_ROPE_DIM // 2
    plans = (
        [(0, o_fox, o_fox + hq, attn_scale), (hq, o_fox + hq, o_fox + 3 * hq, 1.0),
         (3 * hq, o_sb, o_sb + hq, attn_scale), (4 * hq, o_sb + hq, o_sb + 3 * hq, 1.0)],
        [(0, o_cq, o_ckv, 1.0), (MLA_Q_LORA, o_pool, o_gate, 1.0),
         (MLA_Q_LORA + o_gate - o_pool, o_ckv, o_kr, 1.0)],
        [(0, o_gate, w_in.shape[2], 1.0)],
        [(0, None, SMALL_WIDTH, 0.0), (0, o_kr, o_pool, 1.0), (MLA_ROPE_DIM, o_kr + half, o_pool, 1.0),
         (MLA_ROPE_DIM + half, o_kr, o_kr + half, 1.0), (2 * MLA_ROPE_DIM, o_f, o_sb, 1.0)],
    )
    w_qkv, w_lat, w_gates, w_small1 = _regroup_w_in(
        w_in, plans, (6 * hq, o_kr - o_cq + o_gate - o_pool, w_in.shape[2] - o_gate, SMALL_WIDTH))

    x2 = x.reshape(n, d)
    pending = None
    for l in range(depth):
        m6 = mod[l].reshape(batch, 6, 1, d)
        shift1, scale1, gate1, shift2, scale2, gate2 = [m6[:, t] for t in range(6)]

        wq = w_uq[l].reshape(MLA_Q_LORA, nh, MLA_NOPE_DIM + MLA_ROPE_DIM) * mla_scale
        wq_rope = wq[:, :, MLA_NOPE_DIM:]
        w_q = jnp.concatenate([
            wq[:, :, :MLA_NOPE_DIM].reshape(MLA_Q_LORA, -1),
            _pad_lanes(wq_rope, HEAD_DIM).reshape(MLA_Q_LORA, -1),
            _pad_lanes(_swap_halves(wq_rope), HEAD_DIM).reshape(MLA_Q_LORA, -1)], axis=1).astype(BF16)
        wkv = w_ukv[l].reshape(MLA_KV_LORA, nh, 2 * HEAD_DIM)
        w_kv = jnp.concatenate([wkv[:, :, :HEAD_DIM].reshape(MLA_KV_LORA, -1),
                                wkv[:, :, HEAD_DIM:].reshape(MLA_KV_LORA, -1)], axis=1).astype(BF16)

        x2, h, small = _norm_small(x2, g_norm1[l], scale1, shift1, w_small1[l], seq,
                                   moe_in=pending)
        prm = (w_qkv, w_lat, w_gates, b_forget[l], g_q_norm[l], w_q, g_kv_norm[l], w_kv,
               w_pool[l], pool_scale[l], w_branch[l].astype(BF16), w_out[l].astype(BF16), gate1)
        x2 = _mixer_layer(x2, h, small, l, batch, seq, prm)

        w_small2 = _pad_lanes(jnp.concatenate([w_route_group[l], w_route_expert[l]], axis=1),
                              SMALL_WIDTH).astype(BF16)
        _, h, logits = _norm_small(x2, g_norm2[l], scale2, shift2, w_small2, seq)
        pending = _moe_layer(h, logits, gate2, b_route_group[l], b_route_expert[l],
                             w_gate, w_up, w_down, l)
    return _final_norm(x2, pending, g_final, seq).reshape(batch, seq, d)
```

```python
import functools

import numpy as np
import jax
import jax.numpy as jnp
from jax import lax
from jax.experimental import pallas as pl
from jax.experimental.pallas import tpu as pltpu

F32 = jnp.float32
BF16 = jnp.bfloat16

NORM_EPS = 1e-6
N_HEADS = 4
HEAD_DIM = 128
MLA_Q_LORA = 512
MLA_KV_LORA = 256
MLA_NOPE_DIM = 128
MLA_ROPE_DIM = 64
ROPE_BASE = 10000.0
POOL_WINDOWS = (2, 4, 8, 16)
POOL_HALO = 16
N_BRANCHES = 4
BRANCH_WIDTH = 512
N_EXPERT_GROUPS = 4
EXPERTS_PER_GROUP = 8
N_EXPERTS = N_EXPERT_GROUPS * EXPERTS_PER_GROUP
EXPERT_TOP_K = 2
SMALL_WIDTH = 256

LANE = 128
V7X_VMEM_BYTES = 64 * 1024 * 1024
VMEM_LIMIT = V7X_VMEM_BYTES - 12 * 1024 * 1024


def _cparams(*sem, vmem=VMEM_LIMIT):
    return pltpu.CompilerParams(dimension_semantics=sem, vmem_limit_bytes=vmem)


def _tile(n, pref):
    t = min(n, pref)
    assert n % t == 0, (n, pref)
    return t


def _mod_body(c_ref, w_ref, b_ref, o_ref):
    c = c_ref[...]
    ca = c * jax.nn.sigmoid(c)
    acc = jnp.dot(ca.astype(BF16), w_ref[0].astype(BF16), preferred_element_type=F32)
    o_ref[0] = acc + b_ref[0]


def _modulation(c_pad, w_mod, b_mod):
    depth, d, n = w_mod.shape
    tn = _tile(n, 1024)
    return pl.pallas_call(
        _mod_body,
        grid=(depth, n // tn),
        in_specs=[
            pl.BlockSpec((8, d), lambda l, j: (0, 0)),
            pl.BlockSpec((1, d, tn), lambda l, j: (l, 0, j)),
            pl.BlockSpec((1, 1, tn), lambda l, j: (l, 0, j)),
        ],
        out_specs=pl.BlockSpec((1, 8, tn), lambda l, j: (l, 0, j)),
        out_shape=jax.ShapeDtypeStruct((depth, 8, n), F32),
        compiler_params=_cparams("parallel", "parallel"),
        name="adaln_modulation",
    )(c_pad, w_mod, b_mod.reshape(depth, 1, n))


REGROUP_CHUNK = 1024


def _regroup_body(w_ref, *out_refs, plans):
    total = w_ref.shape[2]

    def cols(a, b):
        a0 = (a // LANE) * LANE
        b1 = min(-(-b // LANE) * LANE, total)
        return w_ref[0, :, a0:b1][:, a - a0:b - a0]

    for o_ref, plan in zip(out_refs, plans):
        for dst, a, b, scale in plan:
            if a is None:
                o_ref[0, :, dst:b] = jnp.zeros((o_ref.shape[1], b - dst), o_ref.dtype)
                continue
            for c0 in range(0, b - a, REGROUP_CHUNK):
                c1 = min(c0 + REGROUP_CHUNK, b - a)
                v = cols(a + c0, a + c1)
                if scale != 1.0:
                    v = v * scale
                o_ref[0, :, dst + c0:dst + c1] = v.astype(o_ref.dtype)


def _regroup_w_in(w_in, plans, widths):
    depth, d, total = w_in.shape
    tr = _tile(d, 256)
    return pl.pallas_call(
        functools.partial(_regroup_body, plans=plans),
        grid=(depth, d // tr),
        in_specs=[pl.BlockSpec((1, tr, total), lambda l, i: (l, i, 0))],
        out_specs=[pl.BlockSpec((1, tr, w), lambda l, i: (l, i, 0)) for w in widths],
        out_shape=[jax.ShapeDtypeStruct((depth, d, w), BF16) for w in widths],
        compiler_params=_cparams("parallel", "parallel"),
        name="regroup_w_in",
    )(w_in)


def _add_moe(x_ref, y0_ref, y1_ref, gate_ref):
    return x_ref[...] + gate_ref[0] * (y0_ref[...].astype(F32) + y1_ref[...].astype(F32))


def _moe_in_specs(tm, d, per_b):
    row = pl.BlockSpec((tm, d), lambda i: (i, 0))
    return [row, row, pl.BlockSpec((1, 1, d), lambda i: (i // per_b, 0, 0))]


def _norm_small_body(*refs, has_moe):
    if has_moe:
        x_ref, y0_ref, y1_ref, gate_ref, g_ref, sc_ref, sh_ref, w_ref, xo_ref, h_ref, s_ref = refs
        x = _add_moe(x_ref, y0_ref, y1_ref, gate_ref)
        xo_ref[...] = x
    else:
        x_ref, g_ref, sc_ref, sh_ref, w_ref, h_ref, s_ref = refs
        x = x_ref[...]
    ms = jnp.mean(x * x, axis=-1, keepdims=True)
    y = x * lax.rsqrt(ms + NORM_EPS) * g_ref[...]
    h = y * (1.0 + sc_ref[0]) + sh_ref[0]
    h_ref[...] = h.astype(h_ref.dtype)
    s_ref[...] = jnp.dot(h.astype(BF16), w_ref[...], preferred_element_type=F32)


def _norm_small(x2, g, scale, shift, w_small, seq, moe_in=None):
    n, d = x2.shape
    tm = _tile(seq, 512)
    per_b = seq // tm
    ws = w_small.shape[1]
    row = pl.BlockSpec((tm, d), lambda i: (i, 0))
    has_moe = moe_in is not None
    outs = pl.pallas_call(
        functools.partial(_norm_small_body, has_moe=has_moe),
        grid=(n // tm,),
        in_specs=[row] + (_moe_in_specs(tm, d, per_b) if has_moe else []) + [
            pl.BlockSpec((1, d), lambda i: (0, 0)),
            pl.BlockSpec((1, 1, d), lambda i: (i // per_b, 0, 0)),
            pl.BlockSpec((1, 1, d), lambda i: (i // per_b, 0, 0)),
            pl.BlockSpec((d, ws), lambda i: (0, 0)),
        ],
        out_specs=([row] if has_moe else []) + [row, pl.BlockSpec((tm, ws), lambda i: (i, 0))],
        out_shape=([jax.ShapeDtypeStruct((n, d), F32)] if has_moe else []) + [
            jax.ShapeDtypeStruct((n, d), BF16),
            jax.ShapeDtypeStruct((n, ws), F32),
        ],
        compiler_params=_cparams("parallel"),
        name="norm_modulate_small_proj",
    )(x2, *(moe_in if has_moe else ()), g.reshape(1, d), scale, shift, w_small)
    return outs if has_moe else (x2, *outs)


def _mm_body(a_ref, w_ref, o_ref):
    o_ref[...] = jnp.dot(a_ref[...], w_ref[0], preferred_element_type=F32).astype(o_ref.dtype)


def _matmul(a, w, layer, out_dtype, tm=1024, tn=1024, name="matmul"):
    m, k = a.shape
    n = w.shape[2]
    tm = _tile(m, tm)
    tn = _tile(n, tn)
    return pl.pallas_call(
        _mm_body,
        grid=(m // tm, n // tn),
        in_specs=[
            pl.BlockSpec((tm, k), lambda i, j: (i, 0)),
            pl.BlockSpec((1, k, tn), lambda i, j: (layer, 0, j)),
        ],
        out_specs=pl.BlockSpec((tm, tn), lambda i, j: (i, j)),
        out_shape=jax.ShapeDtypeStruct((m, n), out_dtype),
        compiler_params=_cparams("parallel", "parallel"),
        name=name,
    )(a, w)


def _mm_residual_body(a_ref, w_ref, x_ref, gate_ref, o_ref):
    acc = jnp.dot(a_ref[...], w_ref[...], preferred_element_type=F32)
    o_ref[...] = x_ref[...] + gate_ref[0] * acc


def _matmul_residual(a, w, x2, gate, seq, tm=1024, tn=1024):
    m, k = a.shape
    n = w.shape[1]
    tm = _tile(seq, tm)
    tn = _tile(n, tn)
    per_b = seq // tm
    return pl.pallas_call(
        _mm_residual_body,
        grid=(m // tm, n // tn),
        in_specs=[
            pl.BlockSpec((tm, k), lambda i, j: (i, 0)),
            pl.BlockSpec((k, tn), lambda i, j: (0, j)),
            pl.BlockSpec((tm, tn), lambda i, j: (i, j)),
            pl.BlockSpec((1, 1, tn), lambda i, j: (i // per_b, 0, j)),
        ],
        out_specs=pl.BlockSpec((tm, tn), lambda i, j: (i, j)),
        out_shape=jax.ShapeDtypeStruct((m, n), F32),
        compiler_params=_cparams("parallel", "parallel"),
        name="out_proj_residual",
    )(a, w, x2, gate)


def _latent_norm(a_ref, g_ref):
    a = a_ref[...]
    ms = jnp.mean(a * a, axis=-1, keepdims=True)
    return (a * lax.rsqrt(ms + NORM_EPS) * g_ref[...]).astype(BF16)


def _mla_q_body(a_ref, g_ref, w_ref, cos_ref, sin_ref, nope_ref, rope_ref):
    acc = jnp.dot(_latent_norm(a_ref, g_ref), w_ref[...], preferred_element_type=F32)
    w = nope_ref.shape[1]
    nope_ref[...] = acc[:, :w].astype(BF16)
    rope_ref[...] = (acc[:, w:2 * w] * cos_ref[...] + acc[:, 2 * w:] * sin_ref[...]).astype(BF16)


def _mla_q(latent, col_block, g, w_q, cos_t, sin_t, seq):
    n = latent.shape[0]
    k = w_q.shape[0]
    w = w_q.shape[1] // 3
    tm = _tile(seq, 1024)
    per_b = seq // tm
    return pl.pallas_call(
        _mla_q_body,
        grid=(n // tm,),
        in_specs=[
            pl.BlockSpec((tm, k), lambda i: (i, col_block)),
            pl.BlockSpec((1, k), lambda i: (0, 0)),
            pl.BlockSpec((k, 3 * w), lambda i: (0, 0)),
            pl.BlockSpec((tm, w), lambda i: (i % per_b, 0)),
            pl.BlockSpec((tm, w), lambda i: (i % per_b, 0)),
        ],
        out_specs=[
            pl.BlockSpec((tm, w), lambda i: (i, 0)),
            pl.BlockSpec((tm, w), lambda i: (i, 0)),
        ],
        out_shape=[
            jax.ShapeDtypeStruct((n, w), BF16),
            jax.ShapeDtypeStruct((n, w), BF16),
        ],
        compiler_params=_cparams("parallel"),
        name="mla_q_up_rope",
    )(latent, g.reshape(1, k), w_q, cos_t, sin_t)


def _mla_kv_body(a_ref, g_ref, w_ref, o_ref):
    o_ref[...] = jnp.dot(_latent_norm(a_ref, g_ref), w_ref[...],
                         preferred_element_type=F32).astype(BF16)


def _mla_kv(latent, col_block, g, w_kv, seq):
    n = latent.shape[0]
    k, nout = w_kv.shape
    tm = _tile(seq, 1024)
    return pl.pallas_call(
        _mla_kv_body,
        grid=(n // tm,),
        in_specs=[
            pl.BlockSpec((tm, k), lambda i: (i, col_block)),
            pl.BlockSpec((1, k), lambda i: (0, 0)),
            pl.BlockSpec((k, nout), lambda i: (0, 0)),
        ],
        out_specs=pl.BlockSpec((tm, nout), lambda i: (i, 0)),
        out_shape=jax.ShapeDtypeStruct((n, nout), BF16),
        compiler_params=_cparams("parallel"),
        name="mla_kv_up",
    )(latent, g.reshape(1, k), w_kv)


def _softmax_attn_body(*refs, tq, tk, has_extra, has_bias):
    refs = list(refs)
    qm_ref = refs.pop(0)
    qe_ref = refs.pop(0) if has_extra else None
    km_ref = refs.pop(0)
    ke_ref = refs.pop(0) if has_extra else None
    v_ref = refs.pop(0)
    kb_ref = refs.pop(0) if has_bias else None
    o_ref, m_ref, acc_ref = refs
    qi = pl.program_id(1)
    m_ref[...] = jnp.full(m_ref.shape, -jnp.inf, F32)
    acc_ref[...] = jnp.zeros(acc_ref.shape, F32)
    chunks_per_q = tq // tk
    n_full = qi * chunks_per_q
    ones = jnp.ones((tk, HEAD_DIM), BF16)

    def head_chunk(h, k0, diag):
        lanes = slice(h * HEAD_DIM, (h + 1) * HEAD_DIM)
        q = qm_ref[0, :, lanes]
        k = km_ref[0, pl.ds(k0, tk), lanes]
        if has_extra:
            q = jnp.concatenate([q, qe_ref[0, :, lanes]], axis=-1)
            k = jnp.concatenate([k, ke_ref[0, pl.ds(k0, tk), :]], axis=-1)
        s = lax.dot_general(q, k, (((1,), (1,)), ((), ())), preferred_element_type=F32)
        if has_bias:
            s = s + kb_ref[0, h, :, pl.ds(k0, tk)]
        if diag is not None:
            row = lax.broadcasted_iota(jnp.int32, (tq, tk), 0)
            col = lax.broadcasted_iota(jnp.int32, (tq, tk), 1) + diag * tk
            s = jnp.where(row >= col, s, -jnp.inf)
        blocks = [s[:, j * LANE:(j + 1) * LANE] for j in range(tk // LANE)]
        mx = blocks[0]
        for blk in blocks[1:]:
            mx = jnp.maximum(mx, blk)
        m_prev = m_ref[h]
        m_new = jnp.maximum(m_prev, jnp.max(mx, axis=-1, keepdims=True))
        alpha = jnp.exp(m_prev - m_new)
        p = jnp.concatenate([jnp.exp(blk - m_new) for blk in blocks], axis=-1).astype(BF16)
        v_ext = jnp.concatenate([v_ref[0, pl.ds(k0, tk), lanes], ones], axis=-1)
        pv = jnp.dot(p, v_ext, preferred_element_type=F32)
        acc_ref[h, :, :HEAD_DIM] = alpha * acc_ref[h, :, :HEAD_DIM] + pv[:, :HEAD_DIM]
        acc_ref[h, :, HEAD_DIM:] = alpha * acc_ref[h, :, HEAD_DIM:] + pv[:, HEAD_DIM:]
        m_ref[h] = m_new

    def chunk(c, diag):
        k0 = pl.multiple_of(c * tk, tk)
        for h in range(N_HEADS):
            head_chunk(h, k0, diag)

    def full_chunk(c, carry):
        chunk(c, None)
        return carry

    lax.fori_loop(0, n_full, full_chunk, 0)
    for d in range(chunks_per_q):
        chunk(n_full + d, d)
    for h in range(N_HEADS):
        o_ref[0, :, h * HEAD_DIM:(h + 1) * HEAD_DIM] = (
            acc_ref[h, :, :HEAD_DIM] / acc_ref[h, :, HEAD_DIM:]).astype(o_ref.dtype)


def _softmax_attention(qm, q_blk, km, k_blk, v, v_blk, batch, seq, extra=None, key_bias=None):
    tq = _tile(seq, 512)
    tk = _tile(seq, 512)
    width = N_HEADS * HEAD_DIM
    args = [qm]
    specs = [pl.BlockSpec((1, tq, width), lambda b, i: (b, i, q_blk))]
    if extra is not None:
        args.append(extra[0])
        specs.append(pl.BlockSpec((1, tq, width), lambda b, i: (b, i, 0)))
    args.append(km)
    specs.append(pl.BlockSpec((1, seq, width), lambda b, i: (b, 0, k_blk)))
    if extra is not None:
        args.append(extra[1])
        specs.append(pl.BlockSpec((1, seq, HEAD_DIM), lambda b, i: (b, 0, 0)))
    args.append(v)
    specs.append(pl.BlockSpec((1, seq, width), lambda b, i: (b, 0, v_blk)))
    if key_bias is not None:
        args.append(key_bias)
        specs.append(pl.BlockSpec((1, N_HEADS, 1, seq), lambda b, i: (b, 0, 0, 0)))

    return pl.pallas_call(
        functools.partial(_softmax_attn_body, tq=tq, tk=tk, has_extra=extra is not None,
                          has_bias=key_bias is not None),
        grid=(batch, seq // tq),
        in_specs=specs,
        out_specs=pl.BlockSpec((1, tq, width), lambda b, i: (b, i, 0)),
        out_shape=jax.ShapeDtypeStruct((batch, seq, width), BF16),
        scratch_shapes=[
            pltpu.VMEM((N_HEADS, tq, HEAD_DIM), F32),
            pltpu.VMEM((N_HEADS, tq, 2 * HEAD_DIM), F32),
        ],
        compiler_params=_cparams("parallel", "arbitrary"),
        name="causal_softmax_attention",
    )(*args)


SB_EXP_UNDERFLOW = 110.0


def _sb_attn_body(q_ref, k_ref, v_ref, tri_ref, o_ref, r_ref, acc_ref, *, tq, tk):
    qi = pl.program_id(1)
    r_ref[...] = jnp.zeros(r_ref.shape, F32)
    acc_ref[...] = jnp.zeros(acc_ref.shape, F32)
    chunks_per_q = tq // tk
    n_full = qi * chunks_per_q

    def head_chunk(h, k0, diag):
        lanes = slice(h * HEAD_DIM, (h + 1) * HEAD_DIM)
        z = lax.dot_general(q_ref[0, :, lanes], k_ref[0, pl.ds(k0, tk), lanes],
                            (((1,), (1,)), ((), ())), preferred_element_type=F32)
        sp = jnp.maximum(z, 0.0) + jnp.log(1.0 + jnp.exp(-jnp.abs(z)))
        log_beta = z - sp
        if diag is not None:
            row = lax.broadcasted_iota(jnp.int32, (tq, tk), 0)
            col = lax.broadcasted_iota(jnp.int32, (tq, tk), 1) + diag * tk
            strict = row > col
            sp = jnp.where(strict, sp, 0.0)
        hi = sp.astype(BF16)
        lo = (sp - hi.astype(F32)).astype(BF16)
        tri = tri_ref[...]
        e =jnp.dot(hi, tri, preferred_element_type=F32) + jnp.dot(lo, tri, preferred_element_type=F32)
        r = r_ref[h]
        a = jnp.concatenate(
            [jnp.exp(log_beta[:, j * LANE:(j + 1) * LANE] - e[:, j * LANE:(j + 1) * LANE] - r)
             for j in range(tk // LANE)], axis=-1)
        if diag is not None:
            a = jnp.where(strict, a, 0.0)
        acc_ref[h] += jnp.dot(a.astype(BF16), v_ref[0, pl.ds(k0, tk), lanes],
                              preferred_element_type=F32)
        r_ref[h] = r + e[:, tk:]

    def chunk(c, diag):
        k0 = pl.multiple_of(c * tk, tk)
        for h in range(N_HEADS):
            head_chunk(h, k0, diag)

    for d in reversed(range(chunks_per_q)):
        chunk(n_full + d, d)

    def more(carry):
        i, r_min = carry
        return jnp.logical_and(i < n_full, r_min < SB_EXP_UNDERFLOW)

    def earlier_chunk(carry):
        i, _ = carry
        chunk(n_full - 1 - i, None)
        return i + 1, jnp.min(r_ref[...])

    lax.while_loop(more, earlier_chunk, (jnp.int32(0), jnp.min(r_ref[...])))
    for h in range(N_HEADS):
        o_ref[0, :, h * HEAD_DIM:(h + 1) * HEAD_DIM] = acc_ref[h].astype(o_ref.dtype)


def _sb_tri(tk):
    j_src = np.arange(tk)[:, None]
    j_dst = np.arange(tk + LANE)[None, :]
    return jnp.asarray((j_src > j_dst) | (j_dst >= tk), dtype=BF16)


def _sb_attention(qkv, q_blk, k_blk, v_blk, batch, seq):
    tq = _tile(seq, 512)
    tk = _tile(seq, 256)
    width = N_HEADS * HEAD_DIM
    return pl.pallas_call(
        functools.partial(_sb_attn_body, tq=tq, tk=tk),
        grid=(batch, seq // tq),
        in_specs=[
            pl.BlockSpec((1, tq, width), lambda b, i: (b, i, q_blk)),
            pl.BlockSpec((1, seq, width), lambda b, i: (b, 0, k_blk)),
            pl.BlockSpec((1, seq, width), lambda b, i: (b, 0, v_blk)),
            pl.BlockSpec((tk, tk + LANE), lambda b, i: (0, 0)),
        ],
        out_specs=pl.BlockSpec((1, tq, width), lambda b, i: (b, i, 0)),
        out_shape=jax.ShapeDtypeStruct((batch, seq, width), BF16),
        scratch_shapes=[
            pltpu.VMEM((N_HEADS, tq, HEAD_DIM), F32),
            pltpu.VMEM((N_HEADS, tq, HEAD_DIM), F32),
        ],
        compiler_params=_cparams("parallel", "arbitrary"),
        name="stick_breaking_attention",
    )(qkv, qkv, qkv, _sb_tri(tk))


def _pool_body(u_ref, halo_ref, w_ref, sc_ref, o_ref, ext_ref, *, ts):
    i = pl.program_id(1)
    halo = jnp.where(i > 0, halo_ref[0], 0.0)
    ext_ref[:POOL_HALO, :] = halo
    ext_ref[POOL_HALO:, :] = u_ref[0]
    pos = i * ts + lax.broadcasted_iota(jnp.int32, (ts, 1), 0)
    outs = []
    for g, win in enumerate(POOL_WINDOWS):
        lanes = slice(g * LANE, (g + 1) * LANE)
        tok = ext_ref[POOL_HALO:, lanes]
        tot = tok
        for back in range(1, win):
            tot = tot + ext_ref[POOL_HALO - back:POOL_HALO - back + ts, lanes]
        cnt = jnp.minimum(pos + 1, win).astype(F32)
        pooled = (tot / cnt - tok).astype(BF16)
        outs.append(jnp.dot(pooled, w_ref[g], preferred_element_type=F32))
    y = jnp.concatenate(outs, axis=-1) * sc_ref[...]
    o_ref[0] = y.astype(o_ref.dtype)


def _pool_mixer(latent3, col_block, w_pool, pool_scale, batch, seq):
    width = len(POOL_WINDOWS) * LANE
    ts = _tile(seq, 512)
    halo_per_tile = ts // POOL_HALO
    return pl.pallas_call(
        functools.partial(_pool_body, ts=ts),
        grid=(batch, seq // ts),
        in_specs=[
            pl.BlockSpec((1, ts, width), lambda b, i: (b, i, col_block)),
            pl.BlockSpec((1, POOL_HALO, width),
                         lambda b, i: (b, jnp.maximum(i * halo_per_tile - 1, 0), col_block)),
            pl.BlockSpec((len(POOL_WINDOWS), LANE, LANE), lambda b, i: (0, 0, 0)),
            pl.BlockSpec((1, width), lambda b, i: (0, 0)),
        ],
        out_specs=pl.BlockSpec((1, ts, width), lambda b, i: (b, i, 0)),
        out_shape=jax.ShapeDtypeStruct((batch, seq, width), BF16),
        scratch_shapes=[pltpu.VMEM((ts + POOL_HALO, width), F32)],
        compiler_params=_cparams("parallel", "arbitrary"),
        name="multiscale_pool",
    )(latent3, latent3, w_pool.astype(BF16), pool_scale.reshape(1, width))


def _merge_body(h_ref, y0, y1, y2, y3, g0, g1, g2, g3, wb_ref, o_ref):
    h = h_ref[...]
    acc = None
    for n, (y, wg) in enumerate(((y0, g0), (y1, g1), (y2, g2), (y3, g3))):
        gate = jax.nn.sigmoid(jnp.dot(h, wg[0], preferred_element_type=F32))
        t = gate * jnp.dot(y[...], wb_ref[n], preferred_element_type=F32)
        acc = t if acc is None else acc + t
    o_ref[...] = acc.astype(o_ref.dtype)


def _merge(h, ys, w_gates, layer, w_branch):
    n, k = h.shape
    d = w_branch.shape[2]
    tm = _tile(n, 1024)
    tn = _tile(d, 512)
    nj = d // tn
    y_spec = pl.BlockSpec((tm, BRANCH_WIDTH), lambda i, j: (i, 0))

    def g_spec(b):
        return pl.BlockSpec((1, k, tn), lambda i, j: (layer, 0, b * nj + j))

    return pl.pallas_call(
        _merge_body,
        grid=(n // tm, nj),
        in_specs=[pl.BlockSpec((tm, k), lambda i, j: (i, 0))] + [y_spec] * N_BRANCHES
        + [g_spec(b) for b in range(N_BRANCHES)]
        + [pl.BlockSpec((N_BRANCHES, BRANCH_WIDTH, tn), lambda i, j: (0, 0, j))],
        out_specs=pl.BlockSpec((tm, tn), lambda i, j: (i, j)),
        out_shape=jax.ShapeDtypeStruct((n, d), BF16),
        compiler_params=_cparams("parallel", "parallel"),
        name="gated_branch_merge",
    )(h, *ys, *([w_gates] * N_BRANCHES), w_branch)


def _expert_changed(be_ref):
    blk = pl.program_id(1)
    return (blk == 0) | (be_ref[blk] != be_ref[jnp.maximum(blk - 1, 0)])


WEIGHT_STREAMS = 4


def _cast_slabs(slab_refs, dst_ref):
    rows = dst_ref.shape[0] // len(slab_refs)
    for p, ref in enumerate(slab_refs):
        dst_ref[p * rows:(p + 1) * rows, :] = ref[0, 0].astype(BF16)


def _moe_up_body(be_ref, nb_ref, x_ref, *refs):
    wg_refs, wu_refs = refs[:WEIGHT_STREAMS], refs[WEIGHT_STREAMS:2 * WEIGHT_STREAMS]
    o_ref, wg_s, wu_s = refs[2 * WEIGHT_STREAMS:]
    blk = pl.program_id(1)

    @pl.when(_expert_changed(be_ref))
    def _():
        _cast_slabs(wg_refs, wg_s)
        _cast_slabs(wu_refs, wu_s)

    @pl.when(blk < nb_ref[0])
    def _():
        x = x_ref[...]
        g = jnp.dot(x, wg_s[...], preferred_element_type=F32)
        u = jnp.dot(x, wu_s[...], preferred_element_type=F32)
        o_ref[...] = (g * jax.nn.sigmoid(g) * u).astype(o_ref.dtype)

    @pl.when(blk >= nb_ref[0])
    def _():
        o_ref[...] = jnp.zeros(o_ref.shape, o_ref.dtype)


def _moe_down_body(be_ref, nb_ref, a_ref, *refs):
    wd_refs = refs[:WEIGHT_STREAMS]
    rw_ref, o_ref, wd_s = refs[WEIGHT_STREAMS:]
    blk = pl.program_id(1)

    @pl.when(_expert_changed(be_ref))
    def _():
        _cast_slabs(wd_refs, wd_s)

    @pl.when(blk < nb_ref[0])
    def _():
        y = jnp.dot(a_ref[...], wd_s[...], preferred_element_type=F32)
        o_ref[...] = (y * rw_ref[...]).astype(o_ref.dtype)

    @pl.when(blk >= nb_ref[0])
    def _():
        o_ref[...] = jnp.zeros(o_ref.shape, o_ref.dtype)


def _moe_ffn(x_buf, row_w, block_expert, n_used, w_gate, w_up, w_down, layer, tb):
    n_rows, d = x_buf.shape
    f = w_gate.shape[3]
    tf = _tile(f, 1024)
    tn = _tile(d, 2048)
    n_blocks = n_rows // tb
    up_vmem = V7X_VMEM_BYTES - 4 * 1024 * 1024

    def used(i, nb):
        return jnp.minimum(i, nb[0] - 1)

    def slabs(rows, cols):
        return [pl.BlockSpec((1, 1, rows // WEIGHT_STREAMS, cols),
                             lambda j, i, be, nb, p=p: (layer, be[i], p, j))
                for p in range(WEIGHT_STREAMS)]

    act = pl.pallas_call(
        _moe_up_body,
        grid_spec=pltpu.PrefetchScalarGridSpec(
            num_scalar_prefetch=2,
            grid=(f // tf, n_blocks),
            in_specs=[pl.BlockSpec((tb, d), lambda j, i, be, nb: (used(i, nb), 0))]
            + slabs(d, tf) + slabs(d, tf),
            out_specs=pl.BlockSpec((tb, tf), lambda j, i, be, nb: (i, j)),
            scratch_shapes=[pltpu.VMEM((d, tf), BF16), pltpu.VMEM((d, tf), BF16)],
        ),
        out_shape=jax.ShapeDtypeStruct((n_rows, f), BF16),
        compiler_params=_cparams("arbitrary", "arbitrary", vmem=up_vmem),
        name="expert_gate_up_swiglu",
    )(block_expert, n_used, x_buf, *([w_gate] * WEIGHT_STREAMS), *([w_up] * WEIGHT_STREAMS))
    return pl.pallas_call(
        _moe_down_body,
        grid_spec=pltpu.PrefetchScalarGridSpec(
            num_scalar_prefetch=2,
            grid=(d // tn, n_blocks),
            in_specs=[pl.BlockSpec((tb, f), lambda j, i, be, nb: (used(i, nb), 0))]
            + slabs(f, tn)
            + [pl.BlockSpec((tb, 1), lambda j, i, be, nb: (used(i, nb), 0))],
            out_specs=pl.BlockSpec((tb, tn), lambda j, i, be, nb: (i, j)),
            scratch_shapes=[pltpu.VMEM((f, tn), BF16)],
        ),
        out_shape=jax.ShapeDtypeStruct((n_rows, d), BF16),
        compiler_params=_cparams("arbitrary", "arbitrary"),
        name="expert_down",
    )(block_expert, n_used, act, *([w_down] * WEIGHT_STREAMS), row_w)


def _final_norm_body(x_ref, y0_ref, y1_ref, gate_ref, g_ref, o_ref):
    x = _add_moe(x_ref, y0_ref, y1_ref, gate_ref)
    ms = jnp.mean(x * x, axis=-1, keepdims=True)
    o_ref[...] = x * lax.rsqrt(ms + NORM_EPS) * g_ref[...]


def _final_norm(x2, moe_in, g, seq):
    n, d = x2.shape
    tm = _tile(seq, 512)
    row = pl.BlockSpec((tm, d), lambda i: (i, 0))
    return pl.pallas_call(
        _final_norm_body,
        grid=(n // tm,),
        in_specs=[row] + _moe_in_specs(tm, d, seq // tm) + [pl.BlockSpec((1, d), lambda i: (0, 0))],
        out_specs=row,
        out_shape=jax.ShapeDtypeStruct((n, d), F32),
        compiler_params=_cparams("parallel"),
        name="final_rmsnorm",
    )(x2, *moe_in, g.reshape(1, d))


def _rope_tables(seq):
    half = MLA_ROPE_DIM // 2
    inv = jnp.power(ROPE_BASE, -2.0 * jnp.arange(half, dtype=F32) / MLA_ROPE_DIM)
    ang = jnp.arange(seq).astype(F32)[:, None] * inv[None, :]
    cos, sin = jnp.cos(ang), jnp.sin(ang)
    c64 = jnp.concatenate([cos, cos], axis=-1)
    s64 = jnp.concatenate([-sin, sin], axis=-1)
    return c64, s64


def _swap_halves(w):
    half = w.shape[-1] // 2
    return jnp.concatenate([w[..., half:], w[..., :half]], axis=-1)


def _pad_lanes(a, width):
    return jnp.pad(a, [(0, 0)] * (a.ndim - 1) + [(0, width - a.shape[-1])])


def _mixer_layer(x2, h, small, layer, batch, seq, prm):
    n, d = x2.shape
    (w_qkv, w_lat, w_gates, b_forget, g_q, w_q, g_kv, w_kv, w_pool, pool_scale, w_branch, w_out,
     gate1) = prm
    c64, s64 = _rope_tables(seq)

    qkv = _matmul(h, w_qkv, layer, BF16, name="fox_sb_qkv_proj").reshape(batch, seq, -1)
    latent = _matmul(h, w_lat, layer, F32, tn=640, name="latent_pool_proj")

    fox_f = small[:, 2 * MLA_ROPE_DIM:2 * MLA_ROPE_DIM + N_HEADS].reshape(batch, seq, N_HEADS)
    log_f_cum = jnp.cumsum(jax.nn.log_sigmoid(fox_f + b_forget), axis=1)
    k_bias = -log_f_cum.transpose(0, 2, 1)[:, :, None, :]
    nh = N_HEADS
    y_fox = _softmax_attention(qkv, 0, qkv, 1, qkv, 2, batch, seq, key_bias=k_bias)

    y_sb = _sb_attention(qkv, 3, 4, 5, batch, seq)

    cos_t = jnp.tile(_pad_lanes(c64, HEAD_DIM), (1, nh))
    sin_t = jnp.tile(_pad_lanes(s64, HEAD_DIM), (1, nh))
    q_nope, q_rope = _mla_q(latent, 0, g_q, w_q, cos_t, sin_t, seq)
    kv = _mla_kv(latent, 4, g_kv, w_kv, seq).reshape(batch, seq, -1)
    kr = small[:, :MLA_ROPE_DIM] * jnp.tile(c64, (batch, 1)) \
        + small[:, MLA_ROPE_DIM:2 * MLA_ROPE_DIM] * jnp.tile(s64, (batch, 1))
    kr = _pad_lanes(kr.astype(BF16), HEAD_DIM).reshape(batch, seq, HEAD_DIM)
    y_mla = _softmax_attention(q_nope.reshape(batch, seq, -1), 0, kv, 0, kv, 1, batch, seq,
                               extra=(q_rope.reshape(batch, seq, -1), kr))

    y_pool = _pool_mixer(latent.reshape(batch, seq, -1), 1, w_pool, pool_scale, batch, seq)

    ys = [y.reshape(n, BRANCH_WIDTH) for y in (y_fox, y_sb, y_mla, y_pool)]
    merged = _merge(h, ys, w_gates, layer, w_branch)
    return _matmul_residual(merged, w_out, x2, gate1, seq)


def _moe_layer(h, logits, gate2, b_rg, b_re, w_gate, w_up, w_down, layer):
    n, d = h.shape
    tb = 512
    g_logits = logits[:, :N_EXPERT_GROUPS] + b_rg
    g_idx = jnp.argmax(g_logits, axis=-1).astype(jnp.int32)[:, None]
    g_top = jnp.max(g_logits, axis=-1)
    p_group = jnp.exp(g_top - jax.nn.logsumexp(g_logits, axis=-1))
    e_logits = (logits[:, N_EXPERT_GROUPS:N_EXPERT_GROUPS + N_EXPERTS] + b_re).reshape(
        n, N_EXPERT_GROUPS, EXPERTS_PER_GROUP)
    e_logits = jnp.take_along_axis(e_logits, g_idx[:, :, None], axis=1)[:, 0]
    probs = jax.nn.softmax(e_logits, axis=-1)
    lane = jnp.arange(EXPERTS_PER_GROUP, dtype=jnp.int32)[None, :]
    i1 = jnp.argmax(probs, axis=-1).astype(jnp.int32)[:, None]
    p1 = jnp.max(probs, axis=-1, keepdims=True)
    rest = jnp.where(lane == i1, -jnp.inf, probs)
    i2 = jnp.argmax(rest, axis=-1).astype(jnp.int32)[:, None]
    p2 = jnp.max(rest, axis=-1, keepdims=True)
    top_p = jnp.concatenate([p1, p2], axis=-1)
    top_i = jnp.concatenate([i1, i2], axis=-1)
    weights = p_group[:, None] * top_p / jnp.sum(top_p, axis=-1, keepdims=True)

    expert_id = (g_idx * EXPERTS_PER_GROUP + top_i).reshape(-1).astype(jnp.int32)
    m = expert_id.shape[0]
    w_flat = weights.reshape(-1)

    order = jnp.argsort(expert_id).astype(jnp.int32)
    rank = jnp.argsort(order).astype(jnp.int32)
    onehot = expert_id[:, None] == jnp.arange(N_EXPERTS, dtype=jnp.int32)[None, :]
    counts = jnp.sum(onehot, axis=0, dtype=jnp.int32)
    padded = ((counts + tb - 1) // tb) * tb
    start = jnp.cumsum(counts) - counts
    pend = jnp.cumsum(padded)
    shift = pend - padded - start
    n_rows = m + N_EXPERTS * tb
    n_blocks = n_rows // tb
    blk = jnp.arange(n_blocks, dtype=jnp.int32)
    n_used = (pend[-1] // tb).astype(jnp.int32)
    block_expert = jnp.minimum(
        jnp.sum(pend[None, :] <= (blk * tb)[:, None], axis=1), N_EXPERTS - 1).astype(jnp.int32)
    last_used = block_expert[jnp.maximum(n_used - 1, 0)]
    block_expert = jnp.where(blk < n_used, block_expert, last_used)

    rows = jnp.arange(n_rows, dtype=jnp.int32).reshape(n_blocks, tb)
    src_sorted = rows - shift[block_expert][:, None]
    valid = (src_sorted < (start + counts)[block_expert][:, None]) & (blk < n_used)[:, None]
    src_assign = order.at[jnp.clip(src_sorted, 0, m - 1).reshape(-1)].get(mode="promise_in_bounds")
    valid = valid.reshape(-1)
    row_tok = jnp.where(valid, src_assign // EXPERT_TOP_K, rows.reshape(-1) % n)
    row_w = jnp.where(valid, w_flat.at[src_assign].get(mode="promise_in_bounds"), 0.0).astype(F32)
    dest = (rank + jnp.sum(jnp.where(onehot, shift[None, :], 0), axis=1)).reshape(n, EXPERT_TOP_K)

    x_buf = h.at[row_tok].get(mode="promise_in_bounds")
    y = _moe_ffn(x_buf, row_w.reshape(n_rows, 1), block_expert, n_used.reshape(1),
                 w_gate, w_up, w_down, layer, tb)
    return (y.at[dest[:, 0]].get(mode="promise_in_bounds"),
            y.at[dest[:, 1]].get(mode="promise_in_bounds"), gate2)


def kernel(x, c, w_mod, b_mod, g_norm1, g_norm2, w_in, b_forget, g_q_norm, w_uq, g_kv_norm, w_ukv, w_pool, pool_scale, w_branch, w_out, w_route_group, b_route_group, w_route_expert, b_route_expert, w_gate, w_up, w_down, g_final):
    batch, seq, d = x.shape
    depth = w_mod.shape[0]
    n = batch * seq
    nh = N_HEADS

    c_pad = jnp.pad(c, ((0, 8 - batch), (0, 0)))
    mod = _modulation(c_pad, w_mod, b_mod)[:, :batch]

    hq = nh * HEAD_DIM
    o_fox, o_f = 0, 3 * hq
    o_sb = o_f + nh
    o_cq = o_sb + 3 * hq
    o_ckv = o_cq + MLA_Q_LORA
    o_kr = o_ckv + MLA_KV_LORA
    o_pool = o_kr + MLA_ROPE_DIM
    o_gate = o_pool + len(POOL_WINDOWS) * LANE
    attn_scale = HEAD_DIM ** -0.5
    mla_scale = (MLA_NOPE_DIM + MLA_ROPE_DIM) ** -0.5

    half = MLA_ROPE_DIM // 2
    plans = (
        [(0, o_fox, o_fox + hq, attn_scale), (hq, o_fox + hq, o_fox + 3 * hq, 1.0),
         (3 * hq, o_sb, o_sb + hq, attn_scale), (4 * hq, o_sb + hq, o_sb + 3 * hq, 1.0)],
        [(0, o_cq, o_ckv, 1.0), (MLA_Q_LORA, o_pool, o_gate, 1.0),
         (MLA_Q_LORA + o_gate - o_pool, o_ckv, o_kr, 1.0)],
        [(0, o_gate, w_in.shape[2], 1.0)],
        [(0, None, SMALL_WIDTH, 0.0), (0, o_kr, o_pool, 1.0), (MLA_ROPE_DIM, o_kr + half, o_pool, 1.0),
         (MLA_ROPE_DIM + half, o_kr, o_kr + half, 1.0), (2 * MLA_ROPE_DIM, o_f, o_sb, 1.0)],
    )
    w_qkv, w_lat, w_gates, w_small1 = _regroup_w_in(
        w_in, plans, (6 * hq, o_kr - o_cq + o_gate - o_pool, w_in.shape[2] - o_gate, SMALL_WIDTH))

    x2 = x.reshape(n, d)
    pending = None
    for l in range(depth):
        m6 = mod[l].reshape(batch, 6, 1, d)
        shift1, scale1, gate1, shift2, scale2, gate2 = [m6[:, t] for t in range(6)]

        wq = w_uq[l].reshape(MLA_Q_LORA, nh, MLA_NOPE_DIM + MLA_ROPE_DIM) * mla_scale
        wq_rope = wq[:, :, MLA_NOPE_DIM:]
        w_q = jnp.concatenate([
            wq[:, :, :MLA_NOPE_DIM].reshape(MLA_Q_LORA, -1),
            _pad_lanes(wq_rope, HEAD_DIM).reshape(MLA_Q_LORA, -1),
            _pad_lanes(_swap_halves(wq_rope), HEAD_DIM).reshape(MLA_Q_LORA, -1)], axis=1).astype(BF16)
        wkv = w_ukv[l].reshape(MLA_KV_LORA, nh, 2 * HEAD_DIM)
        w_kv = jnp.concatenate([wkv[:, :, :HEAD_DIM].reshape(MLA_KV_LORA, -1),
                                wkv[:, :, HEAD_DIM:].reshape(MLA_KV_LORA, -1)], axis=1).astype(BF16)

        x2, h, small = _norm_small(x2, g_norm1[l], scale1, shift1, w_small1[l], seq,
                                   moe_in=pending)
        prm = (w_qkv, w_lat, w_gates, b_forget[l], g_q_norm[l], w_q, g_kv_norm[l], w_kv,
               w_pool[l], pool_scale[l], w_branch[l].astype(BF16), w_out[l].astype(BF16), gate1)
        x2 = _mixer_layer(x2, h, small, l, batch, seq, prm)

        w_small2 = _pad_lanes(jnp.concatenate([w_route_group[l], w_route_expert[l]], axis=1),
                              SMALL_WIDTH).astype(BF16)
        _, h, logits = _norm_small(x2, g_norm2[l], scale2, shift2, w_small2, seq)
        pending = _moe_layer(h, logits, gate2, b_route_group[l], b_route_expert[l],
                             w_gate, w_up, w_down, l)
    return _final_norm(x2, pending, g_final, seq).reshape(batch, seq, d)
```

```python
import functools

import numpy as np
import jax
import jax.numpy as jnp
from jax import lax
from jax.experimental import pallas as pl
from jax.experimental.pallas import tpu as pltpu

F32 = jnp.float32
BF16 = jnp.bfloat16

NORM_EPS = 1e-6
N_HEADS = 4
HEAD_DIM = 128
MLA_Q_LORA = 512
MLA_KV_LORA = 256
MLA_NOPE_DIM = 128
MLA_ROPE_DIM = 64
ROPE_BASE = 10000.0
POOL_WINDOWS = (2, 4, 8, 16)
POOL_HALO = 16
N_BRANCHES = 4
BRANCH_WIDTH = 512
N_EXPERT_GROUPS = 4
EXPERTS_PER_GROUP = 8
N_EXPERTS = N_EXPERT_GROUPS * EXPERTS_PER_GROUP
EXPERT_TOP_K = 2
SMALL_WIDTH = 256

LANE = 128
V7X_VMEM_BYTES = 64 * 1024 * 1024
VMEM_LIMIT = V7X_VMEM_BYTES - 12 * 1024 * 1024


MM_ROWS, MM_COLS = 1024, 1024
NORM_ROWS = 512
ATTN_Q, ATTN_K = 512, 512
SB_Q, SB_K = 256, 256
POOL_ROWS = 512
MERGE_COLS = 512
MOE_ROWS = 512
REGROUP_ROWS = 256


def _cparams(*sem, vmem=VMEM_LIMIT):
    return pltpu.CompilerParams(dimension_semantics=sem, vmem_limit_bytes=vmem)


def _tile(n, pref):
    t = min(n, pref)
    assert n % t == 0, (n, pref)
    return t


def _mod_body(c_ref, w_ref, b_ref, o_ref):
    c = c_ref[...]
    ca = c * jax.nn.sigmoid(c)
    acc = jnp.dot(ca.astype(BF16), w_ref[0].astype(BF16), preferred_element_type=F32)
    o_ref[0] = acc + b_ref[0]


def _modulation(c_pad, w_mod, b_mod):
    depth, d, n = w_mod.shape
    tn = _tile(n, MM_COLS)
    return pl.pallas_call(
        _mod_body,
        grid=(depth, n // tn),
        in_specs=[
            pl.BlockSpec((8, d), lambda l, j: (0, 0)),
            pl.BlockSpec((1, d, tn), lambda l, j: (l, 0, j)),
            pl.BlockSpec((1, 1, tn), lambda l, j: (l, 0, j)),
        ],
        out_specs=pl.BlockSpec((1, 8, tn), lambda l, j: (l, 0, j)),
        out_shape=jax.ShapeDtypeStruct((depth, 8, n), F32),
        compiler_params=_cparams("parallel", "parallel"),
        name="adaln_modulation",
    )(c_pad, w_mod, b_mod.reshape(depth, 1, n))


REGROUP_CHUNK = 1024


def _regroup_body(w_ref, *out_refs, plans):
    total = w_ref.shape[2]

    def cols(a, b):
        a0 = (a // LANE) * LANE
        b1 = min(-(-b // LANE) * LANE, total)
        return w_ref[0, :, a0:b1][:, a - a0:b - a0]

    for o_ref, plan in zip(out_refs, plans):
        for dst, a, b, scale in plan:
            if a is None:
                o_ref[0, :, dst:b] = jnp.zeros((o_ref.shape[1], b - dst), o_ref.dtype)
                continue
            for c0 in range(0, b - a, REGROUP_CHUNK):
                c1 = min(c0 + REGROUP_CHUNK, b - a)
                v = cols(a + c0, a + c1)
                if scale != 1.0:
                    v = v * scale
                o_ref[0, :, dst + c0:dst + c1] = v.astype(o_ref.dtype)


def _regroup_w_in(w_in, plans, widths):
    depth, d, total = w_in.shape
    tr = _tile(d, REGROUP_ROWS)
    return pl.pallas_call(
        functools.partial(_regroup_body, plans=plans),
        grid=(depth, d // tr),
        in_specs=[pl.BlockSpec((1, tr, total), lambda l, i: (l, i, 0))],
        out_specs=[pl.BlockSpec((1, tr, w), lambda l, i: (l, i, 0)) for w in widths],
        out_shape=[jax.ShapeDtypeStruct((depth, d, w), BF16) for w in widths],
        compiler_params=_cparams("parallel", "parallel"),
        name="regroup_w_in",
    )(w_in)


def _add_moe(x_ref, y0_ref, y1_ref, gate_ref):
    return x_ref[...] + gate_ref[0] * (y0_ref[...].astype(F32) + y1_ref[...].astype(F32))


def _moe_in_specs(tm, d, per_b):
    row = pl.BlockSpec((tm, d), lambda i: (i, 0))
    return [row, row, pl.BlockSpec((1, 1, d), lambda i: (i // per_b, 0, 0))]


def _norm_small_body(*refs, has_moe):
    if has_moe:
        x_ref, y0_ref, y1_ref, gate_ref, g_ref, sc_ref, sh_ref, w_ref, xo_ref, h_ref, s_ref = refs
        x = _add_moe(x_ref, y0_ref, y1_ref, gate_ref)
        xo_ref[...] = x
    else:
        x_ref, g_ref, sc_ref, sh_ref, w_ref, h_ref, s_ref = refs
        x = x_ref[...]
    ms = jnp.mean(x * x, axis=-1, keepdims=True)
    y = x * lax.rsqrt(ms + NORM_EPS) * g_ref[...]
    h = y * (1.0 + sc_ref[0]) + sh_ref[0]
    h_ref[...] = h.astype(h_ref.dtype)
    s_ref[...] = jnp.dot(h.astype(BF16), w_ref[...], preferred_element_type=F32)


def _norm_small(x2, g, scale, shift, w_small, seq, moe_in=None):
    n, d = x2.shape
    tm = _tile(seq, NORM_ROWS)
    per_b = seq // tm
    ws = w_small.shape[1]
    row = pl.BlockSpec((tm, d), lambda i: (i, 0))
    has_moe = moe_in is not None
    outs = pl.pallas_call(
        functools.partial(_norm_small_body, has_moe=has_moe),
        grid=(n // tm,),
        in_specs=[row] + (_moe_in_specs(tm, d, per_b) if has_moe else []) + [
            pl.BlockSpec((1, d), lambda i: (0, 0)),
            pl.BlockSpec((1, 1, d), lambda i: (i // per_b, 0, 0)),
            pl.BlockSpec((1, 1, d), lambda i: (i // per_b, 0, 0)),
            pl.BlockSpec((d, ws), lambda i: (0, 0)),
        ],
        out_specs=([row] if has_moe else []) + [row, pl.BlockSpec((tm, ws), lambda i: (i, 0))],
        out_shape=([jax.ShapeDtypeStruct((n, d), F32)] if has_moe else []) + [
            jax.ShapeDtypeStruct((n, d), BF16),
            jax.ShapeDtypeStruct((n, ws), F32),
        ],
        compiler_params=_cparams("parallel"),
        name="norm_modulate_small_proj",
    )(x2, *(moe_in if has_moe else ()), g.reshape(1, d), scale, shift, w_small)
    return outs if has_moe else (x2, *outs)


def _mm_body(a_ref, w_ref, o_ref):
    o_ref[...] = jnp.dot(a_ref[...], w_ref[0], preferred_element_type=F32).astype(o_ref.dtype)


def _matmul(a, w, layer, out_dtype, tm=MM_ROWS, tn=MM_COLS, name="matmul"):
    m, k = a.shape
    n = w.shape[2]
    tm = _tile(m, tm)
    tn = _tile(n, tn)
    return pl.pallas_call(
        _mm_body,
        grid=(m // tm, n // tn),
        in_specs=[
            pl.BlockSpec((tm, k), lambda i, j: (i, 0)),
            pl.BlockSpec((1, k, tn), lambda i, j: (layer, 0, j)),
        ],
        out_specs=pl.BlockSpec((tm, tn), lambda i, j: (i, j)),
        out_shape=jax.ShapeDtypeStruct((m, n), out_dtype),
        compiler_params=_cparams("parallel", "parallel"),
        name=name,
    )(a, w)


def _mm_residual_body(a_ref, w_ref, x_ref, gate_ref, o_ref):
    acc = jnp.dot(a_ref[...], w_ref[...], preferred_element_type=F32)
    o_ref[...] = x_ref[...] + gate_ref[0] * acc


def _matmul_residual(a, w, x2, gate, seq, tm=MM_ROWS, tn=MM_COLS):
    m, k = a.shape
    n = w.shape[1]
    tm = _tile(seq, tm)
    tn = _tile(n, tn)
    per_b = seq // tm
    return pl.pallas_call(
        _mm_residual_body,
        grid=(m // tm, n // tn),
        in_specs=[
            pl.BlockSpec((tm, k), lambda i, j: (i, 0)),
            pl.BlockSpec((k, tn), lambda i, j: (0, j)),
            pl.BlockSpec((tm, tn), lambda i, j: (i, j)),
            pl.BlockSpec((1, 1, tn), lambda i, j: (i // per_b, 0, j)),
        ],
        out_specs=pl.BlockSpec((tm, tn), lambda i, j: (i, j)),
        out_shape=jax.ShapeDtypeStruct((m, n), F32),
        compiler_params=_cparams("parallel", "parallel"),
        name="out_proj_residual",
    )(a, w, x2, gate)


def _latent_norm(a_ref, g_ref):
    a = a_ref[...]
    ms = jnp.mean(a * a, axis=-1, keepdims=True)
    return (a * lax.rsqrt(ms + NORM_EPS) * g_ref[...]).astype(BF16)


def _mla_q_body(a_ref, g_ref, w_ref, cos_ref, sin_ref, nope_ref, rope_ref):
    acc = jnp.dot(_latent_norm(a_ref, g_ref), w_ref[...], preferred_element_type=F32)
    w = nope_ref.shape[1]
    nope_ref[...] = acc[:, :w].astype(BF16)
    rope_ref[...] = (acc[:, w:2 * w] * cos_ref[...] + acc[:, 2 * w:] * sin_ref[...]).astype(BF16)


def _mla_q(latent, col_block, g, w_q, cos_t, sin_t, seq):
    n = latent.shape[0]
    k = w_q.shape[0]
    w = w_q.shape[1] // 3
    tm = _tile(seq, MM_ROWS)
    per_b = seq // tm
    return pl.pallas_call(
        _mla_q_body,
        grid=(n // tm,),
        in_specs=[
            pl.BlockSpec((tm, k), lambda i: (i, col_block)),
            pl.BlockSpec((1, k), lambda i: (0, 0)),
            pl.BlockSpec((k, 3 * w), lambda i: (0, 0)),
            pl.BlockSpec((tm, w), lambda i: (i % per_b, 0)),
            pl.BlockSpec((tm, w), lambda i: (i % per_b, 0)),
        ],
        out_specs=[
            pl.BlockSpec((tm, w), lambda i: (i, 0)),
            pl.BlockSpec((tm, w), lambda i: (i, 0)),
        ],
        out_shape=[
            jax.ShapeDtypeStruct((n, w), BF16),
            jax.ShapeDtypeStruct((n, w), BF16),
        ],
        compiler_params=_cparams("parallel"),
        name="mla_q_up_rope",
    )(latent, g.reshape(1, k), w_q, cos_t, sin_t)


def _mla_kv_body(a_ref, g_ref, w_ref, o_ref):
    o_ref[...] = jnp.dot(_latent_norm(a_ref, g_ref), w_ref[...],
                         preferred_element_type=F32).astype(BF16)


def _mla_kv(latent, col_block, g, w_kv, seq):
    n = latent.shape[0]
    k, nout = w_kv.shape
    tm = _tile(seq, MM_ROWS)
    return pl.pallas_call(
        _mla_kv_body,
        grid=(n // tm,),
        in_specs=[
            pl.BlockSpec((tm, k), lambda i: (i, col_block)),
            pl.BlockSpec((1, k), lambda i: (0, 0)),
            pl.BlockSpec((k, nout), lambda i: (0, 0)),
        ],
        out_specs=pl.BlockSpec((tm, nout), lambda i: (i, 0)),
        out_shape=jax.ShapeDtypeStruct((n, nout), BF16),
        compiler_params=_cparams("parallel"),
        name="mla_kv_up",
    )(latent, g.reshape(1, k), w_kv)


def _softmax_attn_body(*refs, tq, tk, has_extra, has_bias):
    refs = list(refs)
    qm_ref = refs.pop(0)
    qe_ref = refs.pop(0) if has_extra else None
    km_ref = refs.pop(0)
    ke_ref = refs.pop(0) if has_extra else None
    v_ref = refs.pop(0)
    kb_ref = refs.pop(0) if has_bias else None
    o_ref, m_ref, acc_ref = refs
    qi = pl.program_id(1)
    m_ref[...] = jnp.full(m_ref.shape, -jnp.inf, F32)
    acc_ref[...] = jnp.zeros(acc_ref.shape, F32)
    chunks_per_q = tq // tk
    n_full = qi * chunks_per_q
    ones = jnp.ones((tk, HEAD_DIM), BF16)

    def head_chunk(h, k0, diag):
        lanes = slice(h * HEAD_DIM, (h + 1) * HEAD_DIM)
        q = qm_ref[0, :, lanes]
        k = km_ref[0, pl.ds(k0, tk), lanes]
        if has_extra:
            q = jnp.concatenate([q, qe_ref[0, :, lanes]], axis=-1)
            k = jnp.concatenate([k, ke_ref[0, pl.ds(k0, tk), :]], axis=-1)
        s = lax.dot_general(q, k, (((1,), (1,)), ((), ())), preferred_element_type=F32)
        if has_bias:
            s = s + kb_ref[0, h, :, pl.ds(k0, tk)]
        if diag is not None:
            row = lax.broadcasted_iota(jnp.int32, (tq, tk), 0)
            col = lax.broadcasted_iota(jnp.int32, (tq, tk), 1) + diag * tk
            s = jnp.where(row >= col, s, -jnp.inf)
        blocks = [s[:, j * LANE:(j + 1) * LANE] for j in range(tk // LANE)]
        mx = blocks[0]
        for blk in blocks[1:]:
            mx = jnp.maximum(mx, blk)
        m_prev = m_ref[h]
        m_new = jnp.maximum(m_prev, jnp.max(mx, axis=-1, keepdims=True))
        alpha = jnp.exp(m_prev - m_new)
        p = jnp.concatenate([jnp.exp(blk - m_new) for blk in blocks], axis=-1).astype(BF16)
        v_ext = jnp.concatenate([v_ref[0, pl.ds(k0, tk), lanes], ones], axis=-1)
        pv = jnp.dot(p, v_ext, preferred_element_type=F32)
        acc_ref[h, :, :HEAD_DIM] = alpha * acc_ref[h, :, :HEAD_DIM] + pv[:, :HEAD_DIM]
        acc_ref[h, :, HEAD_DIM:] = alpha * acc_ref[h, :, HEAD_DIM:] + pv[:, HEAD_DIM:]
        m_ref[h] = m_new

    def chunk(c, diag):
        k0 = pl.multiple_of(c * tk, tk)
        for h in range(N_HEADS):
            head_chunk(h, k0, diag)

    def full_chunk(c, carry):
        chunk(c, None)
        return carry

    lax.fori_loop(0, n_full, full_chunk, 0)
    for d in range(chunks_per_q):
        chunk(n_full + d, d)
    for h in range(N_HEADS):
        o_ref[0, :, h * HEAD_DIM:(h + 1) * HEAD_DIM] = (
            acc_ref[h, :, :HEAD_DIM] / acc_ref[h, :, HEAD_DIM:]).astype(o_ref.dtype)


def _softmax_attention(qm, q_blk, km, k_blk, v, v_blk, batch, seq, extra=None, key_bias=None):
    tq = _tile(seq, ATTN_Q)
    tk = _tile(seq, ATTN_K)
    width = N_HEADS * HEAD_DIM
    args = [qm]
    specs = [pl.BlockSpec((1, tq, width), lambda b, i: (b, i, q_blk))]
    if extra is not None:
        args.append(extra[0])
        specs.append(pl.BlockSpec((1, tq, width), lambda b, i: (b, i, 0)))
    args.append(km)
    specs.append(pl.BlockSpec((1, seq, width), lambda b, i: (b, 0, k_blk)))
    if extra is not None:
        args.append(extra[1])
        specs.append(pl.BlockSpec((1, seq, HEAD_DIM), lambda b, i: (b, 0, 0)))
    args.append(v)
    specs.append(pl.BlockSpec((1, seq, width), lambda b, i: (b, 0, v_blk)))
    if key_bias is not None:
        args.append(key_bias)
        specs.append(pl.BlockSpec((1, N_HEADS, 1, seq), lambda b, i: (b, 0, 0, 0)))

    return pl.pallas_call(
        functools.partial(_softmax_attn_body, tq=tq, tk=tk, has_extra=extra is not None,
                          has_bias=key_bias is not None),
        grid=(batch, seq // tq),
        in_specs=specs,
        out_specs=pl.BlockSpec((1, tq, width), lambda b, i: (b, i, 0)),
        out_shape=jax.ShapeDtypeStruct((batch, seq, width), BF16),
        scratch_shapes=[
            pltpu.VMEM((N_HEADS, tq, HEAD_DIM), F32),
            pltpu.VMEM((N_HEADS, tq, 2 * HEAD_DIM), F32),
        ],
        compiler_params=_cparams("parallel", "arbitrary"),
        name="causal_softmax_attention",
    )(*args)


SB_EXP_UNDERFLOW = 110.0


def _sb_attn_body(q_ref, k_ref, v_ref, tri_ref, o_ref, r_ref, acc_ref, *, tq, tk):
    qi = pl.program_id(1)
    r_ref[...] = jnp.zeros(r_ref.shape, F32)
    acc_ref[...] = jnp.zeros(acc_ref.shape, F32)
    chunks_per_q = tq // tk
    n_full = qi * chunks_per_q

    def head_chunk(h, k0, diag):
        lanes = slice(h * HEAD_DIM, (h + 1) * HEAD_DIM)
        z = lax.dot_general(q_ref[0, :, lanes], k_ref[0, pl.ds(k0, tk), lanes],
                            (((1,), (1,)), ((), ())), preferred_element_type=F32)
        sp = jnp.maximum(z, 0.0) + jnp.log(1.0 + jnp.exp(-jnp.abs(z)))
        log_beta = z - sp
        if diag is not None:
            row = lax.broadcasted_iota(jnp.int32, (tq, tk), 0)
            col = lax.broadcasted_iota(jnp.int32, (tq, tk), 1) + diag * tk
            strict = row > col
            sp = jnp.where(strict, sp, 0.0)
        hi = sp.astype(BF16)
        lo = (sp - hi.astype(F32)).astype(BF16)
        tri = tri_ref[...]
        e =jnp.dot(hi, tri, preferred_element_type=F32) + jnp.dot(lo, tri, preferred_element_type=F32)
        r = r_ref[h]
        a = jnp.concatenate(
            [jnp.exp(log_beta[:, j * LANE:(j + 1) * LANE] - e[:, j * LANE:(j + 1) * LANE] - r)
             for j in range(tk // LANE)], axis=-1)
        if diag is not None:
            a = jnp.where(strict, a, 0.0)
        acc_ref[h] += jnp.dot(a.astype(BF16), v_ref[0, pl.ds(k0, tk), lanes],
                              preferred_element_type=F32)
        r_ref[h] = r + e[:, tk:]

    def chunk(c, diag):
        k0 = pl.multiple_of(c * tk, tk)
        for h in range(N_HEADS):
            head_chunk(h, k0, diag)

    for d in reversed(range(chunks_per_q)):
        chunk(n_full + d, d)

    def more(carry):
        i, r_min = carry
        return jnp.logical_and(i < n_full, r_min < SB_EXP_UNDERFLOW)

    def earlier_chunk(carry):
        i, _ = carry
        chunk(n_full - 1 - i, None)
        return i + 1, jnp.min(r_ref[...])

    lax.while_loop(more, earlier_chunk, (jnp.int32(0), jnp.min(r_ref[...])))
    for h in range(N_HEADS):
        o_ref[0, :, h * HEAD_DIM:(h + 1) * HEAD_DIM] = acc_ref[h].astype(o_ref.dtype)


def _sb_tri(tk):
    j_src = np.arange(tk)[:, None]
    j_dst = np.arange(tk + LANE)[None, :]
    return jnp.asarray((j_src > j_dst) | (j_dst >= tk), dtype=BF16)


def _sb_attention(qkv, q_blk, k_blk, v_blk, batch, seq):
    tq = _tile(seq, SB_Q)
    tk = _tile(seq, SB_K)
    width = N_HEADS * HEAD_DIM
    return pl.pallas_call(
        functools.partial(_sb_attn_body, tq=tq, tk=tk),
        grid=(batch, seq // tq),
        in_specs=[
            pl.BlockSpec((1, tq, width), lambda b, i: (b, i, q_blk)),
            pl.BlockSpec((1, seq, width), lambda b, i: (b, 0, k_blk)),
            pl.BlockSpec((1, seq, width), lambda b, i: (b, 0, v_blk)),
            pl.BlockSpec((tk, tk + LANE), lambda b, i: (0, 0)),
        ],
        out_specs=pl.BlockSpec((1, tq, width), lambda b, i: (b, i, 0)),
        out_shape=jax.ShapeDtypeStruct((batch, seq, width), BF16),
        scratch_shapes=[
            pltpu.VMEM((N_HEADS, tq, HEAD_DIM), F32),
            pltpu.VMEM((N_HEADS, tq, HEAD_DIM), F32),
        ],
        compiler_params=_cparams("parallel", "arbitrary"),
        name="stick_breaking_attention",
    )(qkv, qkv, qkv, _sb_tri(tk))


def _pool_body(u_ref, halo_ref, w_ref, sc_ref, o_ref, ext_ref, *, ts):
    i = pl.program_id(1)
    halo = jnp.where(i > 0, halo_ref[0], 0.0)
    ext_ref[:POOL_HALO, :] = halo
    ext_ref[POOL_HALO:, :] = u_ref[0]
    pos = i * ts + lax.broadcasted_iota(jnp.int32, (ts, 1), 0)
    outs = []
    for g, win in enumerate(POOL_WINDOWS):
        lanes = slice(g * LANE, (g + 1) * LANE)
        tok = ext_ref[POOL_HALO:, lanes]
        tot = tok
        for back in range(1, win):
            tot = tot + ext_ref[POOL_HALO - back:POOL_HALO - back + ts, lanes]
        cnt = jnp.minimum(pos + 1, win).astype(F32)
        pooled = (tot / cnt - tok).astype(BF16)
        outs.append(jnp.dot(pooled, w_ref[g], preferred_element_type=F32))
    y = jnp.concatenate(outs, axis=-1) * sc_ref[...]
    o_ref[0] = y.astype(o_ref.dtype)


def _pool_mixer(latent3, col_block, w_pool, pool_scale, batch, seq):
    width = len(POOL_WINDOWS) * LANE
    ts = _tile(seq, POOL_ROWS)
    halo_per_tile = ts // POOL_HALO
    return pl.pallas_call(
        functools.partial(_pool_body, ts=ts),
        grid=(batch, seq // ts),
        in_specs=[
            pl.BlockSpec((1, ts, width), lambda b, i: (b, i, col_block)),
            pl.BlockSpec((1, POOL_HALO, width),
                         lambda b, i: (b, jnp.maximum(i * halo_per_tile - 1, 0), col_block)),
            pl.BlockSpec((len(POOL_WINDOWS), LANE, LANE), lambda b, i: (0, 0, 0)),
            pl.BlockSpec((1, width), lambda b, i: (0, 0)),
        ],
        out_specs=pl.BlockSpec((1, ts, width), lambda b, i: (b, i, 0)),
        out_shape=jax.ShapeDtypeStruct((batch, seq, width), BF16),
        scratch_shapes=[pltpu.VMEM((ts + POOL_HALO, width), F32)],
        compiler_params=_cparams("parallel", "arbitrary"),
        name="multiscale_pool",
    )(latent3, latent3, w_pool.astype(BF16), pool_scale.reshape(1, width))


def _merge_body(h_ref, y0, y1, y2, y3, g0, g1, g2, g3, wb_ref, o_ref):
    h = h_ref[...]
    acc = None
    for n, (y, wg) in enumerate(((y0, g0), (y1, g1), (y2, g2), (y3, g3))):
        gate = jax.nn.sigmoid(jnp.dot(h, wg[0], preferred_element_type=F32))
        t = gate * jnp.dot(y[...], wb_ref[n], preferred_element_type=F32)
        acc = t if acc is None else acc + t
    o_ref[...] = acc.astype(o_ref.dtype)


def _merge(h, ys, w_gates, layer, w_branch):
    n, k = h.shape
    d = w_branch.shape[2]
    tm = _tile(n, MM_ROWS)
    tn = _tile(d, MERGE_COLS)
    nj = d // tn
    y_spec = pl.BlockSpec((tm, BRANCH_WIDTH), lambda i, j: (i, 0))

    def g_spec(b):
        return pl.BlockSpec((1, k, tn), lambda i, j: (layer, 0, b * nj + j))

    return pl.pallas_call(
        _merge_body,
        grid=(n // tm, nj),
        in_specs=[pl.BlockSpec((tm, k), lambda i, j: (i, 0))] + [y_spec] * N_BRANCHES
        + [g_spec(b) for b in range(N_BRANCHES)]
        + [pl.BlockSpec((N_BRANCHES, BRANCH_WIDTH, tn), lambda i, j: (0, 0, j))],
        out_specs=pl.BlockSpec((tm, tn), lambda i, j: (i, j)),
        out_shape=jax.ShapeDtypeStruct((n, d), BF16),
        compiler_params=_cparams("parallel", "parallel"),
        name="gated_branch_merge",
    )(h, *ys, *([w_gates] * N_BRANCHES), w_branch)


def _expert_changed(be_ref):
    blk = pl.program_id(1)
    return (blk == 0) | (be_ref[blk] != be_ref[jnp.maximum(blk - 1, 0)])


def _moe_up_body(be_ref, nb_ref, x_ref, wg_ref, wu_ref, o_ref, wg_s, wu_s):
    blk = pl.program_id(1)

    @pl.when(_expert_changed(be_ref))
    def _():
        wg_s[...] = wg_ref[0, 0].astype(BF16)
        wu_s[...] = wu_ref[0, 0].astype(BF16)

    @pl.when(blk < nb_ref[0])
    def _():
        x = x_ref[...]
        g = jnp.dot(x, wg_s[...], preferred_element_type=F32)
        u = jnp.dot(x, wu_s[...], preferred_element_type=F32)
        o_ref[...] = (g * jax.nn.sigmoid(g) * u).astype(o_ref.dtype)

    @pl.when(blk >= nb_ref[0])
    def _():
        o_ref[...] = jnp.zeros(o_ref.shape, o_ref.dtype)


def _moe_down_body(be_ref, nb_ref, a_ref, wd_ref, rw_ref, o_ref, wd_s):
    blk = pl.program_id(1)

    @pl.when(_expert_changed(be_ref))
    def _():
        wd_s[...] = wd_ref[0, 0].astype(BF16)

    @pl.when(blk < nb_ref[0])
    def _():
        y = jnp.dot(a_ref[...], wd_s[...], preferred_element_type=F32)
        o_ref[...] = (y * rw_ref[...]).astype(o_ref.dtype)

    @pl.when(blk >= nb_ref[0])
    def _():
        o_ref[...] = jnp.zeros(o_ref.shape, o_ref.dtype)


def _moe_ffn(x_buf, row_w, block_expert, n_used, w_gate, w_up, w_down, layer, tb):
    n_rows, d = x_buf.shape
    f = w_gate.shape[3]
    tf, tn = f, d
    n_blocks = n_rows // tb
    up_vmem = V7X_VMEM_BYTES - 4 * 1024 * 1024

    def used(i, nb):
        return jnp.minimum(i, nb[0] - 1)

    def expert_tile(rows, cols):
        return pl.BlockSpec((1, 1, rows, cols), lambda j, i, be, nb: (layer, be[i], 0, j))

    act = pl.pallas_call(
        _moe_up_body,
        grid_spec=pltpu.PrefetchScalarGridSpec(
            num_scalar_prefetch=2,
            grid=(f // tf, n_blocks),
            in_specs=[pl.BlockSpec((tb, d), lambda j, i, be, nb: (used(i, nb), 0)),
                      expert_tile(d, tf), expert_tile(d, tf)],
            out_specs=pl.BlockSpec((tb, tf), lambda j, i, be, nb: (i, j)),
            scratch_shapes=[pltpu.VMEM((d, tf), BF16), pltpu.VMEM((d, tf), BF16)],
        ),
        out_shape=jax.ShapeDtypeStruct((n_rows, f), BF16),
        compiler_params=_cparams("arbitrary", "arbitrary", vmem=up_vmem),
        name="expert_gate_up_swiglu",
    )(block_expert, n_used, x_buf, w_gate, w_up)
    return pl.pallas_call(
        _moe_down_body,
        grid_spec=pltpu.PrefetchScalarGridSpec(
            num_scalar_prefetch=2,
            grid=(d // tn, n_blocks),
            in_specs=[pl.BlockSpec((tb, f), lambda j, i, be, nb: (used(i, nb), 0)),
                      expert_tile(f, tn),
                      pl.BlockSpec((tb, 1), lambda j, i, be, nb: (used(i, nb), 0))],
            out_specs=pl.BlockSpec((tb, tn), lambda j, i, be, nb: (i, j)),
            scratch_shapes=[pltpu.VMEM((f, tn), BF16)],
        ),
        out_shape=jax.ShapeDtypeStruct((n_rows, d), BF16),
        compiler_params=_cparams("arbitrary", "arbitrary"),
        name="expert_down",
    )(block_expert, n_used, act, w_down, row_w)


def _final_norm_body(x_ref, y0_ref, y1_ref, gate_ref, g_ref, o_ref):
    x = _add_moe(x_ref, y0_ref, y1_ref, gate_ref)
    ms = jnp.mean(x * x, axis=-1, keepdims=True)
    o_ref[...] = x * lax.rsqrt(ms + NORM_EPS) * g_ref[...]


def _final_norm(x2, moe_in, g, seq):
    n, d = x2.shape
    tm = _tile(seq, NORM_ROWS)
    row = pl.BlockSpec((tm, d), lambda i: (i, 0))
    return pl.pallas_call(
        _final_norm_body,
        grid=(n // tm,),
        in_specs=[row] + _moe_in_specs(tm, d, seq // tm) + [pl.BlockSpec((1, d), lambda i: (0, 0))],
        out_specs=row,
        out_shape=jax.ShapeDtypeStruct((n, d), F32),
        compiler_params=_cparams("parallel"),
        name="final_rmsnorm",
    )(x2, *moe_in, g.reshape(1, d))


def _rope_tables(seq):
    half = MLA_ROPE_DIM // 2
    inv = jnp.power(ROPE_BASE, -2.0 * jnp.arange(half, dtype=F32) / MLA_ROPE_DIM)
    ang = jnp.arange(seq).astype(F32)[:, None] * inv[None, :]
    cos, sin = jnp.cos(ang), jnp.sin(ang)
    c64 = jnp.concatenate([cos, cos], axis=-1)
    s64 = jnp.concatenate([-sin, sin], axis=-1)
    return c64, s64


def _swap_halves(w):
    half = w.shape[-1] // 2
    return jnp.concatenate([w[..., half:], w[..., :half]], axis=-1)


def _pad_lanes(a, width):
    return jnp.pad(a, [(0, 0)] * (a.ndim - 1) + [(0, width - a.shape[-1])])


def _mixer_layer(x2, h, small, layer, batch, seq, prm):
    n, d = x2.shape
    (w_qkv, w_lat, w_gates, b_forget, g_q, w_q, g_kv, w_kv, w_pool, pool_scale, w_branch, w_out,
     gate1) = prm
    c64, s64 = _rope_tables(seq)

    qkv = _matmul(h, w_qkv, layer, BF16, name="fox_sb_qkv_proj").reshape(batch, seq, -1)
    latent = _matmul(h, w_lat, layer, F32, tn=w_lat.shape[2] // 2,
                     name="latent_pool_proj")

    fox_f = small[:, 2 * MLA_ROPE_DIM:2 * MLA_ROPE_DIM + N_HEADS].reshape(batch, seq, N_HEADS)
    log_f_cum = jnp.cumsum(jax.nn.log_sigmoid(fox_f + b_forget), axis=1)
    k_bias = -log_f_cum.transpose(0, 2, 1)[:, :, None, :]
    nh = N_HEADS
    y_fox = _softmax_attention(qkv, 0, qkv, 1, qkv, 2, batch, seq, key_bias=k_bias)

    y_sb = _sb_attention(qkv, 3, 4, 5, batch, seq)

    cos_t = jnp.tile(_pad_lanes(c64, HEAD_DIM), (1, nh))
    sin_t = jnp.tile(_pad_lanes(s64, HEAD_DIM), (1, nh))
    q_nope, q_rope = _mla_q(latent, 0, g_q, w_q, cos_t, sin_t, seq)
    kv = _mla_kv(latent, 4, g_kv, w_kv, seq).reshape(batch, seq, -1)
    kr = small[:, :MLA_ROPE_DIM] * jnp.tile(c64, (batch, 1)) \
        + small[:, MLA_ROPE_DIM:2 * MLA_ROPE_DIM] * jnp.tile(s64, (batch, 1))
    kr = _pad_lanes(kr.astype(BF16), HEAD_DIM).reshape(batch, seq, HEAD_DIM)
    y_mla = _softmax_attention(q_nope.reshape(batch, seq, -1), 0, kv, 0, kv, 1, batch, seq,
                               extra=(q_rope.reshape(batch, seq, -1), kr))

    y_pool = _pool_mixer(latent.reshape(batch, seq, -1), 1, w_pool, pool_scale, batch, seq)

    ys = [y.reshape(n, BRANCH_WIDTH) for y in (y_fox, y_sb, y_mla, y_pool)]
    merged = _merge(h, ys, w_gates, layer, w_branch)
    return _matmul_residual(merged, w_out, x2, gate1, seq)


def _moe_layer(h, logits, gate2, b_rg, b_re, w_gate, w_up, w_down, layer):
    n, d = h.shape
    tb = MOE_ROWS
    g_logits = logits[:, :N_EXPERT_GROUPS] + b_rg
    g_idx = jnp.argmax(g_logits, axis=-1).astype(jnp.int32)[:, None]
    g_top = jnp.max(g_logits, axis=-1)
    p_group = jnp.exp(g_top - jax.nn.logsumexp(g_logits, axis=-1))
    e_logits = (logits[:, N_EXPERT_GROUPS:N_EXPERT_GROUPS + N_EXPERTS] + b_re).reshape(
        n, N_EXPERT_GROUPS, EXPERTS_PER_GROUP)
    e_logits = jnp.take_along_axis(e_logits, g_idx[:, :, None], axis=1)[:, 0]
    probs = jax.nn.softmax(e_logits, axis=-1)
    lane = jnp.arange(EXPERTS_PER_GROUP, dtype=jnp.int32)[None, :]
    i1 = jnp.argmax(probs, axis=-1).astype(jnp.int32)[:, None]
    p1 = jnp.max(probs, axis=-1, keepdims=True)
    rest = jnp.where(lane == i1, -jnp.inf, probs)
    i2 = jnp.argmax(rest, axis=-1).astype(jnp.int32)[:, None]
    p2 = jnp.max(rest, axis=-1, keepdims=True)
    top_p = jnp.concatenate([p1, p2], axis=-1)
    top_i = jnp.concatenate([i1, i2], axis=-1)
    weights = p_group[:, None] * top_p / jnp.sum(top_p, axis=-1, keepdims=True)

    expert_id = (g_idx * EXPERTS_PER_GROUP + top_i).reshape(-1).astype(jnp.int32)
    m = expert_id.shape[0]
    w_flat = weights.reshape(-1)

    order = jnp.argsort(expert_id).astype(jnp.int32)
    rank = jnp.argsort(order).astype(jnp.int32)
    onehot = expert_id[:, None] == jnp.arange(N_EXPERTS, dtype=jnp.int32)[None, :]
    counts = jnp.sum(onehot, axis=0, dtype=jnp.int32)
    padded = ((counts + tb - 1) // tb) * tb
    start = jnp.cumsum(counts) - counts
    pend = jnp.cumsum(padded)
    shift = pend - padded - start
    n_rows = m + N_EXPERTS * tb
    n_blocks = n_rows // tb
    blk = jnp.arange(n_blocks, dtype=jnp.int32)
    n_used = (pend[-1] // tb).astype(jnp.int32)
    block_expert = jnp.minimum(
        jnp.sum(pend[None, :] <= (blk * tb)[:, None], axis=1), N_EXPERTS - 1).astype(jnp.int32)
    last_used = block_expert[jnp.maximum(n_used - 1, 0)]
    block_expert = jnp.where(blk < n_used, block_expert, last_used)

    rows = jnp.arange(n_rows, dtype=jnp.int32).reshape(n_blocks, tb)
    src_sorted = rows - shift[block_expert][:, None]
    valid = (src_sorted < (start + counts)[block_expert][:, None]) & (blk < n_used)[:, None]
    src_assign = order.at[jnp.clip(src_sorted, 0, m - 1).reshape(-1)].get(mode="promise_in_bounds")
    valid = valid.reshape(-1)
    row_tok = jnp.where(valid, src_assign // EXPERT_TOP_K, rows.reshape(-1) % n)
    row_w = jnp.where(valid, w_flat.at[src_assign].get(mode="promise_in_bounds"), 0.0).astype(F32)
    dest = (rank + jnp.sum(jnp.where(onehot, shift[None, :], 0), axis=1)).reshape(n, EXPERT_TOP_K)

    x_buf = h.at[row_tok].get(mode="promise_in_bounds")
    y = _moe_ffn(x_buf, row_w.reshape(n_rows, 1), block_expert, n_used.reshape(1),
                 w_gate, w_up, w_down, layer, tb)
    return (y.at[dest[:, 0]].get(mode="promise_in_bounds"),
            y.at[dest[:, 1]].get(mode="promise_in_bounds"), gate2)


def kernel(x, c, w_mod, b_mod, g_norm1, g_norm2, w_in, b_forget, g_q_norm, w_uq, g_kv_norm, w_ukv, w_pool, pool_scale, w_branch, w_out, w_route_group, b_route_group, w_route_expert, b_route_expert, w_gate, w_up, w_down, g_final):
    batch, seq, d = x.shape
    depth = w_mod.shape[0]
    n = batch * seq
    nh = N_HEADS

    c_pad = jnp.pad(c, ((0, 8 - batch), (0, 0)))
    mod = _modulation(c_pad, w_mod, b_mod)[:, :batch]

    hq = nh * HEAD_DIM
    o_fox, o_f = 0, 3 * hq
    o_sb = o_f + nh
    o_cq = o_sb + 3 * hq
    o_ckv = o_cq + MLA_Q_LORA
    o_kr = o_ckv + MLA_KV_LORA
    o_pool = o_kr + MLA_ROPE_DIM
    o_gate = o_pool + len(POOL_WINDOWS) * LANE
    attn_scale = HEAD_DIM ** -0.5
    mla_scale = (MLA_NOPE_DIM + MLA_ROPE_DIM) ** -0.5

    half = MLA_ROPE_DIM // 2
    plans = (
        [(0, o_fox, o_fox + hq, attn_scale), (hq, o_fox + hq, o_fox + 3 * hq, 1.0),
         (3 * hq, o_sb, o_sb + hq, attn_scale), (4 * hq, o_sb + hq, o_sb + 3 * hq, 1.0)],
        [(0, o_cq, o_ckv, 1.0), (MLA_Q_LORA, o_pool, o_gate, 1.0),
         (MLA_Q_LORA + o_gate - o_pool, o_ckv, o_kr, 1.0)],
        [(0, o_gate, w_in.shape[2], 1.0)],
        [(0, None, SMALL_WIDTH, 0.0), (0, o_kr, o_pool, 1.0), (MLA_ROPE_DIM, o_kr + half, o_pool, 1.0),
         (MLA_ROPE_DIM + half, o_kr, o_kr + half, 1.0), (2 * MLA_ROPE_DIM, o_f, o_sb, 1.0)],
    )
    w_qkv, w_lat, w_gates, w_small1 = _regroup_w_in(
        w_in, plans, (6 * hq, o_kr - o_cq + o_gate - o_pool, w_in.shape[2] - o_gate, SMALL_WIDTH))

    x2 = x.reshape(n, d)
    pending = None
    for l in range(depth):
        m6 = mod[l].reshape(batch, 6, 1, d)
        shift1, scale1, gate1, shift2, scale2, gate2 = [m6[:, t] for t in range(6)]

        wq = w_uq[l].reshape(MLA_Q_LORA, nh, MLA_NOPE_DIM + MLA_ROPE_DIM) * mla_scale
        wq_rope = wq[:, :, MLA_NOPE_DIM:]
        w_q = jnp.concatenate([
            wq[:, :, :MLA_NOPE_DIM].reshape(MLA_Q_LORA, -1),
            _pad_lanes(wq_rope, HEAD_DIM).reshape(MLA_Q_LORA, -1),
            _pad_lanes(_swap_halves(wq_rope), HEAD_DIM).reshape(MLA_Q_LORA, -1)], axis=1).astype(BF16)
        wkv = w_ukv[l].reshape(MLA_KV_LORA, nh, 2 * HEAD_DIM)
        w_kv = jnp.concatenate([wkv[:, :, :HEAD_DIM].reshape(MLA_KV_LORA, -1),
                                wkv[:, :, HEAD_DIM:].reshape(MLA_KV_LORA, -1)], axis=1).astype(BF16)

        x2, h, small = _norm_small(x2, g_norm1[l], scale1, shift1, w_small1[l], seq,
                                   moe_in=pending)
        prm = (w_qkv, w_lat, w_gates, b_forget[l], g_q_norm[l], w_q, g_kv_norm[l], w_kv,
               w_pool[l], pool_scale[l], w_branch[l].astype(BF16), w_out[l].astype(BF16), gate1)
        x2 = _mixer_layer(x2, h, small, l, batch, seq, prm)

        w_small2 = _pad_lanes(jnp.concatenate([w_route_group[l], w_route_expert[l]], axis=1),
                              SMALL_WIDTH).astype(BF16)
        _, h, logits = _norm_small(x2, g_norm2[l], scale2, shift2, w_small2, seq)
        pending = _moe_layer(h, logits, gate2, b_route_group[l], b_route_expert[l],
                             w_gate, w_up, w_down, l)
    return _final_norm(x2, pending, g_final, seq).reshape(batch, seq, d)
```

```python
import functools

import numpy as np
import jax
import jax.numpy as jnp
from jax import lax
from jax.experimental import pallas as pl
from jax.experimental.pallas import tpu as pltpu

F32 = jnp.float32
BF16 = jnp.bfloat16

NORM_EPS = 1e-6
N_HEADS = 4
HEAD_DIM = 128
MLA_Q_LORA = 512
MLA_KV_LORA = 256
MLA_NOPE_DIM = 128
MLA_ROPE_DIM = 64
ROPE_BASE = 10000.0
POOL_WINDOWS = (2, 4, 8, 16)
POOL_HALO = 16
N_BRANCHES = 4
BRANCH_WIDTH = 512
N_EXPERT_GROUPS = 4
EXPERTS_PER_GROUP = 8
N_EXPERTS = N_EXPERT_GROUPS * EXPERTS_PER_GROUP
EXPERT_TOP_K = 2
SMALL_WIDTH = 256

LANE = 128
V7X_VMEM_BYTES = 64 * 1024 * 1024
VMEM_LIMIT = V7X_VMEM_BYTES - 12 * 1024 * 1024


MM_ROWS, MM_COLS = 1024, 1024
NORM_ROWS = 512
ATTN_Q, ATTN_K = 512, 512
SB_Q, SB_K = 512, 256
POOL_ROWS = 512
MERGE_COLS = 512
MOE_ROWS = 512
REGROUP_ROWS = 256


def _cparams(*sem, vmem=VMEM_LIMIT):
    return pltpu.CompilerParams(dimension_semantics=sem, vmem_limit_bytes=vmem)


def _tile(n, pref):
    t = min(n, pref)
    assert n % t == 0, (n, pref)
    return t


def _mod_body(c_ref, w_ref, b_ref, o_ref):
    c = c_ref[...]
    ca = c * jax.nn.sigmoid(c)
    acc = jnp.dot(ca.astype(BF16), w_ref[0].astype(BF16), preferred_element_type=F32)
    o_ref[0] = acc + b_ref[0]


def _modulation(c_pad, w_mod, b_mod):
    depth, d, n = w_mod.shape
    tn = _tile(n, MM_COLS)
    return pl.pallas_call(
        _mod_body,
        grid=(depth, n // tn),
        in_specs=[
            pl.BlockSpec((8, d), lambda l, j: (0, 0)),
            pl.BlockSpec((1, d, tn), lambda l, j: (l, 0, j)),
            pl.BlockSpec((1, 1, tn), lambda l, j: (l, 0, j)),
        ],
        out_specs=pl.BlockSpec((1, 8, tn), lambda l, j: (l, 0, j)),
        out_shape=jax.ShapeDtypeStruct((depth, 8, n), F32),
        compiler_params=_cparams("parallel", "parallel"),
        name="adaln_modulation",
    )(c_pad, w_mod, b_mod.reshape(depth, 1, n))


REGROUP_CHUNK = 1024


def _regroup_body(w_ref, *out_refs, plans):
    total = w_ref.shape[2]

    def cols(a, b):
        a0 = (a // LANE) * LANE
        b1 = min(-(-b // LANE) * LANE, total)
        return w_ref[0, :, a0:b1][:, a - a0:b - a0]

    for o_ref, plan in zip(out_refs, plans):
        for dst, a, b, scale in plan:
            if a is None:
                o_ref[0, :, dst:b] = jnp.zeros((o_ref.shape[1], b - dst), o_ref.dtype)
                continue
            for c0 in range(0, b - a, REGROUP_CHUNK):
                c1 = min(c0 + REGROUP_CHUNK, b - a)
                v = cols(a + c0, a + c1)
                if scale != 1.0:
                    v = v * scale
                o_ref[0, :, dst + c0:dst + c1] = v.astype(o_ref.dtype)


def _regroup_w_in(w_in, plans, widths):
    depth, d, total = w_in.shape
    tr = _tile(d, REGROUP_ROWS)
    return pl.pallas_call(
        functools.partial(_regroup_body, plans=plans),
        grid=(depth, d // tr),
        in_specs=[pl.BlockSpec((1, tr, total), lambda l, i: (l, i, 0))],
        out_specs=[pl.BlockSpec((1, tr, w), lambda l, i: (l, i, 0)) for w in widths],
        out_shape=[jax.ShapeDtypeStruct((depth, d, w), BF16) for w in widths],
        compiler_params=_cparams("parallel", "parallel"),
        name="regroup_w_in",
    )(w_in)


def _add_moe(x_ref, y0_ref, y1_ref, gate_ref):
    return x_ref[...] + gate_ref[0] * (y0_ref[...].astype(F32) + y1_ref[...].astype(F32))


def _moe_in_specs(tm, d, per_b):
    row = pl.BlockSpec((tm, d), lambda i: (i, 0))
    return [row, row, pl.BlockSpec((1, 1, d), lambda i: (i // per_b, 0, 0))]


def _norm_small_body(*refs, has_moe):
    if has_moe:
        x_ref, y0_ref, y1_ref, gate_ref, g_ref, sc_ref, sh_ref, w_ref, xo_ref, h_ref, s_ref = refs
        x = _add_moe(x_ref, y0_ref, y1_ref, gate_ref)
        xo_ref[...] = x
    else:
        x_ref, g_ref, sc_ref, sh_ref, w_ref, h_ref, s_ref = refs
        x = x_ref[...]
    ms = jnp.mean(x * x, axis=-1, keepdims=True)
    y = x * lax.rsqrt(ms + NORM_EPS) * g_ref[...]
    h = y * (1.0 + sc_ref[0]) + sh_ref[0]
    h_ref[...] = h.astype(h_ref.dtype)
    s_ref[...] = jnp.dot(h.astype(BF16), w_ref[...], preferred_element_type=F32)


def _norm_small(x2, g, scale, shift, w_small, seq, moe_in=None):
    n, d = x2.shape
    tm = _tile(seq, NORM_ROWS)
    per_b = seq // tm
    ws = w_small.shape[1]
    row = pl.BlockSpec((tm, d), lambda i: (i, 0))
    has_moe = moe_in is not None
    outs = pl.pallas_call(
        functools.partial(_norm_small_body, has_moe=has_moe),
        grid=(n // tm,),
        in_specs=[row] + (_moe_in_specs(tm, d, per_b) if has_moe else []) + [
            pl.BlockSpec((1, d), lambda i: (0, 0)),
            pl.BlockSpec((1, 1, d), lambda i: (i // per_b, 0, 0)),
            pl.BlockSpec((1, 1, d), lambda i: (i // per_b, 0, 0)),
            pl.BlockSpec((d, ws), lambda i: (0, 0)),
        ],
        out_specs=([row] if has_moe else []) + [row, pl.BlockSpec((tm, ws), lambda i: (i, 0))],
        out_shape=([jax.ShapeDtypeStruct((n, d), F32)] if has_moe else []) + [
            jax.ShapeDtypeStruct((n, d), BF16),
            jax.ShapeDtypeStruct((n, ws), F32),
        ],
        compiler_params=_cparams("parallel"),
        name="norm_modulate_small_proj",
    )(x2, *(moe_in if has_moe else ()), g.reshape(1, d), scale, shift, w_small)
    return outs if has_moe else (x2, *outs)


def _mm_body(a_ref, w_ref, o_ref):
    o_ref[...] = jnp.dot(a_ref[...], w_ref[0], preferred_element_type=F32).astype(o_ref.dtype)


def _matmul(a, w, layer, out_dtype, tm=MM_ROWS, tn=MM_COLS, name="matmul"):
    m, k = a.shape
    n = w.shape[2]
    tm = _tile(m, tm)
    tn = _tile(n, tn)
    return pl.pallas_call(
        _mm_body,
        grid=(m // tm, n // tn),
        in_specs=[
            pl.BlockSpec((tm, k), lambda i, j: (i, 0)),
            pl.BlockSpec((1, k, tn), lambda i, j: (layer, 0, j)),
        ],
        out_specs=pl.BlockSpec((tm, tn), lambda i, j: (i, j)),
        out_shape=jax.ShapeDtypeStruct((m, n), out_dtype),
        compiler_params=_cparams("parallel", "parallel"),
        name=name,
    )(a, w)


def _mm_residual_body(a_ref, w_ref, x_ref, gate_ref, o_ref):
    acc = jnp.dot(a_ref[...], w_ref[...], preferred_element_type=F32)
    o_ref[...] = x_ref[...] + gate_ref[0] * acc


def _matmul_residual(a, w, x2, gate, seq, tm=MM_ROWS, tn=MM_COLS):
    m, k = a.shape
    n = w.shape[1]
    tm = _tile(seq, tm)
    tn = _tile(n, tn)
    per_b = seq // tm
    return pl.pallas_call(
        _mm_residual_body,
        grid=(m // tm, n // tn),
        in_specs=[
            pl.BlockSpec((tm, k), lambda i, j: (i, 0)),
            pl.BlockSpec((k, tn), lambda i, j: (0, j)),
            pl.BlockSpec((tm, tn), lambda i, j: (i, j)),
            pl.BlockSpec((1, 1, tn), lambda i, j: (i // per_b, 0, j)),
        ],
        out_specs=pl.BlockSpec((tm, tn), lambda i, j: (i, j)),
        out_shape=jax.ShapeDtypeStruct((m, n), F32),
        compiler_params=_cparams("parallel", "parallel"),
        name="out_proj_residual",
    )(a, w, x2, gate)


def _latent_norm(a_ref, g_ref):
    a = a_ref[...]
    ms = jnp.mean(a * a, axis=-1, keepdims=True)
    return (a * lax.rsqrt(ms + NORM_EPS) * g_ref[...]).astype(BF16)


def _mla_q_body(a_ref, g_ref, w_ref, cos_ref, sin_ref, nope_ref, rope_ref):
    acc = jnp.dot(_latent_norm(a_ref, g_ref), w_ref[...], preferred_element_type=F32)
    w = nope_ref.shape[1]
    nope_ref[...] = acc[:, :w].astype(BF16)
    rope_ref[...] = (acc[:, w:2 * w] * cos_ref[...] + acc[:, 2 * w:] * sin_ref[...]).astype(BF16)


def _mla_q(latent, col_block, g, w_q, cos_t, sin_t, seq):
    n = latent.shape[0]
    k = w_q.shape[0]
    w = w_q.shape[1] // 3
    tm = _tile(seq, MM_ROWS)
    per_b = seq // tm
    return pl.pallas_call(
        _mla_q_body,
        grid=(n // tm,),
        in_specs=[
            pl.BlockSpec((tm, k), lambda i: (i, col_block)),
            pl.BlockSpec((1, k), lambda i: (0, 0)),
            pl.BlockSpec((k, 3 * w), lambda i: (0, 0)),
            pl.BlockSpec((tm, w), lambda i: (i % per_b, 0)),
            pl.BlockSpec((tm, w), lambda i: (i % per_b, 0)),
        ],
        out_specs=[
            pl.BlockSpec((tm, w), lambda i: (i, 0)),
            pl.BlockSpec((tm, w), lambda i: (i, 0)),
        ],
        out_shape=[
            jax.ShapeDtypeStruct((n, w), BF16),
            jax.ShapeDtypeStruct((n, w), BF16),
        ],
        compiler_params=_cparams("parallel"),
        name="mla_q_up_rope",
    )(latent, g.reshape(1, k), w_q, cos_t, sin_t)


def _mla_kv_body(a_ref, g_ref, w_ref, o_ref):
    o_ref[...] = jnp.dot(_latent_norm(a_ref, g_ref), w_ref[...],
                         preferred_element_type=F32).astype(BF16)


def _mla_kv(latent, col_block, g, w_kv, seq):
    n = latent.shape[0]
    k, nout = w_kv.shape
    tm = _tile(seq, MM_ROWS)
    return pl.pallas_call(
        _mla_kv_body,
        grid=(n // tm,),
        in_specs=[
            pl.BlockSpec((tm, k), lambda i: (i, col_block)),
            pl.BlockSpec((1, k), lambda i: (0, 0)),
            pl.BlockSpec((k, nout), lambda i: (0, 0)),
        ],
        out_specs=pl.BlockSpec((tm, nout), lambda i: (i, 0)),
        out_shape=jax.ShapeDtypeStruct((n, nout), BF16),
        compiler_params=_cparams("parallel"),
        name="mla_kv_up",
    )(latent, g.reshape(1, k), w_kv)


def _softmax_attn_body(*refs, tq, tk, has_extra, has_bias):
    refs = list(refs)
    qm_ref = refs.pop(0)
    qe_ref = refs.pop(0) if has_extra else None
    km_ref = refs.pop(0)
    ke_ref = refs.pop(0) if has_extra else None
    v_ref = refs.pop(0)
    kb_ref = refs.pop(0) if has_bias else None
    o_ref, m_ref, acc_ref = refs
    qi = pl.program_id(1)
    m_ref[...] = jnp.full(m_ref.shape, -jnp.inf, F32)
    acc_ref[...] = jnp.zeros(acc_ref.shape, F32)
    chunks_per_q = tq // tk
    n_full = qi * chunks_per_q
    ones = jnp.ones((tk, HEAD_DIM), BF16)

    def head_chunk(h, k0, diag):
        lanes = slice(h * HEAD_DIM, (h + 1) * HEAD_DIM)
        q = qm_ref[0, :, lanes]
        k = km_ref[0, pl.ds(k0, tk), lanes]
        if has_extra:
            q = jnp.concatenate([q, qe_ref[0, :, lanes]], axis=-1)
            k = jnp.concatenate([k, ke_ref[0, pl.ds(k0, tk), :]], axis=-1)
        s = lax.dot_general(q, k, (((1,), (1,)), ((), ())), preferred_element_type=F32)
        if has_bias:
            s = s + kb_ref[0, h, :, pl.ds(k0, tk)]
        if diag is not None:
            row = lax.broadcasted_iota(jnp.int32, (tq, tk), 0)
            col = lax.broadcasted_iota(jnp.int32, (tq, tk), 1) + diag * tk
            s = jnp.where(row >= col, s, -jnp.inf)
        blocks = [s[:, j * LANE:(j + 1) * LANE] for j in range(tk // LANE)]
        mx = blocks[0]
        for blk in blocks[1:]:
            mx = jnp.maximum(mx, blk)
        m_prev = m_ref[h]
        m_new = jnp.maximum(m_prev, jnp.max(mx, axis=-1, keepdims=True))
        alpha = jnp.exp(m_prev - m_new)
        p = jnp.concatenate([jnp.exp(blk - m_new) for blk in blocks], axis=-1).astype(BF16)
        v_ext = jnp.concatenate([v_ref[0, pl.ds(k0, tk), lanes], ones], axis=-1)
        pv = jnp.dot(p, v_ext, preferred_element_type=F32)
        acc_ref[h, :, :HEAD_DIM] = alpha * acc_ref[h, :, :HEAD_DIM] + pv[:, :HEAD_DIM]
        acc_ref[h, :, HEAD_DIM:] = alpha * acc_ref[h, :, HEAD_DIM:] + pv[:, HEAD_DIM:]
        m_ref[h] = m_new

    def chunk(c, diag):
        k0 = pl.multiple_of(c * tk, tk)
        for h in range(N_HEADS):
            head_chunk(h, k0, diag)

    def full_chunk(c, carry):
        chunk(c, None)
        return carry

    lax.fori_loop(0, n_full, full_chunk, 0)
    for d in range(chunks_per_q):
        chunk(n_full + d, d)
    for h in range(N_HEADS):
        o_ref[0, :, h * HEAD_DIM:(h + 1) * HEAD_DIM] = (
            acc_ref[h, :, :HEAD_DIM] / acc_ref[h, :, HEAD_DIM:]).astype(o_ref.dtype)


def _softmax_attention(qm, q_blk, km, k_blk, v, v_blk, batch, seq, extra=None, key_bias=None):
    tq = _tile(seq, ATTN_Q)
    tk = _tile(seq, ATTN_K)
    width = N_HEADS * HEAD_DIM
    args = [qm]
    specs = [pl.BlockSpec((1, tq, width), lambda b, i: (b, i, q_blk))]
    if extra is not None:
        args.append(extra[0])
        specs.append(pl.BlockSpec((1, tq, width), lambda b, i: (b, i, 0)))
    args.append(km)
    specs.append(pl.BlockSpec((1, seq, width), lambda b, i: (b, 0, k_blk)))
    if extra is not None:
        args.append(extra[1])
        specs.append(pl.BlockSpec((1, seq, HEAD_DIM), lambda b, i: (b, 0, 0)))
    args.append(v)
    specs.append(pl.BlockSpec((1, seq, width), lambda b, i: (b, 0, v_blk)))
    if key_bias is not None:
        args.append(key_bias)
        specs.append(pl.BlockSpec((1, N_HEADS, 1, seq), lambda b, i: (b, 0, 0, 0)))

    return pl.pallas_call(
        functools.partial(_softmax_attn_body, tq=tq, tk=tk, has_extra=extra is not None,
                          has_bias=key_bias is not None),
        grid=(batch, seq // tq),
        in_specs=specs,
        out_specs=pl.BlockSpec((1, tq, width), lambda b, i: (b, i, 0)),
        out_shape=jax.ShapeDtypeStruct((batch, seq, width), BF16),
        scratch_shapes=[
            pltpu.VMEM((N_HEADS, tq, HEAD_DIM), F32),
            pltpu.VMEM((N_HEADS, tq, 2 * HEAD_DIM), F32),
        ],
        compiler_params=_cparams("parallel", "arbitrary"),
        name="causal_softmax_attention",
    )(*args)


SB_EXP_UNDERFLOW = 110.0


def _sb_attn_body(q_ref, k_ref, v_ref, tri_ref, o_ref, r_ref, acc_ref, *, tq, tk):
    qi = pl.program_id(1)
    r_ref[...] = jnp.zeros(r_ref.shape, F32)
    acc_ref[...] = jnp.zeros(acc_ref.shape, F32)
    chunks_per_q = tq // tk
    n_full = qi * chunks_per_q

    def head_chunk(h, k0, diag):
        lanes = slice(h * HEAD_DIM, (h + 1) * HEAD_DIM)
        z = lax.dot_general(q_ref[0, :, lanes], k_ref[0, pl.ds(k0, tk), lanes],
                            (((1,), (1,)), ((), ())), preferred_element_type=F32)
        sp = jnp.maximum(z, 0.0) + jnp.log(1.0 + jnp.exp(-jnp.abs(z)))
        log_beta = z - sp
        if diag is not None:
            row = lax.broadcasted_iota(jnp.int32, (tq, tk), 0)
            col = lax.broadcasted_iota(jnp.int32, (tq, tk), 1) + diag * tk
            strict = row > col
            sp = jnp.where(strict, sp, 0.0)
        hi = sp.astype(BF16)
        lo = (sp - hi.astype(F32)).astype(BF16)
        tri = tri_ref[...]
        e =jnp.dot(hi, tri, preferred_element_type=F32) + jnp.dot(lo, tri, preferred_element_type=F32)
        r = r_ref[h]
        a = jnp.concatenate(
            [jnp.exp(log_beta[:, j * LANE:(j + 1) * LANE] - e[:, j * LANE:(j + 1) * LANE] - r)
             for j in range(tk // LANE)], axis=-1)
        if diag is not None:
            a = jnp.where(strict, a, 0.0)
        acc_ref[h] += jnp.dot(a.astype(BF16), v_ref[0, pl.ds(k0, tk), lanes],
                              preferred_element_type=F32)
        r_ref[h] = r + e[:, tk:]

    def chunk(c, diag):
        k0 = pl.multiple_of(c * tk, tk)
        for h in range(N_HEADS):
            head_chunk(h, k0, diag)

    for d in reversed(range(chunks_per_q)):
        chunk(n_full + d, d)

    def more(carry):
        i, r_min = carry
        return jnp.logical_and(i < n_full, r_min < SB_EXP_UNDERFLOW)

    def earlier_chunk(carry):
        i, _ = carry
        chunk(n_full - 1 - i, None)
        return i + 1, jnp.min(r_ref[...])

    lax.while_loop(more, earlier_chunk, (jnp.int32(0), jnp.min(r_ref[...])))
    for h in range(N_HEADS):
        o_ref[0, :, h * HEAD_DIM:(h + 1) * HEAD_DIM] = acc_ref[h].astype(o_ref.dtype)


def _sb_tri(tk):
    j_src = np.arange(tk)[:, None]
    j_dst = np.arange(tk + LANE)[None, :]
    return jnp.asarray((j_src > j_dst) | (j_dst >= tk), dtype=BF16)


def _sb_attention(qkv, q_blk, k_blk, v_blk, batch, seq):
    tq = _tile(seq, SB_Q)
    tk = _tile(seq, SB_K)
    width = N_HEADS * HEAD_DIM
    return pl.pallas_call(
        functools.partial(_sb_attn_body, tq=tq, tk=tk),
        grid=(batch, seq // tq),
        in_specs=[
            pl.BlockSpec((1, tq, width), lambda b, i: (b, i, q_blk)),
            pl.BlockSpec((1, seq, width), lambda b, i: (b, 0, k_blk)),
            pl.BlockSpec((1, seq, width), lambda b, i: (b, 0, v_blk)),
            pl.BlockSpec((tk, tk + LANE), lambda b, i: (0, 0)),
        ],
        out_specs=pl.BlockSpec((1, tq, width), lambda b, i: (b, i, 0)),
        out_shape=jax.ShapeDtypeStruct((batch, seq, width), BF16),
        scratch_shapes=[
            pltpu.VMEM((N_HEADS, tq, HEAD_DIM), F32),
            pltpu.VMEM((N_HEADS, tq, HEAD_DIM), F32),
        ],
        compiler_params=_cparams("parallel", "arbitrary"),
        name="stick_breaking_attention",
    )(qkv, qkv, qkv, _sb_tri(tk))


def _pool_body(u_ref, halo_ref, w_ref, sc_ref, o_ref, ext_ref, *, ts):
    i = pl.program_id(1)
    halo = jnp.where(i > 0, halo_ref[0], 0.0)
    ext_ref[:POOL_HALO, :] = halo
    ext_ref[POOL_HALO:, :] = u_ref[0]
    pos = i * ts + lax.broadcasted_iota(jnp.int32, (ts, 1), 0)
    outs = []
    for g, win in enumerate(POOL_WINDOWS):
        lanes = slice(g * LANE, (g + 1) * LANE)
        tok = ext_ref[POOL_HALO:, lanes]
        tot = tok
        for back in range(1, win):
            tot = tot + ext_ref[POOL_HALO - back:POOL_HALO - back + ts, lanes]
        cnt = jnp.minimum(pos + 1, win).astype(F32)
        pooled = (tot / cnt - tok).astype(BF16)
        outs.append(jnp.dot(pooled, w_ref[g], preferred_element_type=F32))
    y = jnp.concatenate(outs, axis=-1) * sc_ref[...]
    o_ref[0] = y.astype(o_ref.dtype)


def _pool_mixer(latent3, col_block, w_pool, pool_scale, batch, seq):
    width = len(POOL_WINDOWS) * LANE
    ts = _tile(seq, POOL_ROWS)
    halo_per_tile = ts // POOL_HALO
    return pl.pallas_call(
        functools.partial(_pool_body, ts=ts),
        grid=(batch, seq // ts),
        in_specs=[
            pl.BlockSpec((1, ts, width), lambda b, i: (b, i, col_block)),
            pl.BlockSpec((1, POOL_HALO, width),
                         lambda b, i: (b, jnp.maximum(i * halo_per_tile - 1, 0), col_block)),
            pl.BlockSpec((len(POOL_WINDOWS), LANE, LANE), lambda b, i: (0, 0, 0)),
            pl.BlockSpec((1, width), lambda b, i: (0, 0)),
        ],
        out_specs=pl.BlockSpec((1, ts, width), lambda b, i: (b, i, 0)),
        out_shape=jax.ShapeDtypeStruct((batch, seq, width), BF16),
        scratch_shapes=[pltpu.VMEM((ts + POOL_HALO, width), F32)],
        compiler_params=_cparams("parallel", "arbitrary"),
        name="multiscale_pool",
    )(latent3, latent3, w_pool.astype(BF16), pool_scale.reshape(1, width))


def _merge_body(h_ref, y0, y1, y2, y3, g0, g1, g2, g3, wb_ref, o_ref):
    h = h_ref[...]
    acc = None
    for n, (y, wg) in enumerate(((y0, g0), (y1, g1), (y2, g2), (y3, g3))):
        gate = jax.nn.sigmoid(jnp.dot(h, wg[0], preferred_element_type=F32))
        t = gate * jnp.dot(y[...], wb_ref[n], preferred_element_type=F32)
        acc = t if acc is None else acc + t
    o_ref[...] = acc.astype(o_ref.dtype)


def _merge(h, ys, w_gates, layer, w_branch):
    n, k = h.shape
    d = w_branch.shape[2]
    tm = _tile(n, MM_ROWS)
    tn = _tile(d, MERGE_COLS)
    nj = d // tn
    y_spec = pl.BlockSpec((tm, BRANCH_WIDTH), lambda i, j: (i, 0))

    def g_spec(b):
        return pl.BlockSpec((1, k, tn), lambda i, j: (layer, 0, b * nj + j))

    return pl.pallas_call(
        _merge_body,
        grid=(n // tm, nj),
        in_specs=[pl.BlockSpec((tm, k), lambda i, j: (i, 0))] + [y_spec] * N_BRANCHES
        + [g_spec(b) for b in range(N_BRANCHES)]
        + [pl.BlockSpec((N_BRANCHES, BRANCH_WIDTH, tn), lambda i, j: (0, 0, j))],
        out_specs=pl.BlockSpec((tm, tn), lambda i, j: (i, j)),
        out_shape=jax.ShapeDtypeStruct((n, d), BF16),
        compiler_params=_cparams("parallel", "parallel"),
        name="gated_branch_merge",
    )(h, *ys, *([w_gates] * N_BRANCHES), w_branch)


def _expert_changed(be_ref):
    blk = pl.program_id(1)
    return (blk == 0) | (be_ref[blk] != be_ref[jnp.maximum(blk - 1, 0)])


def _moe_up_body(be_ref, nb_ref, x_ref, wg_ref, wu_ref, o_ref, wg_s, wu_s):
    blk = pl.program_id(1)

    @pl.when(_expert_changed(be_ref))
    def _():
        wg_s[...] = wg_ref[0, 0].astype(BF16)
        wu_s[...] = wu_ref[0, 0].astype(BF16)

    @pl.when(blk < nb_ref[0])
    def _():
        x = x_ref[...]
        g = jnp.dot(x, wg_s[...], preferred_element_type=F32)
        u = jnp.dot(x, wu_s[...], preferred_element_type=F32)
        o_ref[...] = (g * jax.nn.sigmoid(g) * u).astype(o_ref.dtype)

    @pl.when(blk >= nb_ref[0])
    def _():
        o_ref[...] = jnp.zeros(o_ref.shape, o_ref.dtype)


def _moe_down_body(be_ref, nb_ref, a_ref, wd_ref, rw_ref, o_ref, wd_s):
    blk = pl.program_id(1)

    @pl.when(_expert_changed(be_ref))
    def _():
        wd_s[...] = wd_ref[0, 0].astype(BF16)

    @pl.when(blk < nb_ref[0])
    def _():
        y = jnp.dot(a_ref[...], wd_s[...], preferred_element_type=F32)
        o_ref[...] = (y * rw_ref[...]).astype(o_ref.dtype)

    @pl.when(blk >= nb_ref[0])
    def _():
        o_ref[...] = jnp.zeros(o_ref.shape, o_ref.dtype)


def _moe_ffn(x_buf, row_w, block_expert, n_used, w_gate, w_up, w_down, layer, tb):
    n_rows, d = x_buf.shape
    f = w_gate.shape[3]
    tf, tn = f, d
    n_blocks = n_rows // tb
    up_vmem = V7X_VMEM_BYTES - 4 * 1024 * 1024

    def used(i, nb):
        return jnp.minimum(i, nb[0] - 1)

    def expert_tile(rows, cols):
        return pl.BlockSpec((1, 1, rows, cols), lambda j, i, be, nb: (layer, be[i], 0, j))

    act = pl.pallas_call(
        _moe_up_body,
        grid_spec=pltpu.PrefetchScalarGridSpec(
            num_scalar_prefetch=2,
            grid=(f // tf, n_blocks),
            in_specs=[pl.BlockSpec((tb, d), lambda j, i, be, nb: (used(i, nb), 0)),
                      expert_tile(d, tf), expert_tile(d, tf)],
            out_specs=pl.BlockSpec((tb, tf), lambda j, i, be, nb: (i, j)),
            scratch_shapes=[pltpu.VMEM((d, tf), BF16), pltpu.VMEM((d, tf), BF16)],
        ),
        out_shape=jax.ShapeDtypeStruct((n_rows, f), BF16),
        compiler_params=_cparams("arbitrary", "arbitrary", vmem=up_vmem),
        name="expert_gate_up_swiglu",
    )(block_expert, n_used, x_buf, w_gate, w_up)
    return pl.pallas_call(
        _moe_down_body,
        grid_spec=pltpu.PrefetchScalarGridSpec(
            num_scalar_prefetch=2,
            grid=(d // tn, n_blocks),
            in_specs=[pl.BlockSpec((tb, f), lambda j, i, be, nb: (used(i, nb), 0)),
                      expert_tile(f, tn),
                      pl.BlockSpec((tb, 1), lambda j, i, be, nb: (used(i, nb), 0))],
            out_specs=pl.BlockSpec((tb, tn), lambda j, i, be, nb: (i, j)),
            scratch_shapes=[pltpu.VMEM((f, tn), BF16)],
        ),
        out_shape=jax.ShapeDtypeStruct((n_rows, d), BF16),
        compiler_params=_cparams("arbitrary", "arbitrary"),
        name="expert_down",
    )(block_expert, n_used, act, w_down, row_w)


def _final_norm_body(x_ref, y0_ref, y1_ref, gate_ref, g_ref, o_ref):
    x = _add_moe(x_ref, y0_ref, y1_ref, gate_ref)
    ms = jnp.mean(x * x, axis=-1, keepdims=True)
    o_ref[...] = x * lax.rsqrt(ms + NORM_EPS) * g_ref[...]


def _final_norm(x2, moe_in, g, seq):
    n, d = x2.shape
    tm = _tile(seq, NORM_ROWS)
    row = pl.BlockSpec((tm, d), lambda i: (i, 0))
    return pl.pallas_call(
        _final_norm_body,
        grid=(n // tm,),
        in_specs=[row] + _moe_in_specs(tm, d, seq // tm) + [pl.BlockSpec((1, d), lambda i: (0, 0))],
        out_specs=row,
        out_shape=jax.ShapeDtypeStruct((n, d), F32),
        compiler_params=_cparams("parallel"),
        name="final_rmsnorm",
    )(x2, *moe_in, g.reshape(1, d))


def _rope_tables(seq):
    half = MLA_ROPE_DIM // 2
    inv = jnp.power(ROPE_BASE, -2.0 * jnp.arange(half, dtype=F32) / MLA_ROPE_DIM)
    ang = jnp.arange(seq).astype(F32)[:, None] * inv[None, :]
    cos, sin = jnp.cos(ang), jnp.sin(ang)
    c64 = jnp.concatenate([cos, cos], axis=-1)
    s64 = jnp.concatenate([-sin, sin], axis=-1)
    return c64, s64


def _swap_halves(w):
    half = w.shape[-1] // 2
    return jnp.concatenate([w[..., half:], w[..., :half]], axis=-1)


def _pad_lanes(a, width):
    return jnp.pad(a, [(0, 0)] * (a.ndim - 1) + [(0, width - a.shape[-1])])


def _mixer_layer(x2, h, small, layer, batch, seq, prm):
    n, d = x2.shape
    (w_qkv, w_lat, w_gates, b_forget, g_q, w_q, g_kv, w_kv, w_pool, pool_scale, w_branch, w_out,
     gate1) = prm
    c64, s64 = _rope_tables(seq)

    qkv = _matmul(h, w_qkv, layer, BF16, name="fox_sb_qkv_proj").reshape(batch, seq, -1)
    latent = _matmul(h, w_lat, layer, F32, tn=w_lat.shape[2] // 2,
                     name="latent_pool_proj")

    fox_f = small[:, 2 * MLA_ROPE_DIM:2 * MLA_ROPE_DIM + N_HEADS].reshape(batch, seq, N_HEADS)
    log_f_cum = jnp.cumsum(jax.nn.log_sigmoid(fox_f + b_forget), axis=1)
    k_bias = -log_f_cum.transpose(0, 2, 1)[:, :, None, :]
    nh = N_HEADS
    y_fox = _softmax_attention(qkv, 0, qkv, 1, qkv, 2, batch, seq, key_bias=k_bias)

    y_sb = _sb_attention(qkv, 3, 4, 5, batch, seq)

    cos_t = jnp.tile(_pad_lanes(c64, HEAD_DIM), (1, nh))
    sin_t = jnp.tile(_pad_lanes(s64, HEAD_DIM), (1, nh))
    q_nope, q_rope = _mla_q(latent, 0, g_q, w_q, cos_t, sin_t, seq)
    kv = _mla_kv(latent, 4, g_kv, w_kv, seq).reshape(batch, seq, -1)
    kr = small[:, :MLA_ROPE_DIM] * jnp.tile(c64, (batch, 1)) \
        + small[:, MLA_ROPE_DIM:2 * MLA_ROPE_DIM] * jnp.tile(s64, (batch, 1))
    kr = _pad_lanes(kr.astype(BF16), HEAD_DIM).reshape(batch, seq, HEAD_DIM)
    y_mla = _softmax_attention(q_nope.reshape(batch, seq, -1), 0, kv, 0, kv, 1, batch, seq,
                               extra=(q_rope.reshape(batch, seq, -1), kr))

    y_pool = _pool_mixer(latent.reshape(batch, seq, -1), 1, w_pool, pool_scale, batch, seq)

    ys = [y.reshape(n, BRANCH_WIDTH) for y in (y_fox, y_sb, y_mla, y_pool)]
    merged = _merge(h, ys, w_gates, layer, w_branch)
    return _matmul_residual(merged, w_out, x2, gate1, seq)


def _moe_layer(h, logits, gate2, b_rg, b_re, w_gate, w_up, w_down, layer):
    n, d = h.shape
    tb = MOE_ROWS
    g_logits = logits[:, :N_EXPERT_GROUPS] + b_rg
    g_idx = jnp.argmax(g_logits, axis=-1).astype(jnp.int32)[:, None]
    g_top = jnp.max(g_logits, axis=-1)
    p_group = jnp.exp(g_top - jax.nn.logsumexp(g_logits, axis=-1))
    e_logits = logits[:, N_EXPERT_GROUPS:N_EXPERT_GROUPS + N_EXPERTS] + b_re
    lane = jnp.arange(N_EXPERTS, dtype=jnp.int32)[None, :]
    in_group = (lane // EXPERTS_PER_GROUP) == g_idx
    probs = jax.nn.softmax(jnp.where(in_group, e_logits, -jnp.inf), axis=-1)
    i1 = jnp.argmax(probs, axis=-1).astype(jnp.int32)[:, None]
    p1 = jnp.max(probs, axis=-1, keepdims=True)
    rest = jnp.where(in_group & (lane != i1), probs, -jnp.inf)
    i2 = jnp.argmax(rest, axis=-1).astype(jnp.int32)[:, None]
    p2 = jnp.max(rest, axis=-1, keepdims=True)
    top_p = jnp.concatenate([p1, p2], axis=-1)
    weights = p_group[:, None] * top_p / jnp.sum(top_p, axis=-1, keepdims=True)

    expert_id = jnp.concatenate([i1, i2], axis=-1).reshape(-1)
    m = expert_id.shape[0]
    w_flat = weights.reshape(-1)

    order = jnp.argsort(expert_id).astype(jnp.int32)
    rank = jnp.argsort(order).astype(jnp.int32)
    onehot = expert_id[:, None] == jnp.arange(N_EXPERTS, dtype=jnp.int32)[None, :]
    counts = jnp.sum(onehot, axis=0, dtype=jnp.int32)
    padded = ((counts + tb - 1) // tb) * tb
    start = jnp.cumsum(counts) - counts
    pend = jnp.cumsum(padded)
    shift = pend - padded - start
    n_rows = m + N_EXPERTS * tb
    n_blocks = n_rows // tb
    blk = jnp.arange(n_blocks, dtype=jnp.int32)
    n_used = (pend[-1] // tb).astype(jnp.int32)
    block_expert = jnp.minimum(
        jnp.sum(pend[None, :] <= (blk * tb)[:, None], axis=1), N_EXPERTS - 1).astype(jnp.int32)
    last_used = block_expert[jnp.maximum(n_used - 1, 0)]
    block_expert = jnp.where(blk < n_used, block_expert, last_used)

    rows = jnp.arange(n_rows, dtype=jnp.int32).reshape(n_blocks, tb)
    src_sorted = rows - shift[block_expert][:, None]
    valid = (src_sorted < (start + counts)[block_expert][:, None]) & (blk < n_used)[:, None]
    src_assign = order.at[jnp.clip(src_sorted, 0, m - 1).reshape(-1)].get(mode="promise_in_bounds")
    valid = valid.reshape(-1)
    row_tok = jnp.where(valid, src_assign // EXPERT_TOP_K, rows.reshape(-1) % n)
    row_w = jnp.where(valid, w_flat.at[src_assign].get(mode="promise_in_bounds"), 0.0).astype(F32)
    dest = (rank + jnp.sum(jnp.where(onehot, shift[None, :], 0), axis=1)).reshape(n, EXPERT_TOP_K)

    x_buf = h.at[row_tok].get(mode="promise_in_bounds")
    y = _moe_ffn(x_buf, row_w.reshape(n_rows, 1), block_expert, n_used.reshape(1),
                 w_gate, w_up, w_down, layer, tb)
    return (y.at[dest[:, 0]].get(mode="promise_in_bounds"),
            y.at[dest[:, 1]].get(mode="promise_in_bounds"), gate2)


def kernel(x, c, w_mod, b_mod, g_norm1, g_norm2, w_in, b_forget, g_q_norm, w_uq, g_kv_norm, w_ukv, w_pool, pool_scale, w_branch, w_out, w_route_group, b_route_group, w_route_expert, b_route_expert, w_gate, w_up, w_down, g_final):
    batch, seq, d = x.shape
    depth = w_mod.shape[0]
    n = batch * seq
    nh = N_HEADS

    c_pad = jnp.pad(c, ((0, 8 - batch), (0, 0)))
    mod = _modulation(c_pad, w_mod, b_mod)[:, :batch]

    hq = nh * HEAD_DIM
    o_fox, o_f = 0, 3 * hq
    o_sb = o_f + nh
    o_cq = o_sb + 3 * hq
    o_ckv = o_cq + MLA_Q_LORA
    o_kr = o_ckv + MLA_KV_LORA
    o_pool = o_kr + MLA_ROPE_DIM
    o_gate = o_pool + len(POOL_WINDOWS) * LANE
    attn_scale = HEAD_DIM ** -0.5
    mla_scale = (MLA_NOPE_DIM + MLA_ROPE_DIM) ** -0.5

    half = MLA_ROPE_DIM // 2
    plans = (
        [(0, o_fox, o_fox + hq, attn_scale), (hq, o_fox + hq, o_fox + 3 * hq, 1.0),
         (3 * hq, o_sb, o_sb + hq, attn_scale), (4 * hq, o_sb + hq, o_sb + 3 * hq, 1.0)],
        [(0, o_cq, o_ckv, 1.0), (MLA_Q_LORA, o_pool, o_gate, 1.0),
         (MLA_Q_LORA + o_gate - o_pool, o_ckv, o_kr, 1.0)],
        [(0, o_gate, w_in.shape[2], 1.0)],
        [(0, None, SMALL_WIDTH, 0.0), (0, o_kr, o_pool, 1.0), (MLA_ROPE_DIM, o_kr + half, o_pool, 1.0),
         (MLA_ROPE_DIM + half, o_kr, o_kr + half, 1.0), (2 * MLA_ROPE_DIM, o_f, o_sb, 1.0)],
    )
    w_qkv, w_lat, w_gates, w_small1 = _regroup_w_in(
        w_in, plans, (6 * hq, o_kr - o_cq + o_gate - o_pool, w_in.shape[2] - o_gate, SMALL_WIDTH))

    x2 = x.reshape(n, d)
    pending = None
    for l in range(depth):
        m6 = mod[l].reshape(batch, 6, 1, d)
        shift1, scale1, gate1, shift2, scale2, gate2 = [m6[:, t] for t in range(6)]

        wq = w_uq[l].reshape(MLA_Q_LORA, nh, MLA_NOPE_DIM + MLA_ROPE_DIM) * mla_scale
        wq_rope = wq[:, :, MLA_NOPE_DIM:]
        w_q = jnp.concatenate([
            wq[:, :, :MLA_NOPE_DIM].reshape(MLA_Q_LORA, -1),
            _pad_lanes(wq_rope, HEAD_DIM).reshape(MLA_Q_LORA, -1),
            _pad_lanes(_swap_halves(wq_rope), HEAD_DIM).reshape(MLA_Q_LORA, -1)], axis=1).astype(BF16)
        wkv = w_ukv[l].reshape(MLA_KV_LORA, nh, 2 * HEAD_DIM)
        w_kv = jnp.concatenate([wkv[:, :, :HEAD_DIM].reshape(MLA_KV_LORA, -1),
                                wkv[:, :, HEAD_DIM:].reshape(MLA_KV_LORA, -1)], axis=1).astype(BF16)

        x2, h, small = _norm_small(x2, g_norm1[l], scale1, shift1, w_small1[l], seq,
                                   moe_in=pending)
        prm = (w_qkv, w_lat, w_gates, b_forget[l], g_q_norm[l], w_q, g_kv_norm[l], w_kv,
               w_pool[l], pool_scale[l], w_branch[l].astype(BF16), w_out[l].astype(BF16), gate1)
        x2 = _mixer_layer(x2, h, small, l, batch, seq, prm)

        w_small2 = _pad_lanes(jnp.concatenate([w_route_group[l], w_route_expert[l]], axis=1),
                              SMALL_WIDTH).astype(BF16)
        _, h, logits = _norm_small(x2, g_norm2[l], scale2, shift2, w_small2, seq)
        pending = _moe_layer(h, logits, gate2, b_route_group[l], b_route_expert[l],
                             w_gate, w_up, w_down, l)
    return _final_norm(x2, pending, g_final, seq).reshape(batch, seq, d)
```

```python
import functools

import numpy as np
import jax
import jax.numpy as jnp
from jax import lax
from jax.experimental import pallas as pl
from jax.experimental.pallas import tpu as pltpu

F32 = jnp.float32
BF16 = jnp.bfloat16

NORM_EPS = 1e-6
N_HEADS = 4
HEAD_DIM = 128
MLA_Q_LORA = 512
MLA_KV_LORA = 256
MLA_NOPE_DIM = 128
MLA_ROPE_DIM = 64
ROPE_BASE = 10000.0
POOL_WINDOWS = (2, 4, 8, 16)
POOL_HALO = 16
N_BRANCHES = 4
BRANCH_WIDTH = 512
N_EXPERT_GROUPS = 4
EXPERTS_PER_GROUP = 8
N_EXPERTS = N_EXPERT_GROUPS * EXPERTS_PER_GROUP
EXPERT_TOP_K = 2
SMALL_WIDTH = 256

LANE = 128
V7X_VMEM_BYTES = 64 * 1024 * 1024
VMEM_LIMIT = V7X_VMEM_BYTES - 12 * 1024 * 1024


MM_ROWS, MM_COLS = 1024, 1024
NORM_ROWS = 512
ATTN_Q, ATTN_K = 512, 512
SB_Q, SB_K = 512, 256
POOL_ROWS = 512
MERGE_COLS = 512
MOE_ROWS = 512
REGROUP_ROWS = 256


def _cparams(*sem, vmem=VMEM_LIMIT):
    return pltpu.CompilerParams(dimension_semantics=sem, vmem_limit_bytes=vmem)


def _tile(n, pref):
    t = min(n, pref)
    assert n % t == 0, (n, pref)
    return t


def _mod_body(c_ref, w_ref, b_ref, o_ref):
    c = c_ref[...]
    ca = c * jax.nn.sigmoid(c)
    acc = jnp.dot(ca.astype(BF16), w_ref[0].astype(BF16), preferred_element_type=F32)
    o_ref[0] = acc + b_ref[0]


def _modulation(c_pad, w_mod, b_mod):
    depth, d, n = w_mod.shape
    tn = _tile(n, MM_COLS)
    return pl.pallas_call(
        _mod_body,
        grid=(depth, n // tn),
        in_specs=[
            pl.BlockSpec((8, d), lambda l, j: (0, 0)),
            pl.BlockSpec((1, d, tn), lambda l, j: (l, 0, j)),
            pl.BlockSpec((1, 1, tn), lambda l, j: (l, 0, j)),
        ],
        out_specs=pl.BlockSpec((1, 8, tn), lambda l, j: (l, 0, j)),
        out_shape=jax.ShapeDtypeStruct((depth, 8, n), F32),
        compiler_params=_cparams("parallel", "parallel"),
        name="adaln_modulation",
    )(c_pad, w_mod, b_mod.reshape(depth, 1, n))


REGROUP_CHUNK = 1024


def _regroup_body(w_ref, *out_refs, plans):
    total = w_ref.shape[2]

    def cols(a, b):
        a0 = (a // LANE) * LANE
        b1 = min(-(-b // LANE) * LANE, total)
        return w_ref[0, :, a0:b1][:, a - a0:b - a0]

    for o_ref, plan in zip(out_refs, plans):
        for dst, a, b, scale in plan:
            if a is None:
                o_ref[0, :, dst:b] = jnp.zeros((o_ref.shape[1], b - dst), o_ref.dtype)
                continue
            for c0 in range(0, b - a, REGROUP_CHUNK):
                c1 = min(c0 + REGROUP_CHUNK, b - a)
                v = cols(a + c0, a + c1)
                if scale != 1.0:
                    v = v * scale
                o_ref[0, :, dst + c0:dst + c1] = v.astype(o_ref.dtype)


def _regroup_w_in(w_in, plans, widths):
    depth, d, total = w_in.shape
    tr = _tile(d, REGROUP_ROWS)
    return pl.pallas_call(
        functools.partial(_regroup_body, plans=plans),
        grid=(depth, d // tr),
        in_specs=[pl.BlockSpec((1, tr, total), lambda l, i: (l, i, 0))],
        out_specs=[pl.BlockSpec((1, tr, w), lambda l, i: (l, i, 0)) for w in widths],
        out_shape=[jax.ShapeDtypeStruct((depth, d, w), BF16) for w in widths],
        compiler_params=_cparams("parallel", "parallel"),
        name="regroup_w_in",
    )(w_in)


def _add_moe(x_ref, y0_ref, y1_ref, rw_ref, gate_ref):
    rw = rw_ref[...]
    moe = rw[:, 0:1] * y0_ref[...].astype(F32) + rw[:, 1:2] * y1_ref[...].astype(F32)
    return x_ref[...] + gate_ref[0] * moe


def _moe_in_specs(tm, d, per_b):
    row = pl.BlockSpec((tm, d), lambda i: (i, 0))
    return [row, row, pl.BlockSpec((tm, EXPERT_TOP_K), lambda i: (i, 0)),
            pl.BlockSpec((1, 1, d), lambda i: (i // per_b, 0, 0))]


def _norm_small_body(*refs, has_moe):
    if has_moe:
        (x_ref, y0_ref, y1_ref, rw_ref, gate_ref, g_ref, sc_ref, sh_ref, w_ref,
         xo_ref, h_ref, s_ref) = refs
        x = _add_moe(x_ref, y0_ref, y1_ref, rw_ref, gate_ref)
        xo_ref[...] = x
    else:
        x_ref, g_ref, sc_ref, sh_ref, w_ref, h_ref, s_ref = refs
        x = x_ref[...]
    ms = jnp.mean(x * x, axis=-1, keepdims=True)
    y = x * lax.rsqrt(ms + NORM_EPS) * g_ref[...]
    h = y * (1.0 + sc_ref[0]) + sh_ref[0]
    h_ref[...] = h.astype(h_ref.dtype)
    s_ref[...] = jnp.dot(h.astype(BF16), w_ref[...], preferred_element_type=F32)


def _norm_small(x2, g, scale, shift, w_small, seq, moe_in=None):
    n, d = x2.shape
    tm = _tile(seq, NORM_ROWS)
    per_b = seq // tm
    ws = w_small.shape[1]
    row = pl.BlockSpec((tm, d), lambda i: (i, 0))
    has_moe = moe_in is not None
    outs = pl.pallas_call(
        functools.partial(_norm_small_body, has_moe=has_moe),
        grid=(n // tm,),
        in_specs=[row] + (_moe_in_specs(tm, d, per_b) if has_moe else []) + [
            pl.BlockSpec((1, d), lambda i: (0, 0)),
            pl.BlockSpec((1, 1, d), lambda i: (i // per_b, 0, 0)),
            pl.BlockSpec((1, 1, d), lambda i: (i // per_b, 0, 0)),
            pl.BlockSpec((d, ws), lambda i: (0, 0)),
        ],
        out_specs=([row] if has_moe else []) + [row, pl.BlockSpec((tm, ws), lambda i: (i, 0))],
        out_shape=([jax.ShapeDtypeStruct((n, d), F32)] if has_moe else []) + [
            jax.ShapeDtypeStruct((n, d), BF16),
            jax.ShapeDtypeStruct((n, ws), F32),
        ],
        compiler_params=_cparams("parallel"),
        name="norm_modulate_small_proj",
    )(x2, *(moe_in if has_moe else ()), g.reshape(1, d), scale, shift, w_small)
    return outs if has_moe else (x2, *outs)


def _mm_body(a_ref, w_ref, o_ref):
    o_ref[...] = jnp.dot(a_ref[...], w_ref[0], preferred_element_type=F32).astype(o_ref.dtype)


def _matmul(a, w, layer, out_dtype, tm=MM_ROWS, tn=MM_COLS, name="matmul"):
    m, k = a.shape
    n = w.shape[2]
    tm = _tile(m, tm)
    tn = _tile(n, tn)
    return pl.pallas_call(
        _mm_body,
        grid=(m // tm, n // tn),
        in_specs=[
            pl.BlockSpec((tm, k), lambda i, j: (i, 0)),
            pl.BlockSpec((1, k, tn), lambda i, j: (layer, 0, j)),
        ],
        out_specs=pl.BlockSpec((tm, tn), lambda i, j: (i, j)),
        out_shape=jax.ShapeDtypeStruct((m, n), out_dtype),
        compiler_params=_cparams("parallel", "parallel"),
        name=name,
    )(a, w)


def _mm_residual_body(a_ref, w_ref, x_ref, gate_ref, o_ref):
    acc = jnp.dot(a_ref[...], w_ref[...], preferred_element_type=F32)
    o_ref[...] = x_ref[...] + gate_ref[0] * acc


def _matmul_residual(a, w, x2, gate, seq, tm=MM_ROWS, tn=MM_COLS):
    m, k = a.shape
    n = w.shape[1]
    tm = _tile(seq, tm)
    tn = _tile(n, tn)
    per_b = seq // tm
    return pl.pallas_call(
        _mm_residual_body,
        grid=(m // tm, n // tn),
        in_specs=[
            pl.BlockSpec((tm, k), lambda i, j: (i, 0)),
            pl.BlockSpec((k, tn), lambda i, j: (0, j)),
            pl.BlockSpec((tm, tn), lambda i, j: (i, j)),
            pl.BlockSpec((1, 1, tn), lambda i, j: (i // per_b, 0, j)),
        ],
        out_specs=pl.BlockSpec((tm, tn), lambda i, j: (i, j)),
        out_shape=jax.ShapeDtypeStruct((m, n), F32),
        compiler_params=_cparams("parallel", "parallel"),
        name="out_proj_residual",
    )(a, w, x2, gate)


def _latent_norm(a_ref, g_ref):
    a = a_ref[...]
    ms = jnp.mean(a * a, axis=-1, keepdims=True)
    return (a * lax.rsqrt(ms + NORM_EPS) * g_ref[...]).astype(BF16)


def _mla_q_body(a_ref, g_ref, w_ref, cos_ref, sin_ref, nope_ref, rope_ref):
    acc = jnp.dot(_latent_norm(a_ref, g_ref), w_ref[...], preferred_element_type=F32)
    w = nope_ref.shape[1]
    nope_ref[...] = acc[:, :w].astype(BF16)
    rope_ref[...] = (acc[:, w:2 * w] * cos_ref[...] + acc[:, 2 * w:] * sin_ref[...]).astype(BF16)


def _mla_q(latent, col_block, g, w_q, cos_t, sin_t, seq):
    n = latent.shape[0]
    k = w_q.shape[0]
    w = w_q.shape[1] // 3
    tm = _tile(seq, MM_ROWS)
    per_b = seq // tm
    return pl.pallas_call(
        _mla_q_body,
        grid=(n // tm,),
        in_specs=[
            pl.BlockSpec((tm, k), lambda i: (i, col_block)),
            pl.BlockSpec((1, k), lambda i: (0, 0)),
            pl.BlockSpec((k, 3 * w), lambda i: (0, 0)),
            pl.BlockSpec((tm, w), lambda i: (i % per_b, 0)),
            pl.BlockSpec((tm, w), lambda i: (i % per_b, 0)),
        ],
        out_specs=[
            pl.BlockSpec((tm, w), lambda i: (i, 0)),
            pl.BlockSpec((tm, w), lambda i: (i, 0)),
        ],
        out_shape=[
            jax.ShapeDtypeStruct((n, w), BF16),
            jax.ShapeDtypeStruct((n, w), BF16),
        ],
        compiler_params=_cparams("parallel"),
        name="mla_q_up_rope",
    )(latent, g.reshape(1, k), w_q, cos_t, sin_t)


def _mla_kv_body(a_ref, g_ref, w_ref, o_ref):
    o_ref[...] = jnp.dot(_latent_norm(a_ref, g_ref), w_ref[...],
                         preferred_element_type=F32).astype(BF16)


def _mla_kv(latent, col_block, g, w_kv, seq):
    n = latent.shape[0]
    k, nout = w_kv.shape
    tm = _tile(seq, MM_ROWS)
    return pl.pallas_call(
        _mla_kv_body,
        grid=(n // tm,),
        in_specs=[
            pl.BlockSpec((tm, k), lambda i: (i, col_block)),
            pl.BlockSpec((1, k), lambda i: (0, 0)),
            pl.BlockSpec((k, nout), lambda i: (0, 0)),
        ],
        out_specs=pl.BlockSpec((tm, nout), lambda i: (i, 0)),
        out_shape=jax.ShapeDtypeStruct((n, nout), BF16),
        compiler_params=_cparams("parallel"),
        name="mla_kv_up",
    )(latent, g.reshape(1, k), w_kv)


def _softmax_attn_body(*refs, tq, tk, has_extra, has_bias):
    refs = list(refs)
    qm_ref = refs.pop(0)
    qe_ref = refs.pop(0) if has_extra else None
    km_ref = refs.pop(0)
    ke_ref = refs.pop(0) if has_extra else None
    v_ref = refs.pop(0)
    kb_ref = refs.pop(0) if has_bias else None
    o_ref, m_ref, acc_ref = refs
    qi = pl.program_id(1)
    m_ref[...] = jnp.full(m_ref.shape, -jnp.inf, F32)
    acc_ref[...] = jnp.zeros(acc_ref.shape, F32)
    chunks_per_q = tq // tk
    n_full = qi * chunks_per_q
    ones = jnp.ones((tk, HEAD_DIM), BF16)

    def head_chunk(h, k0, diag):
        lanes = slice(h * HEAD_DIM, (h + 1) * HEAD_DIM)
        q = qm_ref[0, :, lanes]
        k = km_ref[0, pl.ds(k0, tk), lanes]
        if has_extra:
            q = jnp.concatenate([q, qe_ref[0, :, lanes]], axis=-1)
            k = jnp.concatenate([k, ke_ref[0, pl.ds(k0, tk), :]], axis=-1)
        s = lax.dot_general(q, k, (((1,), (1,)), ((), ())), preferred_element_type=F32)
        if has_bias:
            s = s + kb_ref[0, h, :, pl.ds(k0, tk)]
        if diag is not None:
            row = lax.broadcasted_iota(jnp.int32, (tq, tk), 0)
            col = lax.broadcasted_iota(jnp.int32, (tq, tk), 1) + diag * tk
            s = jnp.where(row >= col, s, -jnp.inf)
        blocks = [s[:, j * LANE:(j + 1) * LANE] for j in range(tk // LANE)]
        mx = blocks[0]
        for blk in blocks[1:]:
            mx = jnp.maximum(mx, blk)
        m_prev = m_ref[h]
        m_new = jnp.maximum(m_prev, jnp.max(mx, axis=-1, keepdims=True))
        alpha = jnp.exp(m_prev - m_new)
        p = jnp.concatenate([jnp.exp(blk - m_new) for blk in blocks], axis=-1).astype(BF16)
        v_ext = jnp.concatenate([v_ref[0, pl.ds(k0, tk), lanes], ones], axis=-1)
        pv = jnp.dot(p, v_ext, preferred_element_type=F32)
        acc_ref[h, :, :HEAD_DIM] = alpha * acc_ref[h, :, :HEAD_DIM] + pv[:, :HEAD_DIM]
        acc_ref[h, :, HEAD_DIM:] = alpha * acc_ref[h, :, HEAD_DIM:] + pv[:, HEAD_DIM:]
        m_ref[h] = m_new

    def chunk(c, diag):
        k0 = pl.multiple_of(c * tk, tk)
        for h in range(N_HEADS):
            head_chunk(h, k0, diag)

    def full_chunk(c, carry):
        chunk(c, None)
        return carry

    lax.fori_loop(0, n_full, full_chunk, 0)
    for d in range(chunks_per_q):
        chunk(n_full + d, d)
    for h in range(N_HEADS):
        o_ref[0, :, h * HEAD_DIM:(h + 1) * HEAD_DIM] = (
            acc_ref[h, :, :HEAD_DIM] / acc_ref[h, :, HEAD_DIM:]).astype(o_ref.dtype)


def _softmax_attention(qm, q_blk, km, k_blk, v, v_blk, batch, seq, extra=None, key_bias=None):
    tq = _tile(seq, ATTN_Q)
    tk = _tile(seq, ATTN_K)
    width = N_HEADS * HEAD_DIM
    args = [qm]
    specs = [pl.BlockSpec((1, tq, width), lambda b, i: (b, i, q_blk))]
    if extra is not None:
        args.append(extra[0])
        specs.append(pl.BlockSpec((1, tq, width), lambda b, i: (b, i, 0)))
    args.append(km)
    specs.append(pl.BlockSpec((1, seq, width), lambda b, i: (b, 0, k_blk)))
    if extra is not None:
        args.append(extra[1])
        specs.append(pl.BlockSpec((1, seq, HEAD_DIM), lambda b, i: (b, 0, 0)))
    args.append(v)
    specs.append(pl.BlockSpec((1, seq, width), lambda b, i: (b, 0, v_blk)))
    if key_bias is not None:
        args.append(key_bias)
        specs.append(pl.BlockSpec((1, N_HEADS, 1, seq), lambda b, i: (b, 0, 0, 0)))

    return pl.pallas_call(
        functools.partial(_softmax_attn_body, tq=tq, tk=tk, has_extra=extra is not None,
                          has_bias=key_bias is not None),
        grid=(batch, seq // tq),
        in_specs=specs,
        out_specs=pl.BlockSpec((1, tq, width), lambda b, i: (b, i, 0)),
        out_shape=jax.ShapeDtypeStruct((batch, seq, width), BF16),
        scratch_shapes=[
            pltpu.VMEM((N_HEADS, tq, HEAD_DIM), F32),
            pltpu.VMEM((N_HEADS, tq, 2 * HEAD_DIM), F32),
        ],
        compiler_params=_cparams("parallel", "arbitrary"),
        name="causal_softmax_attention",
    )(*args)


SB_EXP_UNDERFLOW = 110.0


def _sb_attn_body(q_ref, k_ref, v_ref, tri_ref, o_ref, r_ref, acc_ref, *, tq, tk):
    qi = pl.program_id(1)
    r_ref[...] = jnp.zeros(r_ref.shape, F32)
    acc_ref[...] = jnp.zeros(acc_ref.shape, F32)
    chunks_per_q = tq // tk
    n_full = qi * chunks_per_q

    def head_chunk(h, k0, diag):
        lanes = slice(h * HEAD_DIM, (h + 1) * HEAD_DIM)
        z = lax.dot_general(q_ref[0, :, lanes], k_ref[0, pl.ds(k0, tk), lanes],
                            (((1,), (1,)), ((), ())), preferred_element_type=F32)
        sp = jnp.maximum(z, 0.0) + jnp.log(1.0 + jnp.exp(-jnp.abs(z)))
        log_beta = z - sp
        if diag is not None:
            row = lax.broadcasted_iota(jnp.int32, (tq, tk), 0)
            col = lax.broadcasted_iota(jnp.int32, (tq, tk), 1) + diag * tk
            strict = row > col
            sp = jnp.where(strict, sp, 0.0)
        hi = sp.astype(BF16)
        lo = (sp - hi.astype(F32)).astype(BF16)
        tri = tri_ref[...]
        e =jnp.dot(hi, tri, preferred_element_type=F32) + jnp.dot(lo, tri, preferred_element_type=F32)
        r = r_ref[h]
        a = jnp.concatenate(
            [jnp.exp(log_beta[:, j * LANE:(j + 1) * LANE] - e[:, j * LANE:(j + 1) * LANE] - r)
             for j in range(tk // LANE)], axis=-1)
        if diag is not None:
            a = jnp.where(strict, a, 0.0)
        acc_ref[h] += jnp.dot(a.astype(BF16), v_ref[0, pl.ds(k0, tk), lanes],
                              preferred_element_type=F32)
        r_ref[h] = r + e[:, tk:]

    def chunk(c, diag):
        k0 = pl.multiple_of(c * tk, tk)
        for h in range(N_HEADS):
            head_chunk(h, k0, diag)

    for d in reversed(range(chunks_per_q)):
        chunk(n_full + d, d)

    def more(carry):
        i, r_min = carry
        return jnp.logical_and(i < n_full, r_min < SB_EXP_UNDERFLOW)

    def earlier_chunk(carry):
        i, _ = carry
        chunk(n_full - 1 - i, None)
        return i + 1, jnp.min(r_ref[...])

    lax.while_loop(more, earlier_chunk, (jnp.int32(0), jnp.min(r_ref[...])))
    for h in range(N_HEADS):
        o_ref[0, :, h * HEAD_DIM:(h + 1) * HEAD_DIM] = acc_ref[h].astype(o_ref.dtype)


def _sb_tri(tk):
    j_src = np.arange(tk)[:, None]
    j_dst = np.arange(tk + LANE)[None, :]
    return jnp.asarray((j_src > j_dst) | (j_dst >= tk), dtype=BF16)


def _sb_attention(qkv, q_blk, k_blk, v_blk, batch, seq):
    tq = _tile(seq, SB_Q)
    tk = _tile(seq, SB_K)
    width = N_HEADS * HEAD_DIM
    return pl.pallas_call(
        functools.partial(_sb_attn_body, tq=tq, tk=tk),
        grid=(batch, seq // tq),
        in_specs=[
            pl.BlockSpec((1, tq, width), lambda b, i: (b, i, q_blk)),
            pl.BlockSpec((1, seq, width), lambda b, i: (b, 0, k_blk)),
            pl.BlockSpec((1, seq, width), lambda b, i: (b, 0, v_blk)),
            pl.BlockSpec((tk, tk + LANE), lambda b, i: (0, 0)),
        ],
        out_specs=pl.BlockSpec((1, tq, width), lambda b, i: (b, i, 0)),
        out_shape=jax.ShapeDtypeStruct((batch, seq, width), BF16),
        scratch_shapes=[
            pltpu.VMEM((N_HEADS, tq, HEAD_DIM), F32),
            pltpu.VMEM((N_HEADS, tq, HEAD_DIM), F32),
        ],
        compiler_params=_cparams("parallel", "arbitrary"),
        name="stick_breaking_attention",
    )(qkv, qkv, qkv, _sb_tri(tk))


def _pool_body(u_ref, halo_ref, w_ref, sc_ref, o_ref, ext_ref, *, ts):
    i = pl.program_id(1)
    halo = jnp.where(i > 0, halo_ref[0], 0.0)
    ext_ref[:POOL_HALO, :] = halo
    ext_ref[POOL_HALO:, :] = u_ref[0]
    pos = i * ts + lax.broadcasted_iota(jnp.int32, (ts, 1), 0)
    outs = []
    for g, win in enumerate(POOL_WINDOWS):
        lanes = slice(g * LANE, (g + 1) * LANE)
        tok = ext_ref[POOL_HALO:, lanes]
        tot = tok
        for back in range(1, win):
            tot = tot + ext_ref[POOL_HALO - back:POOL_HALO - back + ts, lanes]
        cnt = jnp.minimum(pos + 1, win).astype(F32)
        pooled = (tot / cnt - tok).astype(BF16)
        outs.append(jnp.dot(pooled, w_ref[g], preferred_element_type=F32))
    y = jnp.concatenate(outs, axis=-1) * sc_ref[...]
    o_ref[0] = y.astype(o_ref.dtype)


def _pool_mixer(latent3, col_block, w_pool, pool_scale, batch, seq):
    width = len(POOL_WINDOWS) * LANE
    ts = _tile(seq, POOL_ROWS)
    halo_per_tile = ts // POOL_HALO
    return pl.pallas_call(
        functools.partial(_pool_body, ts=ts),
        grid=(batch, seq // ts),
        in_specs=[
            pl.BlockSpec((1, ts, width), lambda b, i: (b, i, col_block)),
            pl.BlockSpec((1, POOL_HALO, width),
                         lambda b, i: (b, jnp.maximum(i * halo_per_tile - 1, 0), col_block)),
            pl.BlockSpec((len(POOL_WINDOWS), LANE, LANE), lambda b, i: (0, 0, 0)),
            pl.BlockSpec((1, width), lambda b, i: (0, 0)),
        ],
        out_specs=pl.BlockSpec((1, ts, width), lambda b, i: (b, i, 0)),
        out_shape=jax.ShapeDtypeStruct((batch, seq, width), BF16),
        scratch_shapes=[pltpu.VMEM((ts + POOL_HALO, width), F32)],
        compiler_params=_cparams("parallel", "arbitrary"),
        name="multiscale_pool",
    )(latent3, latent3, w_pool.astype(BF16), pool_scale.reshape(1, width))


def _merge_body(h_ref, y0, y1, y2, y3, g0, g1, g2, g3, wb_ref, o_ref):
    h = h_ref[...]
    acc = None
    for n, (y, wg) in enumerate(((y0, g0), (y1, g1), (y2, g2), (y3, g3))):
        gate = jax.nn.sigmoid(jnp.dot(h, wg[0], preferred_element_type=F32))
        t = gate * jnp.dot(y[...], wb_ref[n], preferred_element_type=F32)
        acc = t if acc is None else acc + t
    o_ref[...] = acc.astype(o_ref.dtype)


def _merge(h, ys, w_gates, layer, w_branch):
    n, k = h.shape
    d = w_branch.shape[2]
    tm = _tile(n, MM_ROWS)
    tn = _tile(d, MERGE_COLS)
    nj = d // tn
    y_spec = pl.BlockSpec((tm, BRANCH_WIDTH), lambda i, j: (i, 0))

    def g_spec(b):
        return pl.BlockSpec((1, k, tn), lambda i, j: (layer, 0, b * nj + j))

    return pl.pallas_call(
        _merge_body,
        grid=(n // tm, nj),
        in_specs=[pl.BlockSpec((tm, k), lambda i, j: (i, 0))] + [y_spec] * N_BRANCHES
        + [g_spec(b) for b in range(N_BRANCHES)]
        + [pl.BlockSpec((N_BRANCHES, BRANCH_WIDTH, tn), lambda i, j: (0, 0, j))],
        out_specs=pl.BlockSpec((tm, tn), lambda i, j: (i, j)),
        out_shape=jax.ShapeDtypeStruct((n, d), BF16),
        compiler_params=_cparams("parallel", "parallel"),
        name="gated_branch_merge",
    )(h, *ys, *([w_gates] * N_BRANCHES), w_branch)


def _expert_changed(be_ref):
    blk = pl.program_id(1)
    return (blk == 0) | (be_ref[blk] != be_ref[jnp.maximum(blk - 1, 0)])


def _moe_up_body(be_ref, nb_ref, x_ref, wg_ref, wu_ref, o_ref, wg_s, wu_s):
    blk = pl.program_id(1)

    @pl.when(_expert_changed(be_ref))
    def _():
        wg_s[...] = wg_ref[0, 0].astype(BF16)
        wu_s[...] = wu_ref[0, 0].astype(BF16)

    @pl.when(blk < nb_ref[0])
    def _():
        x = x_ref[...]
        g = jnp.dot(x, wg_s[...], preferred_element_type=F32)
        u = jnp.dot(x, wu_s[...], preferred_element_type=F32)
        o_ref[...] = (g * jax.nn.sigmoid(g) * u).astype(o_ref.dtype)

    @pl.when(blk >= nb_ref[0])
    def _():
        o_ref[...] = jnp.zeros(o_ref.shape, o_ref.dtype)


def _moe_down_body(be_ref, nb_ref, a_ref, wd_ref, o_ref, wd_s):
    blk = pl.program_id(1)

    @pl.when(_expert_changed(be_ref))
    def _():
        wd_s[...] = wd_ref[0, 0].astype(BF16)

    @pl.when(blk < nb_ref[0])
    def _():
        o_ref[...] = jnp.dot(a_ref[...], wd_s[...], preferred_element_type=F32).astype(o_ref.dtype)

    @pl.when(blk >= nb_ref[0])
    def _():
        o_ref[...] = jnp.zeros(o_ref.shape, o_ref.dtype)


def _moe_ffn(x_buf, block_expert, n_used, w_gate, w_up, w_down, layer, tb):
    n_rows, d = x_buf.shape
    f = w_gate.shape[3]
    tf, tn = f, d
    n_blocks = n_rows // tb
    up_vmem = V7X_VMEM_BYTES - 4 * 1024 * 1024

    def used(i, nb):
        return jnp.minimum(i, nb[0] - 1)

    def expert_tile(rows, cols):
        return pl.BlockSpec((1, 1, rows, cols), lambda j, i, be, nb: (layer, be[i], 0, j))

    act = pl.pallas_call(
        _moe_up_body,
        grid_spec=pltpu.PrefetchScalarGridSpec(
            num_scalar_prefetch=2,
            grid=(f // tf, n_blocks),
            in_specs=[pl.BlockSpec((tb, d), lambda j, i, be, nb: (used(i, nb), 0)),
                      expert_tile(d, tf), expert_tile(d, tf)],
            out_specs=pl.BlockSpec((tb, tf), lambda j, i, be, nb: (i, j)),
            scratch_shapes=[pltpu.VMEM((d, tf), BF16), pltpu.VMEM((d, tf), BF16)],
        ),
        out_shape=jax.ShapeDtypeStruct((n_rows, f), BF16),
        compiler_params=_cparams("arbitrary", "arbitrary", vmem=up_vmem),
        name="expert_gate_up_swiglu",
    )(block_expert, n_used, x_buf, w_gate, w_up)
    return pl.pallas_call(
        _moe_down_body,
        grid_spec=pltpu.PrefetchScalarGridSpec(
            num_scalar_prefetch=2,
            grid=(d // tn, n_blocks),
            in_specs=[pl.BlockSpec((tb, f), lambda j, i, be, nb: (used(i, nb), 0)),
                      expert_tile(f, tn)],
            out_specs=pl.BlockSpec((tb, tn), lambda j, i, be, nb: (i, j)),
            scratch_shapes=[pltpu.VMEM((f, tn), BF16)],
        ),
        out_shape=jax.ShapeDtypeStruct((n_rows, d), BF16),
        compiler_params=_cparams("arbitrary", "arbitrary"),
        name="expert_down",
    )(block_expert, n_used, act, w_down)


def _final_norm_body(x_ref, y0_ref, y1_ref, rw_ref, gate_ref, g_ref, o_ref):
    x = _add_moe(x_ref, y0_ref, y1_ref, rw_ref, gate_ref)
    ms = jnp.mean(x * x, axis=-1, keepdims=True)
    o_ref[...] = x * lax.rsqrt(ms + NORM_EPS) * g_ref[...]


def _final_norm(x2, moe_in, g, seq):
    n, d = x2.shape
    tm = _tile(seq, NORM_ROWS)
    row = pl.BlockSpec((tm, d), lambda i: (i, 0))
    return pl.pallas_call(
        _final_norm_body,
        grid=(n // tm,),
        in_specs=[row] + _moe_in_specs(tm, d, seq // tm) + [pl.BlockSpec((1, d), lambda i: (0, 0))],
        out_specs=row,
        out_shape=jax.ShapeDtypeStruct((n, d), F32),
        compiler_params=_cparams("parallel"),
        name="final_rmsnorm",
    )(x2, *moe_in, g.reshape(1, d))


def _rope_tables(seq):
    half = MLA_ROPE_DIM // 2
    inv = jnp.power(ROPE_BASE, -2.0 * jnp.arange(half, dtype=F32) / MLA_ROPE_DIM)
    ang = jnp.arange(seq).astype(F32)[:, None] * inv[None, :]
    cos, sin = jnp.cos(ang), jnp.sin(ang)
    c64 = jnp.concatenate([cos, cos], axis=-1)
    s64 = jnp.concatenate([-sin, sin], axis=-1)
    return c64, s64


def _swap_halves(w):
    half = w.shape[-1] // 2
    return jnp.concatenate([w[..., half:], w[..., :half]], axis=-1)


def _pad_lanes(a, width):
    return jnp.pad(a, [(0, 0)] * (a.ndim - 1) + [(0, width - a.shape[-1])])


def _mixer_layer(x2, h, small, layer, batch, seq, prm):
    n, d = x2.shape
    (w_qkv, w_lat, w_gates, b_forget, g_q, w_q, g_kv, w_kv, w_pool, pool_scale, w_branch, w_out,
     gate1) = prm
    c64, s64 = _rope_tables(seq)

    qkv = _matmul(h, w_qkv, layer, BF16, name="fox_sb_qkv_proj").reshape(batch, seq, -1)
    latent = _matmul(h, w_lat, layer, F32, tn=w_lat.shape[2] // 2,
                     name="latent_pool_proj")

    fox_f = small[:, 2 * MLA_ROPE_DIM:2 * MLA_ROPE_DIM + N_HEADS].reshape(batch, seq, N_HEADS)
    log_f_cum = jnp.cumsum(jax.nn.log_sigmoid(fox_f + b_forget), axis=1)
    k_bias = -log_f_cum.transpose(0, 2, 1)[:, :, None, :]
    nh = N_HEADS
    y_fox = _softmax_attention(qkv, 0, qkv, 1, qkv, 2, batch, seq, key_bias=k_bias)

    y_sb = _sb_attention(qkv, 3, 4, 5, batch, seq)

    cos_t = jnp.tile(_pad_lanes(c64, HEAD_DIM), (1, nh))
    sin_t = jnp.tile(_pad_lanes(s64, HEAD_DIM), (1, nh))
    q_nope, q_rope = _mla_q(latent, 0, g_q, w_q, cos_t, sin_t, seq)
    kv = _mla_kv(latent, 4, g_kv, w_kv, seq).reshape(batch, seq, -1)
    kr = small[:, :MLA_ROPE_DIM] * jnp.tile(c64, (batch, 1)) \
        + small[:, MLA_ROPE_DIM:2 * MLA_ROPE_DIM] * jnp.tile(s64, (batch, 1))
    kr = _pad_lanes(kr.astype(BF16), HEAD_DIM).reshape(batch, seq, HEAD_DIM)
    y_mla = _softmax_attention(q_nope.reshape(batch, seq, -1), 0, kv, 0, kv, 1, batch, seq,
                               extra=(q_rope.reshape(batch, seq, -1), kr))

    y_pool = _pool_mixer(latent.reshape(batch, seq, -1), 1, w_pool, pool_scale, batch, seq)

    ys = [y.reshape(n, BRANCH_WIDTH) for y in (y_fox, y_sb, y_mla, y_pool)]
    merged = _merge(h, ys, w_gates, layer, w_branch)
    return _matmul_residual(merged, w_out, x2, gate1, seq)


def _moe_layer(h, logits, gate2, b_rg, b_re, w_gate, w_up, w_down, layer):
    n, d = h.shape
    tb = MOE_ROWS
    g_logits = logits[:, :N_EXPERT_GROUPS] + b_rg
    g_idx = jnp.argmax(g_logits, axis=-1).astype(jnp.int32)[:, None]
    g_top = jnp.max(g_logits, axis=-1)
    p_group = jnp.exp(g_top - jax.nn.logsumexp(g_logits, axis=-1))
    e_logits = logits[:, N_EXPERT_GROUPS:N_EXPERT_GROUPS + N_EXPERTS] + b_re
    lane = jnp.arange(N_EXPERTS, dtype=jnp.int32)[None, :]
    in_group = (lane // EXPERTS_PER_GROUP) == g_idx
    probs = jax.nn.softmax(jnp.where(in_group, e_logits, -jnp.inf), axis=-1)
    i1 = jnp.argmax(probs, axis=-1).astype(jnp.int32)[:, None]
    p1 = jnp.max(probs, axis=-1, keepdims=True)
    rest = jnp.where(in_group & (lane != i1), probs, -jnp.inf)
    i2 = jnp.argmax(rest, axis=-1).astype(jnp.int32)[:, None]
    p2 = jnp.max(rest, axis=-1, keepdims=True)
    top_p = jnp.concatenate([p1, p2], axis=-1)
    weights = p_group[:, None] * top_p / jnp.sum(top_p, axis=-1, keepdims=True)

    expert_id = jnp.concatenate([i1, i2], axis=-1).reshape(-1)
    m = expert_id.shape[0]

    order = jnp.argsort(expert_id).astype(jnp.int32)
    rank = jnp.argsort(order).astype(jnp.int32)
    onehot = expert_id[:, None] == jnp.arange(N_EXPERTS, dtype=jnp.int32)[None, :]
    counts = jnp.sum(onehot, axis=0, dtype=jnp.int32)
    padded = ((counts + tb - 1) // tb) * tb
    start = jnp.cumsum(counts) - counts
    pend = jnp.cumsum(padded)
    shift = pend - padded - start
    n_rows = m + N_EXPERTS * tb
    n_blocks = n_rows // tb
    blk = jnp.arange(n_blocks, dtype=jnp.int32)
    n_used = (pend[-1] // tb).astype(jnp.int32)
    block_expert = jnp.minimum(
        jnp.sum(pend[None, :] <= (blk * tb)[:, None], axis=1), N_EXPERTS - 1).astype(jnp.int32)
    last_used = block_expert[jnp.maximum(n_used - 1, 0)]
    block_expert = jnp.where(blk < n_used, block_expert, last_used)

    rows = jnp.arange(n_rows, dtype=jnp.int32).reshape(n_blocks, tb)
    src_sorted = rows - shift[block_expert][:, None]
    valid = (src_sorted < (start + counts)[block_expert][:, None]) & (blk < n_used)[:, None]
    src_assign = order.at[jnp.clip(src_sorted, 0, m - 1).reshape(-1)].get(mode="promise_in_bounds")
    valid = valid.reshape(-1)
    row_tok = jnp.where(valid, src_assign // EXPERT_TOP_K, rows.reshape(-1) % n)
    dest = (rank + jnp.sum(jnp.where(onehot, shift[None, :], 0), axis=1)).reshape(n, EXPERT_TOP_K)

    x_buf = h.at[row_tok].get(mode="promise_in_bounds")
    y = _moe_ffn(x_buf, block_expert, n_used.reshape(1), w_gate, w_up, w_down, layer, tb)
    return (y.at[dest[:, 0]].get(mode="promise_in_bounds"),
            y.at[dest[:, 1]].get(mode="promise_in_bounds"), weights, gate2)


def kernel(x, c, w_mod, b_mod, g_norm1, g_norm2, w_in, b_forget, g_q_norm, w_uq, g_kv_norm, w_ukv, w_pool, pool_scale, w_branch, w_out, w_route_group, b_route_group, w_route_expert, b_route_expert, w_gate, w_up, w_down, g_final):
    batch, seq, d = x.shape
    depth = w_mod.shape[0]
    n = batch * seq
    nh = N_HEADS

    c_pad = jnp.pad(c, ((0, 8 - batch), (0, 0)))
    mod = _modulation(c_pad, w_mod, b_mod)[:, :batch]

    hq = nh * HEAD_DIM
    o_fox, o_f = 0, 3 * hq
    o_sb = o_f + nh
    o_cq = o_sb + 3 * hq
    o_ckv = o_cq + MLA_Q_LORA
    o_kr = o_ckv + MLA_KV_LORA
    o_pool = o_kr + MLA_ROPE_DIM
    o_gate = o_pool + len(POOL_WINDOWS) * LANE
    attn_scale = HEAD_DIM ** -0.5
    mla_scale = (MLA_NOPE_DIM + MLA_ROPE_DIM) ** -0.5

    half = MLA_ROPE_DIM // 2
    plans = (
        [(0, o_fox, o_fox + hq, attn_scale), (hq, o_fox + hq, o_fox + 3 * hq, 1.0),
         (3 * hq, o_sb, o_sb + hq, attn_scale), (4 * hq, o_sb + hq, o_sb + 3 * hq, 1.0)],
        [(0, o_cq, o_ckv, 1.0), (MLA_Q_LORA, o_pool, o_gate, 1.0),
         (MLA_Q_LORA + o_gate - o_pool, o_ckv, o_kr, 1.0)],
        [(0, o_gate, w_in.shape[2], 1.0)],
        [(0, None, SMALL_WIDTH, 0.0), (0, o_kr, o_pool, 1.0), (MLA_ROPE_DIM, o_kr + half, o_pool, 1.0),
         (MLA_ROPE_DIM + half, o_kr, o_kr + half, 1.0), (2 * MLA_ROPE_DIM, o_f, o_sb, 1.0)],
    )
    w_qkv, w_lat, w_gates, w_small1 = _regroup_w_in(
        w_in, plans, (6 * hq, o_kr - o_cq + o_gate - o_pool, w_in.shape[2] - o_gate, SMALL_WIDTH))

    x2 = x.reshape(n, d)
    pending = None
    for l in range(depth):
        m6 = mod[l].reshape(batch, 6, 1, d)
        shift1, scale1, gate1, shift2, scale2, gate2 = [m6[:, t] for t in range(6)]

        wq = w_uq[l].reshape(MLA_Q_LORA, nh, MLA_NOPE_DIM + MLA_ROPE_DIM) * mla_scale
        wq_rope = wq[:, :, MLA_NOPE_DIM:]
        w_q = jnp.concatenate([
            wq[:, :, :MLA_NOPE_DIM].reshape(MLA_Q_LORA, -1),
            _pad_lanes(wq_rope, HEAD_DIM).reshape(MLA_Q_LORA, -1),
            _pad_lanes(_swap_halves(wq_rope), HEAD_DIM).reshape(MLA_Q_LORA, -1)], axis=1).astype(BF16)
        wkv = w_ukv[l].reshape(MLA_KV_LORA, nh, 2 * HEAD_DIM)
        w_kv = jnp.concatenate([wkv[:, :, :HEAD_DIM].reshape(MLA_KV_LORA, -1),
                                wkv[:, :, HEAD_DIM:].reshape(MLA_KV_LORA, -1)], axis=1).astype(BF16)

        x2, h, small = _norm_small(x2, g_norm1[l], scale1, shift1, w_small1[l], seq,
                                   moe_in=pending)
        prm = (w_qkv, w_lat, w_gates, b_forget[l], g_q_norm[l], w_q, g_kv_norm[l], w_kv,
               w_pool[l], pool_scale[l], w_branch[l].astype(BF16), w_out[l].astype(BF16), gate1)
        x2 = _mixer_layer(x2, h, small, l, batch, seq, prm)

        w_small2 = _pad_lanes(jnp.concatenate([w_route_group[l], w_route_expert[l]], axis=1),
                              SMALL_WIDTH).astype(BF16)
        _, h, logits = _norm_small(x2, g_norm2[l], scale2, shift2, w_small2, seq)
        pending = _moe_layer(h, logits, gate2, b_route_group[l], b_route_expert[l],
                             w_gate, w_up, w_down, l)
    return _final_norm(x2, pending, g_final, seq).reshape(batch, seq, d)
```
